```python
import jax
import jax.numpy as jnp
from jax import lax
import numpy as np

D_MODEL = 1024
BATCH = 8
SEQ = 4096
DEPTH = 1

N_META = 16
HEAD_DIM = 64
ATTN_Q_HEADS = 8
ATTN_KV_HEADS = 2
ATTN_GROUP = ATTN_Q_HEADS // ATTN_KV_HEADS
WINDOW = 128
BLOCK = 128
ROPE_THETA = 500000.0
ROPE_DIM = HEAD_DIM // 4
RWKV_HEADS = 8
RWKV_HEAD = 64
RWKV_DIM = RWKV_HEADS * RWKV_HEAD
DECAY_LORA = 64
AAA_LORA = 64
GATE_LORA = 160
RWKV_LN_EPS = 64e-5
D_FF = -(-8 * D_MODEL // (3 * 256)) * 256
Q_W = ATTN_Q_HEADS * HEAD_DIM
KV_W = ATTN_KV_HEADS * HEAD_DIM
ATTN_PROJ = Q_W + 2 * KV_W
RWKV_PROJ = 3 * RWKV_DIM + DECAY_LORA + AAA_LORA + GATE_LORA
D_IN = ATTN_PROJ + RWKV_PROJ + 2 * D_MODEL
RMS_EPS = 1e-6
NEG_INF = -1e30

kernel_name = 'hybrid_swa_sink_rwkv7_gated_block'


def rms_norm(x, g):
    xf = x.astype(jnp.float32)
    y = xf * lax.rsqrt(jnp.mean(xf * xf, axis=-1, keepdims=True) + RMS_EPS)
    return (y * g.astype(jnp.float32)).astype(x.dtype)


def partial_rope(t, pos):
    half = ROPE_DIM // 2
    inv_freq = jnp.power(jnp.float32(ROPE_THETA), -jnp.arange(half, dtype=jnp.float32) * (2.0 / ROPE_DIM))
    ang = pos.astype(jnp.float32)[:, None] * inv_freq[None, :]
    cos = jnp.cos(ang)[None, :, None, :]
    sin = jnp.sin(ang)[None, :, None, :]
    tf = t.astype(jnp.float32)
    t1 = tf[..., :half]
    t2 = tf[..., half:ROPE_DIM]
    out = jnp.concatenate([t1 * cos - t2 * sin, t2 * cos + t1 * sin, tf[..., ROPE_DIM:]], axis=-1)
    return out.astype(t.dtype)


def sliding_window_sink_attention(q, k, v, sinks):
    B, L = q.shape[0], q.shape[1]
    pad = BLOCK - N_META
    n_blk = (L + pad) // BLOCK

    def blockify(t):
        t = jnp.pad(t, ((0, 0), (pad, 0), (0, 0), (0, 0)))
        return t.reshape(B, n_blk, BLOCK, t.shape[2], t.shape[3])

    def prev_block(t):
        return jnp.pad(t, ((0, 0), (1, 0), (0, 0), (0, 0), (0, 0)))[:, :-1]

    qb = blockify(q).reshape(B, n_blk, BLOCK, ATTN_KV_HEADS, ATTN_GROUP, HEAD_DIM)
    kb = blockify(k)
    vb = blockify(v)
    k_band = jnp.concatenate([prev_block(kb), kb], axis=2)
    v_band = jnp.concatenate([prev_block(vb), vb], axis=2)
    k_meta = k[:, :N_META]
    v_meta = v[:, :N_META]

    scale = HEAD_DIM ** -0.5
    s_band = jnp.einsum('bnqhgd,bnkhd->bhgnqk', qb, k_band).astype(jnp.float32) * scale
    s_meta = jnp.einsum('bnqhgd,bmhd->bhgnqm', qb, k_meta).astype(jnp.float32) * scale

    q_pos = jnp.arange(n_blk * BLOCK).reshape(n_blk, BLOCK) - pad
    k_pos = (jnp.arange(n_blk)[:, None] - 1) * BLOCK + jnp.arange(2 * BLOCK)[None, :] - pad
    qp = q_pos[:, :, None]
    kp = k_pos[:, None, :]
    band_ok = (kp >= N_META) & (kp <= qp) & (qp - kp < WINDOW)
    meta_ok = jnp.arange(N_META)[None, None, :] <= qp
    s_band = jnp.where(band_ok[None, None, None], s_band, NEG_INF)
    s_meta = jnp.where(meta_ok[None, None, None], s_meta, NEG_INF)
    sink = sinks.astype(jnp.float32).reshape(ATTN_KV_HEADS, ATTN_GROUP)[None, :, :, None, None, None]
    sink = jnp.broadcast_to(sink, s_band.shape[:-1] + (1,))

    probs = jax.nn.softmax(jnp.concatenate([s_meta, s_band, sink], axis=-1), axis=-1)
    p_meta = probs[..., :N_META].astype(v.dtype)
    p_band = probs[..., N_META:N_META + 2 * BLOCK].astype(v.dtype)
    out = (jnp.einsum('bhgnqm,bmhd->bnqhgd', p_meta, v_meta)
           + jnp.einsum('bhgnqk,bnkhd->bnqhgd', p_band, v_band))
    return out.reshape(B, n_blk * BLOCK, Q_W)[:, pad:]


def token_shift(t):
    return jnp.pad(t, ((0, 0), (1, 0), (0, 0)))[:, :-1]


def wkv7_scan(r, decay, k, v, aa, bb):
    B, L, H, N = r.shape

    def step(S, inp):
        r_t, w_t, k_t, v_t, a_t, b_t = inp
        sa = jnp.einsum('bhvk,bhk->bhv', S, a_t)
        S = S * w_t[:, :, None, :] + sa[..., None] * b_t[:, :, None, :] + v_t[..., None] * k_t[:, :, None, :]
        y = jnp.einsum('bhvk,bhk->bhv', S, r_t)
        return S, y

    xs = (jnp.moveaxis(r, 1, 0), jnp.moveaxis(decay, 1, 0), jnp.moveaxis(k, 1, 0),
          jnp.moveaxis(v, 1, 0), jnp.moveaxis(aa, 1, 0), jnp.moveaxis(bb, 1, 0))
    S0 = jnp.zeros((B, H, N, N), jnp.float32)
    _, ys = lax.scan(step, S0, xs)
    return jnp.moveaxis(ys, 0, 1)


def rwkv7_time_mix(p, mix, w0, w2, a0, a2, g2, k_k, k_a, r_k, ln_w, ln_b):
    B, L = p.shape[0], p.shape[1]
    f32 = jnp.float32
    pf = p.astype(f32)
    pf = pf + (token_shift(pf) - pf) * mix.astype(f32)
    o1, o2, o3 = RWKV_DIM, 2 * RWKV_DIM, 3 * RWKV_DIM
    o4 = o3 + DECAY_LORA
    o5 = o4 + AAA_LORA
    r = pf[..., :o1]
    k = pf[..., o1:o2]
    v = pf[..., o2:o3]
    dw = pf[..., o3:o4]
    da = pf[..., o4:o5]
    dg = pf[..., o5:]
    w = -jax.nn.softplus(-(w0.astype(f32) + jnp.tanh(dw) @ w2.astype(f32))) - 0.5
    a = jax.nn.sigmoid(a0.astype(f32) + da @ a2.astype(f32))
    g = jax.nn.sigmoid(dg) @ g2.astype(f32)
    hs = (B, L, RWKV_HEADS, RWKV_HEAD)
    kk = (k * k_k.astype(f32)).reshape(hs)
    kk = kk / jnp.maximum(jnp.sqrt(jnp.sum(kk * kk, axis=-1, keepdims=True)), 1e-12)
    k = k * (1.0 + (a - 1.0) * k_a.astype(f32))
    r = r.reshape(hs)
    k = k.reshape(hs)
    v = v.reshape(hs)
    a = a.reshape(hs)
    decay = jnp.exp(-jnp.exp(w)).reshape(hs)
    y = wkv7_scan(r, decay, k, v, -kk, kk * a)
    mean = jnp.mean(y, axis=-1, keepdims=True)
    var = jnp.mean(jnp.square(y - mean), axis=-1, keepdims=True)
    y = ((y - mean) * lax.rsqrt(var + RWKV_LN_EPS) * ln_w.astype(f32).reshape(RWKV_HEADS, RWKV_HEAD)
         + ln_b.astype(f32).reshape(RWKV_HEADS, RWKV_HEAD))
    y = y + jnp.sum(r * k * r_k.astype(f32), axis=-1, keepdims=True) * v
    return (y.reshape(B, L, RWKV_DIM) * g).astype(p.dtype)


def setup_inputs(seed: int = 0) -> dict:
    key = jax.random.key(seed)
    ks = jax.random.split(key, 25)
    f32 = jnp.float32

    def nrm(k, shape, scale):
        return jax.random.normal(k, shape, f32) * scale

    def unif(k, shape, lo, hi):
        return jax.random.uniform(k, shape, f32, lo, hi)

    Dp = DEPTH
    return {
        'x': nrm(ks[0], (BATCH, SEQ, D_MODEL), 1.0),
        'meta_tokens': nrm(ks[1], (N_META, D_MODEL), 1.0),
        'norm_mix_g': 1.0 + nrm(ks[2], (Dp, D_MODEL), 0.02),
        'w_in': nrm(ks[3], (Dp, D_MODEL, D_IN), D_MODEL ** -0.5),
        'b_in': nrm(ks[4], (Dp, D_IN), 0.02),
        'attn_sinks': nrm(ks[5], (Dp, ATTN_Q_HEADS), 1.0),
        'rwkv_mix': unif(ks[6], (Dp, RWKV_PROJ), 0.0, 1.0),
        'rwkv_w0': unif(ks[7], (Dp, RWKV_DIM), -6.0, -1.0),
        'rwkv_w2': nrm(ks[8], (Dp, DECAY_LORA, RWKV_DIM), 0.1 * DECAY_LORA ** -0.5),
        'rwkv_a0': nrm(ks[9], (Dp, RWKV_DIM), 0.1),
        'rwkv_a2': nrm(ks[10], (Dp, AAA_LORA, RWKV_DIM), 0.5 * AAA_LORA ** -0.5),
        'rwkv_g2': nrm(ks[11], (Dp, GATE_LORA, RWKV_DIM), GATE_LORA ** -0.5),
        'rwkv_k_k': 0.85 + nrm(ks[12], (Dp, RWKV_DIM), 0.02),
        'rwkv_k_a': 1.0 + nrm(ks[13], (Dp, RWKV_DIM), 0.02),
        'rwkv_r_k': -0.04 + nrm(ks[14], (Dp, RWKV_HEADS, RWKV_HEAD), 0.02),
        'rwkv_ln_w': 1.0 + nrm(ks[15], (Dp, RWKV_DIM), 0.02),
        'rwkv_ln_b': nrm(ks[16], (Dp, RWKV_DIM), 0.02),
        'w_br_attn': nrm(ks[17], (Dp, Q_W, D_MODEL), Q_W ** -0.5),
        'w_br_rwkv': nrm(ks[18], (Dp, RWKV_DIM, D_MODEL), RWKV_DIM ** -0.5),
        'w_o': nrm(ks[19], (Dp, D_MODEL, D_MODEL), D_MODEL ** -0.5),
        'norm_ffn_g': 1.0 + nrm(ks[20], (Dp, D_MODEL), 0.02),
        'w_ffn_gate': nrm(ks[21], (Dp, D_MODEL, D_FF), D_MODEL ** -0.5),
        'w_ffn_up': nrm(ks[22], (Dp, D_MODEL, D_FF), D_MODEL ** -0.5),
        'w_ffn_down': nrm(ks[23], (Dp, D_FF, D_MODEL), D_FF ** -0.5),
        'norm_final_g': 1.0 + nrm(ks[24], (D_MODEL,), 0.02),
    }


def reference(x, meta_tokens, norm_mix_g, w_in, b_in, attn_sinks, rwkv_mix, rwkv_w0, rwkv_w2,
              rwkv_a0, rwkv_a2, rwkv_g2, rwkv_k_k, rwkv_k_a, rwkv_r_k, rwkv_ln_w, rwkv_ln_b,
              w_br_attn, w_br_rwkv, w_o, norm_ffn_g, w_ffn_gate, w_ffn_up, w_ffn_down,
              norm_final_g):
    B = x.shape[0]
    meta = jnp.broadcast_to(meta_tokens.astype(x.dtype)[None], (B, N_META, D_MODEL))
    h = jnp.concatenate([meta, x], axis=1)
    L = h.shape[1]
    pos = jnp.arange(L, dtype=jnp.int32)
    for layer in range(DEPTH):
        u = rms_norm(h, norm_mix_g[layer])
        proj = u @ w_in[layer] + b_in[layer]
        p_attn = proj[..., :ATTN_PROJ]
        p_rwkv = proj[..., ATTN_PROJ:ATTN_PROJ + RWKV_PROJ]
        gates = jax.nn.sigmoid(proj[..., ATTN_PROJ + RWKV_PROJ:].astype(jnp.float32)).astype(h.dtype)
        q = p_attn[..., :Q_W].reshape(B, L, ATTN_Q_HEADS, HEAD_DIM)
        k = p_attn[..., Q_W:Q_W + KV_W].reshape(B, L, ATTN_KV_HEADS, HEAD_DIM)
        v = p_attn[..., Q_W + KV_W:].reshape(B, L, ATTN_KV_HEADS, HEAD_DIM)
        q = partial_rope(q, pos)
        k = partial_rope(k, pos)
        y_attn = sliding_window_sink_attention(q, k, v, attn_sinks[layer])
        y_rwkv = rwkv7_time_mix(p_rwkv, rwkv_mix[layer], rwkv_w0[layer], rwkv_w2[layer],
                                rwkv_a0[layer], rwkv_a2[layer], rwkv_g2[layer], rwkv_k_k[layer],
                                rwkv_k_a[layer], rwkv_r_k[layer], rwkv_ln_w[layer],
                                rwkv_ln_b[layer])
        merged = (gates[..., :D_MODEL] * (y_attn @ w_br_attn[layer])
                  + gates[..., D_MODEL:] * (y_rwkv @ w_br_rwkv[layer]))
        h = h + merged @ w_o[layer]
        f = rms_norm(h, norm_ffn_g[layer])
        h = h + (jax.nn.silu(f @ w_ffn_gate[layer]) * (f @ w_ffn_up[layer])) @ w_ffn_down[layer]
    return rms_norm(h, norm_final_g)[:, N_META:]
```

```python
import functools

import jax
import jax.numpy as jnp
import numpy as np
from jax import lax
from jax.experimental import pallas as pl
from jax.experimental.pallas import tpu as pltpu

F32 = jnp.float32
BF16 = jnp.bfloat16

D_MODEL = 1024
N_META = 16
HEAD_DIM = 64
Q_HEADS = 8
KV_HEADS = 2
WINDOW = 128
ROPE_THETA = 500000.0
ROPE_DIM = HEAD_DIM // 4
RWKV_HEADS = 8
RWKV_DIM = RWKV_HEADS * HEAD_DIM
DECAY_LORA = 64
AAA_LORA = 64
GATE_LORA = 160
RWKV_LN_EPS = 64e-5
D_FF = 2816
Q_W = Q_HEADS * HEAD_DIM
KV_W = KV_HEADS * HEAD_DIM
ATTN_PROJ = Q_W + 2 * KV_W
RWKV_PROJ = 3 * RWKV_DIM + DECAY_LORA + AAA_LORA + GATE_LORA
RMS_EPS = 1e-6
NEG_INF = -1e30

LANES = 128
QKV_W = Q_W + 4 * KV_W
GATE_PAD = 256
RW_W = 3 * RWKV_DIM + LANES + GATE_PAD
CHUNK = 64
GROUP = 4 * HEAD_DIM
N_GROUPS = RWKV_DIM // GROUP
VMEM_LIMIT = 56 * 1024 * 1024


def _dot(a, b):
    return jnp.dot(a, b, preferred_element_type=F32)


def _dot_nt(a, b):
    return lax.dot_general(a, b, (((1,), (1,)), ((), ())), preferred_element_type=F32)


def _dot_tn(a, b):
    return lax.dot_general(a, b, (((0,), (0,)), ((), ())), preferred_element_type=F32)


def _sigmoid(x):
    return 1.0 / (1.0 + jnp.exp(-x))


def _const_spec(shape):
    nd = len(shape)
    return pl.BlockSpec(shape, lambda *_: (0,) * nd, pipeline_mode=pl.Buffered(1))


def _inproj_kernel(x_ref, g_ref, wq_ref, wr_ref, wg_ref, bq_ref, br_ref, bg_ref,
                   cos_ref, s1_ref, s2_ref, qkv_ref, p_ref, gate_ref):
    x = x_ref[...]
    ms = jnp.mean(x * x, axis=-1, keepdims=True)
    u = (x * lax.rsqrt(ms + RMS_EPS) * g_ref[...]).astype(BF16)

    cos = cos_ref[...]
    s1 = s1_ref[...]
    s2 = s2_ref[...]
    n_rope = (Q_W + 2 * KV_W) // LANES
    for j in range(QKV_W // LANES):
        sl = slice(j * LANES, (j + 1) * LANES)
        t = _dot(u, wq_ref[:, sl]) + bq_ref[:, sl]
        if j < n_rope:
            t = t * cos + pltpu.roll(t, 8, 1) * s1 + pltpu.roll(t, LANES - 8, 1) * s2
        qkv_ref[:, sl] = t.astype(BF16)

    step = 384
    for j in range(RW_W // step):
        sl = slice(j * step, (j + 1) * step)
        p_ref[:, sl] = _dot(u, wr_ref[:, sl]) + br_ref[:, sl]

    step = 512
    for j in range(2 * D_MODEL // step):
        sl = slice(j * step, (j + 1) * step)
        gate_ref[:, sl] = _sigmoid(_dot(u, wg_ref[:, sl]) + bg_ref[:, sl]).astype(BF16)


def _inproj(x2, g, wq, wr, wg, bq, br, bg, cos, s1, s2, tm):
    n = x2.shape[0]
    seq_tiles = cos.shape[0] // tm
    row = lambda i: (i, 0)
    pos = lambda i: (i % seq_tiles, 0)
    return pl.pallas_call(
        _inproj_kernel,
        grid=(n // tm,),
        in_specs=[
            pl.BlockSpec((tm, D_MODEL), row),
            _const_spec((1, D_MODEL)),
            _const_spec((D_MODEL, QKV_W)),
            _const_spec((D_MODEL, RW_W)),
            _const_spec((D_MODEL, 2 * D_MODEL)),
            _const_spec((1, QKV_W)),
            _const_spec((1, RW_W)),
            _const_spec((1, 2 * D_MODEL)),
            pl.BlockSpec((tm, LANES), pos),
            pl.BlockSpec((tm, LANES), pos),
            pl.BlockSpec((tm, LANES), pos),
        ],
        out_specs=[
            pl.BlockSpec((tm, QKV_W), row),
            pl.BlockSpec((tm, RW_W), row),
            pl.BlockSpec((tm, 2 * D_MODEL), row),
        ],
        out_shape=[
            jax.ShapeDtypeStruct((n, QKV_W), BF16),
            jax.ShapeDtypeStruct((n, RW_W), F32),
            jax.ShapeDtypeStruct((n, 2 * D_MODEL), BF16),
        ],
        compiler_params=pltpu.CompilerParams(
            dimension_semantics=("arbitrary",), vmem_limit_bytes=VMEM_LIMIT),
        name="inproj",
    )(x2, g, wq, wr, wg, bq, br, bg, cos, s1, s2)


def _attn_kernel(sink_ref, q_ref, kvc_ref, kvp_ref, kvm_ref, o_ref):
    n = pl.program_id(1)
    blk = WINDOW
    lane = lax.broadcasted_iota(jnp.int32, (1, LANES), 1)
    m_lo = (lane < HEAD_DIM).astype(BF16)
    m_hi = (lane >= HEAD_DIM).astype(BF16)

    kv_band = jnp.concatenate([kvp_ref[...], kvc_ref[...]], axis=0)
    kv_meta = kvm_ref[...]

    qi = lax.broadcasted_iota(jnp.int32, (blk, 2 * blk), 0)
    c = lax.broadcasted_iota(jnp.int32, (blk, 2 * blk), 1)
    has_prev = jnp.minimum(n, 1)
    band_ok = jnp.where(c < blk, jnp.where(c > qi, has_prev, 0),
                        jnp.where(c - blk <= qi, 1, 0)) > 0
    lane_m = lax.broadcasted_iota(jnp.int32, (blk, 2 * N_META), 1)
    first = lane_m < N_META

    def halves(kv, g):
        k_plain, k_swap = kv[:, 0:LANES], kv[:, LANES:2 * LANES]
        v_plain, v_swap = kv[:, 2 * LANES:3 * LANES], kv[:, 3 * LANES:4 * LANES]
        if g == 0:
            k_lo, k_hi, v_lo, v_hi = k_plain, k_swap, v_plain, v_swap
        else:
            k_lo, k_hi, v_lo, v_hi = k_swap, k_plain, v_swap, v_plain
        kk = jnp.concatenate([k_lo * m_lo, k_hi * m_hi], axis=0)
        vv = jnp.concatenate([v_lo * m_lo, v_hi * m_hi], axis=0)
        return kk, vv

    for g in range(KV_HEADS):
        kb, vb = halves(kv_band, g)
        km, vm = halves(kv_meta, g)
        for s in (2 * g, 2 * g + 1):
            q = q_ref[:, s * LANES:(s + 1) * LANES]
            sb = _dot_nt(q, kb)
            sm = _dot_nt(q, km)
            sink_a = sink_ref[2 * s]
            sink_b = sink_ref[2 * s + 1]
            sa = jnp.where(band_ok, sb[:, :2 * blk], NEG_INF)
            sbb = jnp.where(band_ok, sb[:, 2 * blk:], NEG_INF)
            mx_a = jnp.maximum(
                jnp.maximum(jnp.max(sa, axis=-1, keepdims=True),
                            jnp.max(jnp.where(first, sm, NEG_INF), axis=-1, keepdims=True)),
                sink_a)
            mx_b = jnp.maximum(
                jnp.maximum(jnp.max(sbb, axis=-1, keepdims=True),
                            jnp.max(jnp.where(first, NEG_INF, sm), axis=-1, keepdims=True)),
                sink_b)
            pa = jnp.exp(sa - mx_a)
            pb = jnp.exp(sbb - mx_b)
            pm = jnp.exp(sm - jnp.where(first, mx_a, mx_b))
            den_a = (jnp.sum(pa, axis=-1, keepdims=True)
                     + jnp.sum(jnp.where(first, pm, 0.0), axis=-1, keepdims=True)
                     + jnp.exp(sink_a - mx_a))
            den_b = (jnp.sum(pb, axis=-1, keepdims=True)
                     + jnp.sum(jnp.where(first, 0.0, pm), axis=-1, keepdims=True)
                     + jnp.exp(sink_b - mx_b))
            inv_a = 1.0 / den_a
            inv_b = 1.0 / den_b
            p_band = jnp.concatenate([pa * inv_a, pb * inv_b], axis=1).astype(BF16)
            p_meta = (pm * jnp.where(first, inv_a, inv_b)).astype(BF16)
            o = _dot(p_band, vb) + _dot(p_meta, vm)
            o_ref[:, s * LANES:(s + 1) * LANES] = o.astype(BF16)


def _attn(sinks, qkv, qkv_meta, batch, seq):
    nb = seq // WINDOW
    n = batch * seq
    kvw = 4 * KV_W
    return pl.pallas_call(
        _attn_kernel,
        grid=(batch, nb),
        in_specs=[
            pl.BlockSpec(memory_space=pltpu.SMEM),
            pl.BlockSpec((WINDOW, Q_W), lambda b, i: (b * nb + i, 0)),
            pl.BlockSpec((WINDOW, kvw), lambda b, i: (b * nb + i, 1)),
            pl.BlockSpec((WINDOW, kvw), lambda b, i: (jnp.maximum(b * nb + i - 1, 0), 1)),
            pl.BlockSpec((N_META, kvw), lambda b, i: (0, 1)),
        ],
        out_specs=pl.BlockSpec((WINDOW, Q_W), lambda b, i: (b * nb + i, 0)),
        out_shape=jax.ShapeDtypeStruct((n, Q_W), BF16),
        compiler_params=pltpu.CompilerParams(
            dimension_semantics=("arbitrary", "arbitrary"), vmem_limit_bytes=VMEM_LIMIT),
        name="attn",
    )(sinks, qkv, qkv, qkv, qkv_meta)


def _split2(x):
    hi = x.astype(BF16)
    lo = (x - hi.astype(F32)).astype(BF16)
    return hi, lo


def _rwkv_kernel(p_ref, pprev_ref, s0_ref, mix_ref, w0_ref, a0_ref, w2a_ref, g2_ref,
                 kk_ref, ka_ref, rk_ref, lnw_ref, lnb_ref,
                 tri_ref, bd_ref, strict_ref, incl_ref, eye_ref,
                 y_ref, sfin_ref, pbuf, s_scr):
    c = pl.program_id(1)

    @pl.when(c == 0)
    def _():
        pbuf[0:8, :] = pprev_ref[...]
        s_scr[...] = s0_ref[...]

    p = p_ref[...]
    pbuf[8:8 + CHUNK, :] = p
    psh = pbuf[7:7 + CHUNK, :]
    pbuf[0:8, :] = p[CHUNK - 8:CHUNK, :]
    pf = p + (psh - p) * mix_ref[...]

    o1, o2, o3 = RWKV_DIM, 2 * RWKV_DIM, 3 * RWKV_DIM
    r = pf[:, :o1]
    k = pf[:, o1:o2]
    v = pf[:, o2:o3]
    dwa = pf[:, o3:o3 + LANES]
    dg = pf[:, o3 + LANES:]

    lane = lax.broadcasted_iota(jnp.int32, (CHUNK, LANES), 1)
    z = jnp.where(lane < DECAY_LORA, jnp.tanh(dwa), dwa).astype(BF16)
    wa = _dot(z, w2a_ref[...])
    zw = -(w0_ref[...] + wa[:, :o1])
    softplus = jnp.maximum(zw, 0.0) + jnp.log(1.0 + jnp.exp(-jnp.abs(zw)))
    logw = -jnp.exp(-softplus - 0.5)
    a = _sigmoid(a0_ref[...] + wa[:, o1:])
    gate = _dot(_sigmoid(dg).astype(BF16), g2_ref[...])

    bd = bd_ref[...]
    kkv = k * kk_ref[...]
    kp = k * (1.0 + (a - 1.0) * ka_ref[...])
    rkk = r * kp * rk_ref[...]

    def seg_sums(xs):
        parts = []
        for x in xs:
            parts.extend(_split2(x))
        out = _dot(jnp.concatenate(parts, axis=0), bd)
        return [out[2 * i * CHUNK:(2 * i + 1) * CHUNK] + out[(2 * i + 1) * CHUNK:(2 * i + 2) * CHUNK]
                for i in range(len(xs))]

    def grp(x, gi):
        return x[:, gi * GROUP:(gi + 1) * GROUP]

    sq = kkv * kkv
    sums = seg_sums([grp(sq, 0), grp(sq, 1), grp(rkk, 0), grp(rkk, 1)])
    ss = jnp.concatenate(sums[0:2], axis=1)
    bonus = jnp.concatenate(sums[2:4], axis=1)
    kkn = kkv / jnp.maximum(jnp.sqrt(ss), 1e-12)
    aa = -kkn
    bb = kkn * a

    tri = tri_ref[...]
    strict = strict_ref[...]
    incl = incl_ref[...]
    eye = eye_ref[...]

    def blk(x):
        xb = x.astype(BF16)
        return jnp.concatenate([xb, xb, xb, xb], axis=0) * bd

    ys = []
    for gi in range(N_GROUPS):
        lw = grp(logw, gi)
        hi = lw.astype(BF16)
        r1 = lw - hi.astype(F32)
        mid = r1.astype(BF16)
        lo = (r1 - mid.astype(F32)).astype(BF16)
        cs = _dot(tri, jnp.concatenate([hi, mid, lo], axis=1))
        lp = cs[:, :GROUP] + cs[:, GROUP:2 * GROUP] + cs[:, 2 * GROUP:]
        lpc = lp[CHUNK - 1:CHUNK, :]
        e_neg = jnp.exp(-lp)
        e_pos = jnp.exp(lp)
        e_end = jnp.exp(lpc - lp)
        rg, kg, vg, ag, bg = grp(r, gi), grp(kp, gi), grp(v, gi), grp(aa, gi), grp(bb, gi)
        at = ag * jnp.exp(lp - lw)
        bt = bg * e_neg
        kt = kg * e_neg
        rt = rg * e_pos
        bh = bg * e_end
        kh = kg * e_end
        pc = jnp.exp(lpc)

        ar = jnp.concatenate([at, rt], axis=0).astype(BF16)
        sb = _dot_nt(ar, blk(bt))
        sk = _dot_nt(ar, blk(kt))
        a_ab = sb[:CHUNK] * strict
        a_rb = sb[CHUNK:] * incl
        a_ak = sk[:CHUNK] * strict
        a_rk = sk[CHUNK:] * incl

        pw = _dot(a_ab.astype(BF16), blk(a_ab))
        minv = eye + a_ab
        for j in range(1, 6):
            if j < 5:
                out = _dot(pw.astype(BF16), jnp.concatenate([blk(pw), blk(minv)], axis=1))
                pw = out[:, :GROUP]
                minv = minv + out[:, GROUP:]
            else:
                minv = minv + _dot(pw.astype(BF16), blk(minv))

        av = _dot(a_ak.astype(BF16), blk(vg))
        wu = _dot(minv.astype(BF16), jnp.concatenate([blk(at), blk(av)], axis=1))
        w = wu[:, :GROUP]
        u0 = wu[:, GROUP:]

        s_old = s_scr[gi]
        s_b = s_old.astype(BF16)
        u = _dot_nt(w.astype(BF16), s_b) + u0
        y = (_dot_nt(rt.astype(BF16), s_b)
             + _dot(jnp.concatenate([a_rb, a_rk], axis=1).astype(BF16),
                    jnp.concatenate([blk(u), blk(vg)], axis=0)))
        upd = _dot_tn(jnp.concatenate([u, vg], axis=0).astype(BF16),
                      jnp.concatenate([bh, kh], axis=0).astype(BF16))
        s_scr[gi] = s_old * pc + upd * bd.astype(F32)
        ys.append(y)

    means = seg_sums(ys)
    ds = [ys[i] - means[i] * (1.0 / HEAD_DIM) for i in range(N_GROUPS)]
    vars_ = seg_sums([d * d for d in ds])
    yn = jnp.concatenate(
        [ds[i] * lax.rsqrt(vars_[i] * (1.0 / HEAD_DIM) + RWKV_LN_EPS) for i in range(N_GROUPS)],
        axis=1)
    yn = yn * lnw_ref[...] + lnb_ref[...]
    y_ref[...] = ((yn + bonus * v) * gate).astype(BF16)

    @pl.when(c == pl.num_programs(1) - 1)
    def _():
        sfin_ref[...] = s_scr[...]


def _rwkv(p, pprev, s0, vecs, w2a, g2p, consts, batch, n_chunks):
    mix, w0, a0, k_k, k_a, r_k, ln_w, ln_b = vecs
    tri, bd, strict, incl, eye = consts
    n = batch * n_chunks * CHUNK
    vec = _const_spec((1, RWKV_DIM))
    return pl.pallas_call(
        _rwkv_kernel,
        grid=(batch, n_chunks),
        in_specs=[
            pl.BlockSpec((CHUNK, RW_W), lambda b, c: (b * n_chunks + c, 0)),
            _const_spec((8, RW_W)),
            _const_spec((N_GROUPS, GROUP, GROUP)),
            _const_spec((1, RW_W)),
            vec, vec,
            _const_spec((LANES, 2 * RWKV_DIM)),
            _const_spec((GATE_PAD, RWKV_DIM)),
            vec, vec, vec, vec, vec,
            _const_spec((CHUNK, CHUNK)),
            _const_spec((GROUP, GROUP)),
            _const_spec((CHUNK, GROUP)),
            _const_spec((CHUNK, GROUP)),
            _const_spec((CHUNK, GROUP)),
        ],
        out_specs=[
            pl.BlockSpec((CHUNK, RWKV_DIM), lambda b, c: (b * n_chunks + c, 0)),
            pl.BlockSpec((N_GROUPS, GROUP, GROUP), lambda b, c: (0, 0, 0)),
        ],
        out_shape=[
            jax.ShapeDtypeStruct((n, RWKV_DIM), BF16),
            jax.ShapeDtypeStruct((N_GROUPS, GROUP, GROUP), F32),
        ],
        scratch_shapes=[
            pltpu.VMEM((8 + CHUNK, RW_W), F32),
            pltpu.VMEM((N_GROUPS, GROUP, GROUP), F32),
        ],
        compiler_params=pltpu.CompilerParams(
            dimension_semantics=("arbitrary", "arbitrary"), vmem_limit_bytes=VMEM_LIMIT),
        name="rwkv",
    )(p, pprev, s0, mix, w0, a0, w2a, g2p, k_k, k_a, r_k, ln_w, ln_b,
      tri, bd, strict, incl, eye)


def _post_kernel(ya_ref, yr_ref, gate_ref, x_ref, wba_ref, wbr_ref, wo_ref, gf_ref,
                 wg_ref, wu_ref, wd_ref, gn_ref, o_ref):
    gates = gate_ref[...].astype(F32)
    merged = (gates[:, :D_MODEL] * _dot(ya_ref[...], wba_ref[...])
              + gates[:, D_MODEL:] * _dot(yr_ref[...], wbr_ref[...]))
    h = x_ref[...] + _dot(merged.astype(BF16), wo_ref[...])
    ms = jnp.mean(h * h, axis=-1, keepdims=True)
    f = (h * lax.rsqrt(ms + RMS_EPS) * gf_ref[...]).astype(BF16)
    gt = _dot(f, wg_ref[...])
    up = _dot(f, wu_ref[...])
    act = (gt * _sigmoid(gt) * up).astype(BF16)
    h = h + _dot(act, wd_ref[...])
    ms = jnp.mean(h * h, axis=-1, keepdims=True)
    o_ref[...] = h * lax.rsqrt(ms + RMS_EPS) * gn_ref[...]


def _post(ya, yr, gates, x2, wba, wbr, wo, gf, wg, wu, wd, gn, tm):
    n = x2.shape[0]
    row = lambda i: (i, 0)
    return pl.pallas_call(
        _post_kernel,
        grid=(n // tm,),
        in_specs=[
            pl.BlockSpec((tm, Q_W), row),
            pl.BlockSpec((tm, RWKV_DIM), row),
            pl.BlockSpec((tm, 2 * D_MODEL), row),
            pl.BlockSpec((tm, D_MODEL), row),
            _const_spec((Q_W, D_MODEL)),
            _const_spec((RWKV_DIM, D_MODEL)),
            _const_spec((D_MODEL, D_MODEL)),
            _const_spec((1, D_MODEL)),
            _const_spec((D_MODEL, D_FF)),
            _const_spec((D_MODEL, D_FF)),
            _const_spec((D_FF, D_MODEL)),
            _const_spec((1, D_MODEL)),
        ],
        out_specs=pl.BlockSpec((tm, D_MODEL), row),
        out_shape=jax.ShapeDtypeStruct((n, D_MODEL), F32),
        compiler_params=pltpu.CompilerParams(
            dimension_semantics=("arbitrary",), vmem_limit_bytes=VMEM_LIMIT),
        name="post",
    )(ya, yr, gates, x2, wba, wbr, wo, gf, wg, wu, wd, gn)


def _rope_tables(pos):
    half = ROPE_DIM // 2
    inv_freq = jnp.power(jnp.float32(ROPE_THETA),
                         -jnp.arange(half, dtype=F32) * (2.0 / ROPE_DIM))
    ang = pos.astype(F32)[:, None] * inv_freq[None, :]
    cos, sin = jnp.cos(ang), jnp.sin(ang)
    n = pos.shape[0]
    one = jnp.ones((n, HEAD_DIM - ROPE_DIM), F32)
    zero8 = jnp.zeros((n, half), F32)
    zero48 = jnp.zeros((n, HEAD_DIM - ROPE_DIM), F32)
    c_head = jnp.concatenate([cos, cos, one], axis=1)
    s1_head = jnp.concatenate([zero8, sin, zero48], axis=1)
    s2_head = jnp.concatenate([-sin, zero8, zero48], axis=1)
    dup = lambda t: jnp.concatenate([t, t], axis=1)
    return dup(c_head), dup(s1_head), dup(s2_head)


def _rwkv_constants():
    t = np.arange(CHUNK)
    tri = (t[None, :] <= t[:, None]).astype(np.float32)
    hh = np.arange(GROUP) // HEAD_DIM
    bd = (hh[:, None] == hh[None, :]).astype(np.float32)
    s = np.arange(GROUP) % CHUNK
    strict = (s[None, :] < t[:, None]).astype(np.float32)
    incl = (s[None, :] <= t[:, None]).astype(np.float32)
    eye = (s[None, :] == t[:, None]).astype(np.float32)
    return (jnp.asarray(tri, BF16), jnp.asarray(bd, BF16), jnp.asarray(strict),
            jnp.asarray(incl), jnp.asarray(eye))


def kernel(x, meta_tokens, norm_mix_g, w_in, b_in, attn_sinks, rwkv_mix, rwkv_w0, rwkv_w2, rwkv_a0, rwkv_a2, rwkv_g2, rwkv_k_k, rwkv_k_a, rwkv_r_k, rwkv_ln_w, rwkv_ln_b, w_br_attn, w_br_rwkv, w_o, norm_ffn_g, w_ffn_gate, w_ffn_up, w_ffn_down, norm_final_g):
    batch, seq, _ = x.shape
    layer = 0
    x2 = x.reshape(batch * seq, D_MODEL)

    w = w_in[layer]
    b = b_in[layer]
    scale = HEAD_DIM ** -0.5
    h64 = HEAD_DIM
    kq, kk0, kk1 = Q_W, Q_W + h64, Q_W + 2 * h64
    vq = Q_W + KV_W

    def qkv_cols(t):
        return jnp.concatenate(
            [t[..., :Q_W] * scale,
             t[..., kq:kq + KV_W],
             t[..., kk0:kk1], t[..., kq:kk0],
             t[..., vq:vq + KV_W],
             t[..., vq + h64:vq + 2 * h64], t[..., vq:vq + h64]], axis=-1)

    def rw_cols(t, n_lead):
        pad = jnp.zeros(t.shape[:n_lead] + (GATE_PAD - GATE_LORA,), t.dtype)
        return jnp.concatenate([t, pad], axis=-1)

    wq = qkv_cols(w).astype(BF16)
    bq = qkv_cols(b)[None]
    wr = rw_cols(w[:, ATTN_PROJ:ATTN_PROJ + RWKV_PROJ], 1).astype(BF16)
    br = rw_cols(b[ATTN_PROJ:ATTN_PROJ + RWKV_PROJ], 0)[None]
    wg = w[:, ATTN_PROJ + RWKV_PROJ:].astype(BF16)
    bg = b[ATTN_PROJ + RWKV_PROJ:][None]
    g_mix = norm_mix_g[layer][None]

    mix = rw_cols(rwkv_mix[layer], 0)[None]
    zl = jnp.zeros((DECAY_LORA, RWKV_DIM), F32)
    w2a = jnp.concatenate(
        [jnp.concatenate([rwkv_w2[layer], zl], axis=1),
         jnp.concatenate([zl, rwkv_a2[layer]], axis=1)], axis=0).astype(BF16)
    g2p = jnp.concatenate(
        [rwkv_g2[layer], jnp.zeros((GATE_PAD - GATE_LORA, RWKV_DIM), F32)], axis=0).astype(BF16)
    vecs = (mix, rwkv_w0[layer][None], rwkv_a0[layer][None], rwkv_k_k[layer][None],
            rwkv_k_a[layer][None], rwkv_r_k[layer].reshape(1, RWKV_DIM),
            rwkv_ln_w[layer][None], rwkv_ln_b[layer][None])
    consts = _rwkv_constants()

    cos_m, s1_m, s2_m = _rope_tables(jnp.arange(N_META, dtype=jnp.int32))
    qkv_m, p_m, _ = _inproj(meta_tokens, g_mix, wq, wr, wg, bq, br, bg,
                            cos_m, s1_m, s2_m, N_META)
    p_m_pad = jnp.concatenate([jnp.zeros((CHUNK - N_META, RW_W), F32), p_m], axis=0)
    zero_state = jnp.zeros((N_GROUPS, GROUP, GROUP), F32)
    _, s_meta = _rwkv(p_m_pad, jnp.zeros((8, RW_W), F32), zero_state, vecs, w2a, g2p,
                      consts, 1, 1)

    cos, s1, s2 = _rope_tables(N_META + jnp.arange(seq, dtype=jnp.int32))
    tm_in = min(512, seq)
    qkv, p, gates = _inproj(x2, g_mix, wq, wr, wg, bq, br, bg, cos, s1, s2, tm_in)
    y_attn = _attn(attn_sinks[layer], qkv, qkv_m, batch, seq)
    y_rwkv, _ = _rwkv(p, p_m[N_META - 8:], s_meta, vecs, w2a, g2p, consts,
                      batch, seq // CHUNK)
    out = _post(y_attn, y_rwkv, gates, x2,
                w_br_attn[layer].astype(BF16), w_br_rwkv[layer].astype(BF16),
                w_o[layer].astype(BF16), norm_ffn_g[layer][None],
                w_ffn_gate[layer].astype(BF16), w_ffn_up[layer].astype(BF16),
                w_ffn_down[layer].astype(BF16), norm_final_g[None], min(256, seq))
    return out.reshape(batch, seq, D_MODEL)
```

```python
import functools

import jax
import jax.numpy as jnp
import numpy as np
from jax import lax
from jax.experimental import pallas as pl
from jax.experimental.pallas import tpu as pltpu

F32 = jnp.float32
BF16 = jnp.bfloat16

D_MODEL = 1024
N_META = 16
HEAD_DIM = 64
Q_HEADS = 8
KV_HEADS = 2
WINDOW = 128
ROPE_THETA = 500000.0
ROPE_DIM = HEAD_DIM // 4
RWKV_HEADS = 8
RWKV_DIM = RWKV_HEADS * HEAD_DIM
DECAY_LORA = 64
AAA_LORA = 64
GATE_LORA = 160
RWKV_LN_EPS = 64e-5
D_FF = 2816
Q_W = Q_HEADS * HEAD_DIM
KV_W = KV_HEADS * HEAD_DIM
ATTN_PROJ = Q_W + 2 * KV_W
RWKV_PROJ = 3 * RWKV_DIM + DECAY_LORA + AAA_LORA + GATE_LORA
RMS_EPS = 1e-6
NEG_INF = -1e30

LANES = 128
QKV_W = Q_W + 4 * KV_W
GATE_PAD = 256
RW_W = 3 * RWKV_DIM + LANES + GATE_PAD
CHUNK = 64
GROUP = 4 * HEAD_DIM
N_GROUPS = RWKV_DIM // GROUP
RWKV_CHUNKS_PER_STEP = 4
VMEM_LIMIT = 56 * 1024 * 1024


def _dot(a, b):
    return jnp.dot(a, b, preferred_element_type=F32)


def _dot_nt(a, b):
    return lax.dot_general(a, b, (((1,), (1,)), ((), ())), preferred_element_type=F32)


def _dot_tn(a, b):
    return lax.dot_general(a, b, (((0,), (0,)), ((), ())), preferred_element_type=F32)


def _sigmoid(x):
    return 1.0 / (1.0 + jnp.exp(-x))


def _const_spec(shape):
    nd = len(shape)
    return pl.BlockSpec(shape, lambda *_: (0,) * nd, pipeline_mode=pl.Buffered(1))


def _inproj_kernel(x_ref, g_ref, wq_ref, wr_ref, wg_ref, bq_ref, br_ref, bg_ref,
                   cos_ref, s1_ref, s2_ref, qkv_ref, p_ref, gate_ref):
    x = x_ref[...]
    ms = jnp.mean(x * x, axis=-1, keepdims=True)
    u = (x * lax.rsqrt(ms + RMS_EPS) * g_ref[...]).astype(BF16)

    cos = cos_ref[...]
    s1 = s1_ref[...]
    s2 = s2_ref[...]
    n_rope = (Q_W + 2 * KV_W) // LANES
    for j in range(QKV_W // LANES):
        sl = slice(j * LANES, (j + 1) * LANES)
        t = _dot(u, wq_ref[:, sl]) + bq_ref[:, sl]
        if j < n_rope:
            t = t * cos + pltpu.roll(t, 8, 1) * s1 + pltpu.roll(t, LANES - 8, 1) * s2
        qkv_ref[:, sl] = t.astype(BF16)

    step = 384
    for j in range(RW_W // step):
        sl = slice(j * step, (j + 1) * step)
        p_ref[:, sl] = _dot(u, wr_ref[:, sl]) + br_ref[:, sl]

    step = 512
    for j in range(2 * D_MODEL // step):
        sl = slice(j * step, (j + 1) * step)
        gate_ref[:, sl] = _sigmoid(_dot(u, wg_ref[:, sl]) + bg_ref[:, sl]).astype(BF16)


def _inproj(x2, g, wq, wr, wg, bq, br, bg, cos, s1, s2, tm):
    n = x2.shape[0]
    seq_tiles = cos.shape[0] // tm
    row = lambda i: (i, 0)
    pos = lambda i: (i % seq_tiles, 0)
    return pl.pallas_call(
        _inproj_kernel,
        grid=(n // tm,),
        in_specs=[
            pl.BlockSpec((tm, D_MODEL), row),
            _const_spec((1, D_MODEL)),
            _const_spec((D_MODEL, QKV_W)),
            _const_spec((D_MODEL, RW_W)),
            _const_spec((D_MODEL, 2 * D_MODEL)),
            _const_spec((1, QKV_W)),
            _const_spec((1, RW_W)),
            _const_spec((1, 2 * D_MODEL)),
            pl.BlockSpec((tm, LANES), pos),
            pl.BlockSpec((tm, LANES), pos),
            pl.BlockSpec((tm, LANES), pos),
        ],
        out_specs=[
            pl.BlockSpec((tm, QKV_W), row),
            pl.BlockSpec((tm, RW_W), row),
            pl.BlockSpec((tm, 2 * D_MODEL), row),
        ],
        out_shape=[
            jax.ShapeDtypeStruct((n, QKV_W), BF16),
            jax.ShapeDtypeStruct((n, RW_W), F32),
            jax.ShapeDtypeStruct((n, 2 * D_MODEL), BF16),
        ],
        compiler_params=pltpu.CompilerParams(
            dimension_semantics=("arbitrary",), vmem_limit_bytes=VMEM_LIMIT),
        name="inproj",
    )(x2, g, wq, wr, wg, bq, br, bg, cos, s1, s2)


def _attn_kernel(sink_ref, q_ref, kvc_ref, kvp_ref, kvm_ref, o_ref):
    n = pl.program_id(1)
    blk = WINDOW
    lane = lax.broadcasted_iota(jnp.int32, (1, LANES), 1)
    m_lo = (lane < HEAD_DIM).astype(BF16)
    m_hi = (lane >= HEAD_DIM).astype(BF16)

    kv_band = jnp.concatenate([kvp_ref[...], kvc_ref[...]], axis=0)
    kv_meta = kvm_ref[...]

    qi = lax.broadcasted_iota(jnp.int32, (blk, 2 * blk), 0)
    c = lax.broadcasted_iota(jnp.int32, (blk, 2 * blk), 1)
    has_prev = jnp.minimum(n, 1)
    band_ok = jnp.where(c < blk, jnp.where(c > qi, has_prev, 0),
                        jnp.where(c - blk <= qi, 1, 0)) > 0
    lane_m = lax.broadcasted_iota(jnp.int32, (blk, 2 * N_META), 1)
    first = lane_m < N_META

    def halves(kv, g):
        k_plain, k_swap = kv[:, 0:LANES], kv[:, LANES:2 * LANES]
        v_plain, v_swap = kv[:, 2 * LANES:3 * LANES], kv[:, 3 * LANES:4 * LANES]
        if g == 0:
            k_lo, k_hi, v_lo, v_hi = k_plain, k_swap, v_plain, v_swap
        else:
            k_lo, k_hi, v_lo, v_hi = k_swap, k_plain, v_swap, v_plain
        kk = jnp.concatenate([k_lo * m_lo, k_hi * m_hi], axis=0)
        vv = jnp.concatenate([v_lo * m_lo, v_hi * m_hi], axis=0)
        return kk, vv

    for g in range(KV_HEADS):
        kb, vb = halves(kv_band, g)
        km, vm = halves(kv_meta, g)
        for s in (2 * g, 2 * g + 1):
            q = q_ref[:, s * LANES:(s + 1) * LANES]
            sb = _dot_nt(q, kb)
            sm = _dot_nt(q, km)
            sink_a = sink_ref[2 * s]
            sink_b = sink_ref[2 * s + 1]
            sa = jnp.where(band_ok, sb[:, :2 * blk], NEG_INF)
            sbb = jnp.where(band_ok, sb[:, 2 * blk:], NEG_INF)
            mx_a = jnp.maximum(
                jnp.maximum(jnp.max(sa, axis=-1, keepdims=True),
                            jnp.max(jnp.where(first, sm, NEG_INF), axis=-1, keepdims=True)),
                sink_a)
            mx_b = jnp.maximum(
                jnp.maximum(jnp.max(sbb, axis=-1, keepdims=True),
                            jnp.max(jnp.where(first, NEG_INF, sm), axis=-1, keepdims=True)),
                sink_b)
            pa = jnp.exp(sa - mx_a)
            pb = jnp.exp(sbb - mx_b)
            pm = jnp.exp(sm - jnp.where(first, mx_a, mx_b))
            den_a = (jnp.sum(pa, axis=-1, keepdims=True)
                     + jnp.sum(jnp.where(first, pm, 0.0), axis=-1, keepdims=True)
                     + jnp.exp(sink_a - mx_a))
            den_b = (jnp.sum(pb, axis=-1, keepdims=True)
                     + jnp.sum(jnp.where(first, 0.0, pm), axis=-1, keepdims=True)
                     + jnp.exp(sink_b - mx_b))
            inv_a = 1.0 / den_a
            inv_b = 1.0 / den_b
            p_band = jnp.concatenate([pa * inv_a, pb * inv_b], axis=1).astype(BF16)
            p_meta = (pm * jnp.where(first, inv_a, inv_b)).astype(BF16)
            o = _dot(p_band, vb) + _dot(p_meta, vm)
            o_ref[:, s * LANES:(s + 1) * LANES] = o.astype(BF16)


def _attn(sinks, qkv, qkv_meta, batch, seq):
    nb = seq // WINDOW
    n = batch * seq
    kvw = 4 * KV_W
    return pl.pallas_call(
        _attn_kernel,
        grid=(batch, nb),
        in_specs=[
            pl.BlockSpec(memory_space=pltpu.SMEM),
            pl.BlockSpec((WINDOW, Q_W), lambda b, i: (b * nb + i, 0)),
            pl.BlockSpec((WINDOW, kvw), lambda b, i: (b * nb + i, 1)),
            pl.BlockSpec((WINDOW, kvw), lambda b, i: (jnp.maximum(b * nb + i - 1, 0), 1)),
            pl.BlockSpec((N_META, kvw), lambda b, i: (0, 1)),
        ],
        out_specs=pl.BlockSpec((WINDOW, Q_W), lambda b, i: (b * nb + i, 0)),
        out_shape=jax.ShapeDtypeStruct((n, Q_W), BF16),
        compiler_params=pltpu.CompilerParams(
            dimension_semantics=("arbitrary", "arbitrary"), vmem_limit_bytes=VMEM_LIMIT),
        name="attn",
    )(sinks, qkv, qkv, qkv, qkv_meta)


def _split2(x):
    hi = x.astype(BF16)
    lo = (x - hi.astype(F32)).astype(BF16)
    return hi, lo


def _rwkv_kernel(p_ref, pprev_ref, s0_ref, mix_ref, w0_ref, a0_ref, w2a_ref, g2_ref,
                 kk_ref, ka_ref, rk_ref, lnw_ref, lnb_ref,
                 tri_ref, bd_ref, strict_ref, incl_ref, eye_ref,
                 y_ref, sfin_ref, pbuf, s_scr, *, nc):
    c = pl.program_id(1)
    tok = nc * CHUNK

    @pl.when(c == 0)
    def _():
        pbuf[0:8, :] = pprev_ref[...]
        s_scr[...] = s0_ref[...]

    p = p_ref[...]
    pbuf[8:8 + tok, :] = p
    psh = pbuf[7:7 + tok, :]
    pbuf[0:8, :] = p[tok - 8:tok, :]
    pf = p + (psh - p) * mix_ref[...]

    o1, o2, o3 = RWKV_DIM, 2 * RWKV_DIM, 3 * RWKV_DIM
    r = pf[:, :o1]
    k = pf[:, o1:o2]
    v = pf[:, o2:o3]
    dwa = pf[:, o3:o3 + LANES]
    dg = pf[:, o3 + LANES:]

    lane = lax.broadcasted_iota(jnp.int32, (tok, LANES), 1)
    z = jnp.where(lane < DECAY_LORA, jnp.tanh(dwa), dwa).astype(BF16)
    wa = _dot(z, w2a_ref[...])
    zw = -(w0_ref[...] + wa[:, :o1])
    softplus = jnp.maximum(zw, 0.0) + jnp.log(1.0 + jnp.exp(-jnp.abs(zw)))
    logw = -jnp.exp(-softplus - 0.5)
    a = _sigmoid(a0_ref[...] + wa[:, o1:])
    gate = _dot(_sigmoid(dg).astype(BF16), g2_ref[...])

    bd = bd_ref[...]
    bd_f = bd.astype(F32)
    kkv = k * kk_ref[...]
    kp = k * (1.0 + (a - 1.0) * ka_ref[...])
    rkk = r * kp * rk_ref[...]

    def seg_sums(xs):
        parts = []
        for x in xs:
            parts.extend(_split2(x))
        out = _dot(jnp.concatenate(parts, axis=0), bd)
        return [out[2 * i * tok:(2 * i + 1) * tok] + out[(2 * i + 1) * tok:(2 * i + 2) * tok]
                for i in range(len(xs))]

    def grp(x, gi):
        return x[:, gi * GROUP:(gi + 1) * GROUP]

    sq = kkv * kkv
    sums = seg_sums([grp(sq, 0), grp(sq, 1), grp(rkk, 0), grp(rkk, 1)])
    ss = jnp.concatenate(sums[0:2], axis=1)
    bonus = jnp.concatenate(sums[2:4], axis=1)
    kkn = kkv / jnp.maximum(jnp.sqrt(ss), 1e-12)
    aa = -kkn
    bb = kkn * a

    tri = tri_ref[...]
    strict = strict_ref[...]
    incl = incl_ref[...]
    eye = eye_ref[...]

    def blk(x):
        xb = x.astype(BF16)
        return jnp.concatenate([xb, xb, xb, xb], axis=0) * bd

    lp_grp, lw_grp = [], []
    for gi in range(N_GROUPS):
        lw_all = grp(logw, gi)
        hi = lw_all.astype(BF16)
        r1 = lw_all - hi.astype(F32)
        mid = r1.astype(BF16)
        lo = (r1 - mid.astype(F32)).astype(BF16)
        cs = _dot(tri, jnp.concatenate([hi, mid, lo], axis=1))
        lp_grp.append(cs[:, :GROUP] + cs[:, GROUP:2 * GROUP] + cs[:, 2 * GROUP:])
        lw_grp.append(lw_all)

    pairs = [(gi, ci) for ci in range(nc) for gi in range(N_GROUPS)]
    st = {}
    for key in pairs:
        gi, ci = key
        rows = slice(ci * CHUNK, (ci + 1) * CHUNK)
        cut = lambda x: x[rows, gi * GROUP:(gi + 1) * GROUP]
        lp = lp_grp[gi][rows]
        lw = lw_grp[gi][rows]
        lpc = lp[CHUNK - 1:CHUNK, :]
        e_neg = jnp.exp(-lp)
        e_end = jnp.exp(lpc - lp)
        rg, kg, vg, ag, bg = cut(r), cut(kp), cut(v), cut(aa), cut(bb)
        at = ag * jnp.exp(lp - lw)
        rt = rg * jnp.exp(lp)
        st[key] = dict(
            at=at, rt=rt.astype(BF16), vg=vg, vblk=blk(vg), pc=jnp.exp(lpc),
            bt=blk(bg * e_neg), kt=blk(kg * e_neg),
            ar=jnp.concatenate([at, rt], axis=0).astype(BF16),
            bk_end=jnp.concatenate([bg * e_end, kg * e_end], axis=0).astype(BF16))

    for key in pairs:
        q = st[key]
        sb = _dot_nt(q["ar"], q["bt"])
        sk = _dot_nt(q["ar"], q["kt"])
        q["a_ab"] = sb[:CHUNK] * strict
        q["a_ak"] = (sk[:CHUNK] * strict).astype(BF16)
        q["arr"] = jnp.concatenate([sb[CHUNK:] * incl, sk[CHUNK:] * incl], axis=1).astype(BF16)

    for key in pairs:
        q = st[key]
        q["pw"] = _dot(q["a_ab"].astype(BF16), blk(q["a_ab"]))
        q["minv"] = eye + q["a_ab"]
    for j in range(1, 6):
        for key in pairs:
            q = st[key]
            if j < 5:
                out = _dot(q["pw"].astype(BF16),
                           jnp.concatenate([blk(q["pw"]), blk(q["minv"])], axis=1))
                q["pw"] = out[:, :GROUP]
                q["minv"] = q["minv"] + out[:, GROUP:]
            else:
                q["minv"] = q["minv"] + _dot(q["pw"].astype(BF16), blk(q["minv"]))

    for key in pairs:
        q = st[key]
        q["av"] = _dot(q["a_ak"], q["vblk"])
    for key in pairs:
        q = st[key]
        wu = _dot(q["minv"].astype(BF16), jnp.concatenate([blk(q["at"]), blk(q["av"])], axis=1))
        q["w"] = wu[:, :GROUP].astype(BF16)
        q["u0"] = wu[:, GROUP:]

    s_cur = [s_scr[gi] for gi in range(N_GROUPS)]
    y_chunks = [[] for _ in range(N_GROUPS)]
    for key in pairs:
        gi, ci = key
        q = st[key]
        s_b = s_cur[gi].astype(BF16)
        u = _dot_nt(q["w"], s_b) + q["u0"]
        y = (_dot_nt(q["rt"], s_b)
             + _dot(q["arr"], jnp.concatenate([blk(u), q["vblk"]], axis=0)))
        upd = _dot_tn(jnp.concatenate([u, q["vg"]], axis=0).astype(BF16), q["bk_end"])
        s_cur[gi] = s_cur[gi] * q["pc"] + upd * bd_f
        y_chunks[gi].append(y)
    ys = []
    for gi in range(N_GROUPS):
        s_scr[gi] = s_cur[gi]
        ys.append(jnp.concatenate(y_chunks[gi], axis=0) if nc > 1 else y_chunks[gi][0])

    means = seg_sums(ys)
    ds = [ys[i] - means[i] * (1.0 / HEAD_DIM) for i in range(N_GROUPS)]
    vars_ = seg_sums([d * d for d in ds])
    yn = jnp.concatenate(
        [ds[i] * lax.rsqrt(vars_[i] * (1.0 / HEAD_DIM) + RWKV_LN_EPS) for i in range(N_GROUPS)],
        axis=1)
    yn = yn * lnw_ref[...] + lnb_ref[...]
    y_ref[...] = ((yn + bonus * v) * gate).astype(BF16)

    @pl.when(c == pl.num_programs(1) - 1)
    def _():
        sfin_ref[...] = s_scr[...]


def _rwkv(p, pprev, s0, vecs, w2a, g2p, consts, batch, n_chunks, nc):
    mix, w0, a0, k_k, k_a, r_k, ln_w, ln_b = vecs
    tri, bd, strict, incl, eye = consts
    n = batch * n_chunks * CHUNK
    tok = nc * CHUNK
    steps = n_chunks // nc
    vec = _const_spec((1, RWKV_DIM))
    return pl.pallas_call(
        functools.partial(_rwkv_kernel, nc=nc),
        grid=(batch, steps),
        in_specs=[
            pl.BlockSpec((tok, RW_W), lambda b, c: (b * steps + c, 0)),
            _const_spec((8, RW_W)),
            _const_spec((N_GROUPS, GROUP, GROUP)),
            _const_spec((1, RW_W)),
            vec, vec,
            _const_spec((LANES, 2 * RWKV_DIM)),
            _const_spec((GATE_PAD, RWKV_DIM)),
            vec, vec, vec, vec, vec,
            _const_spec((tok, tok)),
            _const_spec((GROUP, GROUP)),
            _const_spec((CHUNK, GROUP)),
            _const_spec((CHUNK, GROUP)),
            _const_spec((CHUNK, GROUP)),
        ],
        out_specs=[
            pl.BlockSpec((tok, RWKV_DIM), lambda b, c: (b * steps + c, 0)),
            pl.BlockSpec((N_GROUPS, GROUP, GROUP), lambda b, c: (0, 0, 0)),
        ],
        out_shape=[
            jax.ShapeDtypeStruct((n, RWKV_DIM), BF16),
            jax.ShapeDtypeStruct((N_GROUPS, GROUP, GROUP), F32),
        ],
        scratch_shapes=[
            pltpu.VMEM((8 + tok, RW_W), F32),
            pltpu.VMEM((N_GROUPS, GROUP, GROUP), F32),
        ],
        compiler_params=pltpu.CompilerParams(
            dimension_semantics=("arbitrary", "arbitrary"), vmem_limit_bytes=VMEM_LIMIT),
        name="rwkv",
    )(p, pprev, s0, mix, w0, a0, w2a, g2p, k_k, k_a, r_k, ln_w, ln_b,
      tri, bd, strict, incl, eye)


def _post_kernel(ya_ref, yr_ref, gate_ref, x_ref, wba_ref, wbr_ref, wo_ref, gf_ref,
                 wg_ref, wu_ref, wd_ref, gn_ref, o_ref):
    gates = gate_ref[...].astype(F32)
    merged = (gates[:, :D_MODEL] * _dot(ya_ref[...], wba_ref[...])
              + gates[:, D_MODEL:] * _dot(yr_ref[...], wbr_ref[...]))
    h = x_ref[...] + _dot(merged.astype(BF16), wo_ref[...])
    ms = jnp.mean(h * h, axis=-1, keepdims=True)
    f = (h * lax.rsqrt(ms + RMS_EPS) * gf_ref[...]).astype(BF16)
    gt = _dot(f, wg_ref[...])
    up = _dot(f, wu_ref[...])
    act = (gt * _sigmoid(gt) * up).astype(BF16)
    h = h + _dot(act, wd_ref[...])
    ms = jnp.mean(h * h, axis=-1, keepdims=True)
    o_ref[...] = h * lax.rsqrt(ms + RMS_EPS) * gn_ref[...]


def _post(ya, yr, gates, x2, wba, wbr, wo, gf, wg, wu, wd, gn, tm):
    n = x2.shape[0]
    row = lambda i: (i, 0)
    return pl.pallas_call(
        _post_kernel,
        grid=(n // tm,),
        in_specs=[
            pl.BlockSpec((tm, Q_W), row),
            pl.BlockSpec((tm, RWKV_DIM), row),
            pl.BlockSpec((tm, 2 * D_MODEL), row),
            pl.BlockSpec((tm, D_MODEL), row),
            _const_spec((Q_W, D_MODEL)),
            _const_spec((RWKV_DIM, D_MODEL)),
            _const_spec((D_MODEL, D_MODEL)),
            _const_spec((1, D_MODEL)),
            _const_spec((D_MODEL, D_FF)),
            _const_spec((D_MODEL, D_FF)),
            _const_spec((D_FF, D_MODEL)),
            _const_spec((1, D_MODEL)),
        ],
        out_specs=pl.BlockSpec((tm, D_MODEL), row),
        out_shape=jax.ShapeDtypeStruct((n, D_MODEL), F32),
        compiler_params=pltpu.CompilerParams(
            dimension_semantics=("arbitrary",), vmem_limit_bytes=VMEM_LIMIT),
        name="post",
    )(ya, yr, gates, x2, wba, wbr, wo, gf, wg, wu, wd, gn)


def _rope_tables(pos):
    half = ROPE_DIM // 2
    inv_freq = jnp.power(jnp.float32(ROPE_THETA),
                         -jnp.arange(half, dtype=F32) * (2.0 / ROPE_DIM))
    ang = pos.astype(F32)[:, None] * inv_freq[None, :]
    cos, sin = jnp.cos(ang), jnp.sin(ang)
    n = pos.shape[0]
    one = jnp.ones((n, HEAD_DIM - ROPE_DIM), F32)
    zero8 = jnp.zeros((n, half), F32)
    zero48 = jnp.zeros((n, HEAD_DIM - ROPE_DIM), F32)
    c_head = jnp.concatenate([cos, cos, one], axis=1)
    s1_head = jnp.concatenate([zero8, sin, zero48], axis=1)
    s2_head = jnp.concatenate([-sin, zero8, zero48], axis=1)
    dup = lambda t: jnp.concatenate([t, t], axis=1)
    return dup(c_head), dup(s1_head), dup(s2_head)


def _rwkv_constants(nc):
    tt = np.arange(nc * CHUNK)
    tri = ((tt[None, :] <= tt[:, None])
           & (tt[None, :] // CHUNK == tt[:, None] // CHUNK)).astype(np.float32)
    t = np.arange(CHUNK)
    hh = np.arange(GROUP) // HEAD_DIM
    bd = (hh[:, None] == hh[None, :]).astype(np.float32)
    s = np.arange(GROUP) % CHUNK
    strict = (s[None, :] < t[:, None]).astype(np.float32)
    incl = (s[None, :] <= t[:, None]).astype(np.float32)
    eye = (s[None, :] == t[:, None]).astype(np.float32)
    return (jnp.asarray(tri, BF16), jnp.asarray(bd, BF16), jnp.asarray(strict),
            jnp.asarray(incl), jnp.asarray(eye))


def kernel(x, meta_tokens, norm_mix_g, w_in, b_in, attn_sinks, rwkv_mix, rwkv_w0, rwkv_w2, rwkv_a0, rwkv_a2, rwkv_g2, rwkv_k_k, rwkv_k_a, rwkv_r_k, rwkv_ln_w, rwkv_ln_b, w_br_attn, w_br_rwkv, w_o, norm_ffn_g, w_ffn_gate, w_ffn_up, w_ffn_down, norm_final_g):
    batch, seq, _ = x.shape
    layer = 0
    x2 = x.reshape(batch * seq, D_MODEL)

    w = w_in[layer]
    b = b_in[layer]
    scale = HEAD_DIM ** -0.5
    h64 = HEAD_DIM
    kq, kk0, kk1 = Q_W, Q_W + h64, Q_W + 2 * h64
    vq = Q_W + KV_W

    def qkv_cols(t):
        return jnp.concatenate(
            [t[..., :Q_W] * scale,
             t[..., kq:kq + KV_W],
             t[..., kk0:kk1], t[..., kq:kk0],
             t[..., vq:vq + KV_W],
             t[..., vq + h64:vq + 2 * h64], t[..., vq:vq + h64]], axis=-1)

    def rw_cols(t, n_lead):
        pad = jnp.zeros(t.shape[:n_lead] + (GATE_PAD - GATE_LORA,), t.dtype)
        return jnp.concatenate([t, pad], axis=-1)

    wq = qkv_cols(w).astype(BF16)
    bq = qkv_cols(b)[None]
    wr = rw_cols(w[:, ATTN_PROJ:ATTN_PROJ + RWKV_PROJ], 1).astype(BF16)
    br = rw_cols(b[ATTN_PROJ:ATTN_PROJ + RWKV_PROJ], 0)[None]
    wg = w[:, ATTN_PROJ + RWKV_PROJ:].astype(BF16)
    bg = b[ATTN_PROJ + RWKV_PROJ:][None]
    g_mix = norm_mix_g[layer][None]

    mix = rw_cols(rwkv_mix[layer], 0)[None]
    zl = jnp.zeros((DECAY_LORA, RWKV_DIM), F32)
    w2a = jnp.concatenate(
        [jnp.concatenate([rwkv_w2[layer], zl], axis=1),
         jnp.concatenate([zl, rwkv_a2[layer]], axis=1)], axis=0).astype(BF16)
    g2p = jnp.concatenate(
        [rwkv_g2[layer], jnp.zeros((GATE_PAD - GATE_LORA, RWKV_DIM), F32)], axis=0).astype(BF16)
    vecs = (mix, rwkv_w0[layer][None], rwkv_a0[layer][None], rwkv_k_k[layer][None],
            rwkv_k_a[layer][None], rwkv_r_k[layer].reshape(1, RWKV_DIM),
            rwkv_ln_w[layer][None], rwkv_ln_b[layer][None])

    cos_m, s1_m, s2_m = _rope_tables(jnp.arange(N_META, dtype=jnp.int32))
    qkv_m, p_m, _ = _inproj(meta_tokens, g_mix, wq, wr, wg, bq, br, bg,
                            cos_m, s1_m, s2_m, N_META)
    p_m_pad = jnp.concatenate([jnp.zeros((CHUNK - N_META, RW_W), F32), p_m], axis=0)
    zero_state = jnp.zeros((N_GROUPS, GROUP, GROUP), F32)
    _, s_meta = _rwkv(p_m_pad, jnp.zeros((8, RW_W), F32), zero_state, vecs, w2a, g2p,
                      _rwkv_constants(1), 1, 1, 1)

    cos, s1, s2 = _rope_tables(N_META + jnp.arange(seq, dtype=jnp.int32))
    tm_in = min(512, seq)
    qkv, p, gates = _inproj(x2, g_mix, wq, wr, wg, bq, br, bg, cos, s1, s2, tm_in)
    y_attn = _attn(attn_sinks[layer], qkv, qkv_m, batch, seq)
    nc = RWKV_CHUNKS_PER_STEP
    y_rwkv, _ = _rwkv(p, p_m[N_META - 8:], s_meta, vecs, w2a, g2p, _rwkv_constants(nc),
                      batch, seq // CHUNK, nc)
    out = _post(y_attn, y_rwkv, gates, x2,
                w_br_attn[layer].astype(BF16), w_br_rwkv[layer].astype(BF16),
                w_o[layer].astype(BF16), norm_ffn_g[layer][None],
                w_ffn_gate[layer].astype(BF16), w_ffn_up[layer].astype(BF16),
                w_ffn_down[layer].astype(BF16), norm_final_g[None], min(256, seq))
    return out.reshape(batch, seq, D_MODEL)
```

```python
import functools

import jax
import jax.numpy as jnp
import numpy as np
from jax import lax
from jax.experimental import pallas as pl
from jax.experimental.pallas import tpu as pltpu

F32 = jnp.float32
BF16 = jnp.bfloat16

D_MODEL = 1024
N_META = 16
HEAD_DIM = 64
Q_HEADS = 8
KV_HEADS = 2
WINDOW = 128
ROPE_THETA = 500000.0
ROPE_DIM = HEAD_DIM // 4
RWKV_HEADS = 8
RWKV_DIM = RWKV_HEADS * HEAD_DIM
DECAY_LORA = 64
AAA_LORA = 64
GATE_LORA = 160
RWKV_LN_EPS = 64e-5
D_FF = 2816
Q_W = Q_HEADS * HEAD_DIM
KV_W = KV_HEADS * HEAD_DIM
ATTN_PROJ = Q_W + 2 * KV_W
RWKV_PROJ = 3 * RWKV_DIM + DECAY_LORA + AAA_LORA + GATE_LORA
RMS_EPS = 1e-6
NEG_INF = -1e30

LANES = 128
QKV_W = Q_W + 4 * KV_W
GATE_PAD = 256
RW_W = 3 * RWKV_DIM + LANES + GATE_PAD
CHUNK = 64
GROUP = 4 * HEAD_DIM
N_GROUPS = RWKV_DIM // GROUP
ATTN_BLOCKS_PER_STEP = 2
RWKV_CHUNKS_PER_STEP = 4
VMEM_LIMIT = 56 * 1024 * 1024


def _dot(a, b):
    return jnp.dot(a, b, preferred_element_type=F32)


def _dot_nt(a, b):
    return lax.dot_general(a, b, (((1,), (1,)), ((), ())), preferred_element_type=F32)


def _dot_tn(a, b):
    return lax.dot_general(a, b, (((0,), (0,)), ((), ())), preferred_element_type=F32)


def _sigmoid(x):
    return 1.0 / (1.0 + jnp.exp(-x))


def _const_spec(shape):
    nd = len(shape)
    return pl.BlockSpec(shape, lambda *_: (0,) * nd, pipeline_mode=pl.Buffered(1))


def _inproj_kernel(x_ref, g_ref, wq_ref, wr_ref, wg_ref, bq_ref, br_ref, bg_ref,
                   cos_ref, s1_ref, s2_ref, qkv_ref, p_ref, gate_ref):
    x = x_ref[...]
    ms = jnp.mean(x * x, axis=-1, keepdims=True)
    u = (x * lax.rsqrt(ms + RMS_EPS) * g_ref[...]).astype(BF16)

    cos = cos_ref[...]
    s1 = s1_ref[...]
    s2 = s2_ref[...]
    n_rope = (Q_W + 2 * KV_W) // LANES
    for j in range(QKV_W // LANES):
        sl = slice(j * LANES, (j + 1) * LANES)
        t = _dot(u, wq_ref[:, sl]) + bq_ref[:, sl]
        if j < n_rope:
            t = t * cos + pltpu.roll(t, 8, 1) * s1 + pltpu.roll(t, LANES - 8, 1) * s2
        qkv_ref[:, sl] = t.astype(BF16)

    step = 384
    for j in range(RW_W // step):
        sl = slice(j * step, (j + 1) * step)
        p_ref[:, sl] = _dot(u, wr_ref[:, sl]) + br_ref[:, sl]

    step = 512
    for j in range(2 * D_MODEL // step):
        sl = slice(j * step, (j + 1) * step)
        gate_ref[:, sl] = _sigmoid(_dot(u, wg_ref[:, sl]) + bg_ref[:, sl]).astype(BF16)


def _inproj(x2, g, wq, wr, wg, bq, br, bg, cos, s1, s2, tm):
    n = x2.shape[0]
    seq_tiles = cos.shape[0] // tm
    row = lambda i: (i, 0)
    pos = lambda i: (i % seq_tiles, 0)
    return pl.pallas_call(
        _inproj_kernel,
        grid=(n // tm,),
        in_specs=[
            pl.BlockSpec((tm, D_MODEL), row),
            _const_spec((1, D_MODEL)),
            _const_spec((D_MODEL, QKV_W)),
            _const_spec((D_MODEL, RW_W)),
            _const_spec((D_MODEL, 2 * D_MODEL)),
            _const_spec((1, QKV_W)),
            _const_spec((1, RW_W)),
            _const_spec((1, 2 * D_MODEL)),
            pl.BlockSpec((tm, LANES), pos),
            pl.BlockSpec((tm, LANES), pos),
            pl.BlockSpec((tm, LANES), pos),
        ],
        out_specs=[
            pl.BlockSpec((tm, QKV_W), row),
            pl.BlockSpec((tm, RW_W), row),
            pl.BlockSpec((tm, 2 * D_MODEL), row),
        ],
        out_shape=[
            jax.ShapeDtypeStruct((n, QKV_W), BF16),
            jax.ShapeDtypeStruct((n, RW_W), F32),
            jax.ShapeDtypeStruct((n, 2 * D_MODEL), BF16),
        ],
        compiler_params=pltpu.CompilerParams(
            dimension_semantics=("arbitrary",), vmem_limit_bytes=VMEM_LIMIT),
        name="inproj",
    )(x2, g, wq, wr, wg, bq, br, bg, cos, s1, s2)


def _attn_kernel(sink_ref, q_ref, kvc_ref, kvp_ref, kvm_ref, o_ref, *, qb):
    n = pl.program_id(1)
    blk = WINDOW
    lane = lax.broadcasted_iota(jnp.int32, (1, LANES), 1)
    m_lo = (lane < HEAD_DIM).astype(BF16)
    m_hi = (lane >= HEAD_DIM).astype(BF16)

    kv_all = jnp.concatenate([kvp_ref[...], kvc_ref[...]], axis=0)
    kv_meta = kvm_ref[...]

    qi = lax.broadcasted_iota(jnp.int32, (2 * blk, 2 * blk), 0) % blk
    c = lax.broadcasted_iota(jnp.int32, (2 * blk, 2 * blk), 1)
    cur_ok = jnp.where(c - blk <= qi, 1, 0)
    has_prev = jnp.minimum(n, 1)
    ok_first = jnp.where(c < blk, jnp.where(c > qi, has_prev, 0), cur_ok) > 0
    ok_rest = jnp.where(c < blk, jnp.where(c > qi, 1, 0), cur_ok) > 0
    first = lax.broadcasted_iota(jnp.int32, (2 * blk, 2 * N_META), 1) < N_META
    top = lax.broadcasted_iota(jnp.int32, (2 * blk, 1), 0) < blk

    def halves(kv, g):
        k_plain, k_swap = kv[:, 0:LANES], kv[:, LANES:2 * LANES]
        v_plain, v_swap = kv[:, 2 * LANES:3 * LANES], kv[:, 3 * LANES:4 * LANES]
        if g == 0:
            k_lo, k_hi, v_lo, v_hi = k_plain, k_swap, v_plain, v_swap
        else:
            k_lo, k_hi, v_lo, v_hi = k_swap, k_plain, v_swap, v_plain
        return k_lo * m_lo, k_hi * m_hi, v_lo * m_lo, v_hi * m_hi

    band = [halves(kv_all, g) for g in range(KV_HEADS)]
    meta = [tuple(jnp.concatenate(pair, axis=0) for pair in
                  ((h[0], h[1]), (h[2], h[3])))
            for h in (halves(kv_meta, g) for g in range(KV_HEADS))]

    units = [(j, g) for j in range(qb) for g in range(KV_HEADS)]
    st = {}
    for key in units:
        j, g = key
        rows = slice(j * blk, (j + 2) * blk)
        k_lo, k_hi, v_lo, v_hi = band[g]
        kb = jnp.concatenate([k_lo[rows], k_hi[rows]], axis=0)
        vb = jnp.concatenate([v_lo[rows], v_hi[rows]], axis=0)
        q = jnp.concatenate(
            [q_ref[j * blk:(j + 1) * blk, s * LANES:(s + 1) * LANES] for s in (2 * g, 2 * g + 1)],
            axis=0)
        st[key] = dict(vb=vb,
                       sb=_dot_nt(q, kb),
                       sm=_dot_nt(q, meta[g][0]))

    for key in units:
        j, g = key
        u = st[key]
        ok = ok_first if j == 0 else ok_rest
        sink_a = jnp.where(top, sink_ref[4 * g], sink_ref[4 * g + 2])
        sink_b = jnp.where(top, sink_ref[4 * g + 1], sink_ref[4 * g + 3])
        sb, sm = u["sb"], u["sm"]
        sa = jnp.where(ok, sb[:, :2 * blk], NEG_INF)
        sbb = jnp.where(ok, sb[:, 2 * blk:], NEG_INF)
        mx_a = jnp.maximum(
            jnp.maximum(jnp.max(sa, axis=-1, keepdims=True),
                        jnp.max(jnp.where(first, sm, NEG_INF), axis=-1, keepdims=True)),
            sink_a)
        mx_b = jnp.maximum(
            jnp.maximum(jnp.max(sbb, axis=-1, keepdims=True),
                        jnp.max(jnp.where(first, NEG_INF, sm), axis=-1, keepdims=True)),
            sink_b)
        pa = jnp.exp(sa - mx_a)
        pb = jnp.exp(sbb - mx_b)
        pm = jnp.exp(sm - jnp.where(first, mx_a, mx_b))
        den_a = (jnp.sum(pa, axis=-1, keepdims=True)
                 + jnp.sum(jnp.where(first, pm, 0.0), axis=-1, keepdims=True)
                 + jnp.exp(sink_a - mx_a))
        den_b = (jnp.sum(pb, axis=-1, keepdims=True)
                 + jnp.sum(jnp.where(first, 0.0, pm), axis=-1, keepdims=True)
                 + jnp.exp(sink_b - mx_b))
        inv_a = 1.0 / den_a
        inv_b = 1.0 / den_b
        u["p_band"] = jnp.concatenate([pa * inv_a, pb * inv_b], axis=1).astype(BF16)
        u["p_meta"] = (pm * jnp.where(first, inv_a, inv_b)).astype(BF16)

    for key in units:
        j, g = key
        u = st[key]
        o = (_dot(u["p_band"], u["vb"]) + _dot(u["p_meta"], meta[g][1])).astype(BF16)
        for i, s in enumerate((2 * g, 2 * g + 1)):
            o_ref[j * blk:(j + 1) * blk, s * LANES:(s + 1) * LANES] = o[i * blk:(i + 1) * blk]


def _attn(sinks, qkv, qkv_meta, batch, seq, qb):
    nb = seq // WINDOW
    steps = nb // qb
    n = batch * seq
    kvw = 4 * KV_W
    rows = qb * WINDOW
    return pl.pallas_call(
        functools.partial(_attn_kernel, qb=qb),
        grid=(batch, steps),
        in_specs=[
            pl.BlockSpec(memory_space=pltpu.SMEM),
            pl.BlockSpec((rows, Q_W), lambda b, i: (b * steps + i, 0)),
            pl.BlockSpec((rows, kvw), lambda b, i: (b * steps + i, 1)),
            pl.BlockSpec((WINDOW, kvw), lambda b, i: (jnp.maximum((b * steps + i) * qb - 1, 0), 1)),
            pl.BlockSpec((N_META, kvw), lambda b, i: (0, 1)),
        ],
        out_specs=pl.BlockSpec((rows, Q_W), lambda b, i: (b * steps + i, 0)),
        out_shape=jax.ShapeDtypeStruct((n, Q_W), BF16),
        compiler_params=pltpu.CompilerParams(
            dimension_semantics=("arbitrary", "arbitrary"), vmem_limit_bytes=VMEM_LIMIT),
        name="attn",
    )(sinks, qkv, qkv, qkv, qkv_meta)


def _split2(x):
    hi = x.astype(BF16)
    lo = (x - hi.astype(F32)).astype(BF16)
    return hi, lo


def _rwkv_kernel(p_ref, pprev_ref, s0_ref, mix_ref, w0_ref, a0_ref, w2a_ref, g2_ref,
                 kk_ref, ka_ref, rk_ref, lnw_ref, lnb_ref,
                 tri_ref, bd_ref, strict_ref, incl_ref, eye_ref,
                 y_ref, sfin_ref, pbuf, s_scr, *, nc):
    c = pl.program_id(1)
    tok = nc * CHUNK

    @pl.when(c == 0)
    def _():
        pbuf[0:8, :] = pprev_ref[...]
        s_scr[...] = s0_ref[...]

    p = p_ref[...]
    pbuf[8:8 + tok, :] = p
    psh = pbuf[7:7 + tok, :]
    pbuf[0:8, :] = p[tok - 8:tok, :]
    pf = p + (psh - p) * mix_ref[...]

    o1, o2, o3 = RWKV_DIM, 2 * RWKV_DIM, 3 * RWKV_DIM
    r = pf[:, :o1]
    k = pf[:, o1:o2]
    v = pf[:, o2:o3]
    dwa = pf[:, o3:o3 + LANES]
    dg = pf[:, o3 + LANES:]

    lane = lax.broadcasted_iota(jnp.int32, (tok, LANES), 1)
    z = jnp.where(lane < DECAY_LORA, jnp.tanh(dwa), dwa).astype(BF16)
    wa = _dot(z, w2a_ref[...])
    zw = -(w0_ref[...] + wa[:, :o1])
    softplus = jnp.maximum(zw, 0.0) + jnp.log(1.0 + jnp.exp(-jnp.abs(zw)))
    logw = -jnp.exp(-softplus - 0.5)
    a = _sigmoid(a0_ref[...] + wa[:, o1:])
    gate = _dot(_sigmoid(dg).astype(BF16), g2_ref[...])

    bd = bd_ref[...]
    bd_f = bd.astype(F32)
    kkv = k * kk_ref[...]
    kp = k * (1.0 + (a - 1.0) * ka_ref[...])
    rkk = r * kp * rk_ref[...]

    def seg_sums(xs):
        parts = []
        for x in xs:
            parts.extend(_split2(x))
        out = _dot(jnp.concatenate(parts, axis=0), bd)
        return [out[2 * i * tok:(2 * i + 1) * tok] + out[(2 * i + 1) * tok:(2 * i + 2) * tok]
                for i in range(len(xs))]

    def grp(x, gi):
        return x[:, gi * GROUP:(gi + 1) * GROUP]

    sq = kkv * kkv
    sums = seg_sums([grp(sq, 0), grp(sq, 1), grp(rkk, 0), grp(rkk, 1)])
    ss = jnp.concatenate(sums[0:2], axis=1)
    bonus = jnp.concatenate(sums[2:4], axis=1)
    kkn = kkv / jnp.maximum(jnp.sqrt(ss), 1e-12)
    aa = -kkn
    bb = kkn * a

    tri = tri_ref[...]
    strict = strict_ref[...]
    incl = incl_ref[...]
    eye = eye_ref[...]

    def blk(x):
        xb = x.astype(BF16)
        return jnp.concatenate([xb, xb, xb, xb], axis=0) * bd

    lp_grp, lw_grp = [], []
    for gi in range(N_GROUPS):
        lw_all = grp(logw, gi)
        hi = lw_all.astype(BF16)
        r1 = lw_all - hi.astype(F32)
        mid = r1.astype(BF16)
        lo = (r1 - mid.astype(F32)).astype(BF16)
        cs = _dot(tri, jnp.concatenate([hi, mid, lo], axis=1))
        lp_grp.append(cs[:, :GROUP] + cs[:, GROUP:2 * GROUP] + cs[:, 2 * GROUP:])
        lw_grp.append(lw_all)

    pairs = [(gi, ci) for ci in range(nc) for gi in range(N_GROUPS)]
    st = {}
    for key in pairs:
        gi, ci = key
        rows = slice(ci * CHUNK, (ci + 1) * CHUNK)
        cut = lambda x: x[rows, gi * GROUP:(gi + 1) * GROUP]
        lp = lp_grp[gi][rows]
        lw = lw_grp[gi][rows]
        lpc = lp[CHUNK - 1:CHUNK, :]
        e_neg = jnp.exp(-lp)
        e_end = jnp.exp(lpc - lp)
        rg, kg, vg, ag, bg = cut(r), cut(kp), cut(v), cut(aa), cut(bb)
        at = ag * jnp.exp(lp - lw)
        rt = rg * jnp.exp(lp)
        st[key] = dict(
            at=at, rt=rt.astype(BF16), vg=vg, vblk=blk(vg), pc=jnp.exp(lpc),
            bt=blk(bg * e_neg), kt=blk(kg * e_neg),
            ar=jnp.concatenate([at, rt], axis=0).astype(BF16),
            bk_end=jnp.concatenate([bg * e_end, kg * e_end], axis=0).astype(BF16))

    for key in pairs:
        q = st[key]
        sb = _dot_nt(q["ar"], q["bt"])
        sk = _dot_nt(q["ar"], q["kt"])
        q["a_ab"] = sb[:CHUNK] * strict
        q["a_ak"] = (sk[:CHUNK] * strict).astype(BF16)
        q["arr"] = jnp.concatenate([sb[CHUNK:] * incl, sk[CHUNK:] * incl], axis=1).astype(BF16)

    for key in pairs:
        q = st[key]
        q["pw"] = _dot(q["a_ab"].astype(BF16), blk(q["a_ab"]))
        q["minv"] = eye + q["a_ab"]
    for j in range(1, 6):
        for key in pairs:
            q = st[key]
            if j < 5:
                out = _dot(q["pw"].astype(BF16),
                           jnp.concatenate([blk(q["pw"]), blk(q["minv"])], axis=1))
                q["pw"] = out[:, :GROUP]
                q["minv"] = q["minv"] + out[:, GROUP:]
            else:
                q["minv"] = q["minv"] + _dot(q["pw"].astype(BF16), blk(q["minv"]))

    for key in pairs:
        q = st[key]
        q["av"] = _dot(q["a_ak"], q["vblk"])
    for key in pairs:
        q = st[key]
        wu = _dot(q["minv"].astype(BF16), jnp.concatenate([blk(q["at"]), blk(q["av"])], axis=1))
        q["w"] = wu[:, :GROUP].astype(BF16)
        q["u0"] = wu[:, GROUP:]

    s_cur = [s_scr[gi] for gi in range(N_GROUPS)]
    y_chunks = [[] for _ in range(N_GROUPS)]
    for key in pairs:
        gi, ci = key
        q = st[key]
        s_b = s_cur[gi].astype(BF16)
        u = _dot_nt(q["w"], s_b) + q["u0"]
        y = (_dot_nt(q["rt"], s_b)
             + _dot(q["arr"], jnp.concatenate([blk(u), q["vblk"]], axis=0)))
        upd = _dot_tn(jnp.concatenate([u, q["vg"]], axis=0).astype(BF16), q["bk_end"])
        s_cur[gi] = s_cur[gi] * q["pc"] + upd * bd_f
        y_chunks[gi].append(y)
    ys = []
    for gi in range(N_GROUPS):
        s_scr[gi] = s_cur[gi]
        ys.append(jnp.concatenate(y_chunks[gi], axis=0) if nc > 1 else y_chunks[gi][0])

    means = seg_sums(ys)
    ds = [ys[i] - means[i] * (1.0 / HEAD_DIM) for i in range(N_GROUPS)]
    vars_ = seg_sums([d * d for d in ds])
    yn = jnp.concatenate(
        [ds[i] * lax.rsqrt(vars_[i] * (1.0 / HEAD_DIM) + RWKV_LN_EPS) for i in range(N_GROUPS)],
        axis=1)
    yn = yn * lnw_ref[...] + lnb_ref[...]
    y_ref[...] = ((yn + bonus * v) * gate).astype(BF16)

    @pl.when(c == pl.num_programs(1) - 1)
    def _():
        sfin_ref[...] = s_scr[...]


def _rwkv(p, pprev, s0, vecs, w2a, g2p, consts, batch, n_chunks, nc):
    mix, w0, a0, k_k, k_a, r_k, ln_w, ln_b = vecs
    tri, bd, strict, incl, eye = consts
    n = batch * n_chunks * CHUNK
    tok = nc * CHUNK
    steps = n_chunks // nc
    vec = _const_spec((1, RWKV_DIM))
    return pl.pallas_call(
        functools.partial(_rwkv_kernel, nc=nc),
        grid=(batch, steps),
        in_specs=[
            pl.BlockSpec((tok, RW_W), lambda b, c: (b * steps + c, 0)),
            _const_spec((8, RW_W)),
            _const_spec((N_GROUPS, GROUP, GROUP)),
            _const_spec((1, RW_W)),
            vec, vec,
            _const_spec((LANES, 2 * RWKV_DIM)),
            _const_spec((GATE_PAD, RWKV_DIM)),
            vec, vec, vec, vec, vec,
            _const_spec((tok, tok)),
            _const_spec((GROUP, GROUP)),
            _const_spec((CHUNK, GROUP)),
            _const_spec((CHUNK, GROUP)),
            _const_spec((CHUNK, GROUP)),
        ],
        out_specs=[
            pl.BlockSpec((tok, RWKV_DIM), lambda b, c: (b * steps + c, 0)),
            pl.BlockSpec((N_GROUPS, GROUP, GROUP), lambda b, c: (0, 0, 0)),
        ],
        out_shape=[
            jax.ShapeDtypeStruct((n, RWKV_DIM), BF16),
            jax.ShapeDtypeStruct((N_GROUPS, GROUP, GROUP), F32),
        ],
        scratch_shapes=[
            pltpu.VMEM((8 + tok, RW_W), F32),
            pltpu.VMEM((N_GROUPS, GROUP, GROUP), F32),
        ],
        compiler_params=pltpu.CompilerParams(
            dimension_semantics=("arbitrary", "arbitrary"), vmem_limit_bytes=VMEM_LIMIT),
        name="rwkv",
    )(p, pprev, s0, mix, w0, a0, w2a, g2p, k_k, k_a, r_k, ln_w, ln_b,
      tri, bd, strict, incl, eye)


def _post_kernel(ya_ref, yr_ref, gate_ref, x_ref, wba_ref, wbr_ref, wo_ref, gf_ref,
                 wg_ref, wu_ref, wd_ref, gn_ref, o_ref):
    gates = gate_ref[...].astype(F32)
    merged = (gates[:, :D_MODEL] * _dot(ya_ref[...], wba_ref[...])
              + gates[:, D_MODEL:] * _dot(yr_ref[...], wbr_ref[...]))
    h = x_ref[...] + _dot(merged.astype(BF16), wo_ref[...])
    ms = jnp.mean(h * h, axis=-1, keepdims=True)
    f = (h * lax.rsqrt(ms + RMS_EPS) * gf_ref[...]).astype(BF16)
    gt = _dot(f, wg_ref[...])
    up = _dot(f, wu_ref[...])
    act = (gt * _sigmoid(gt) * up).astype(BF16)
    h = h + _dot(act, wd_ref[...])
    ms = jnp.mean(h * h, axis=-1, keepdims=True)
    o_ref[...] = h * lax.rsqrt(ms + RMS_EPS) * gn_ref[...]


def _post(ya, yr, gates, x2, wba, wbr, wo, gf, wg, wu, wd, gn, tm):
    n = x2.shape[0]
    row = lambda i: (i, 0)
    return pl.pallas_call(
        _post_kernel,
        grid=(n // tm,),
        in_specs=[
            pl.BlockSpec((tm, Q_W), row),
            pl.BlockSpec((tm, RWKV_DIM), row),
            pl.BlockSpec((tm, 2 * D_MODEL), row),
            pl.BlockSpec((tm, D_MODEL), row),
            _const_spec((Q_W, D_MODEL)),
            _const_spec((RWKV_DIM, D_MODEL)),
            _const_spec((D_MODEL, D_MODEL)),
            _const_spec((1, D_MODEL)),
            _const_spec((D_MODEL, D_FF)),
            _const_spec((D_MODEL, D_FF)),
            _const_spec((D_FF, D_MODEL)),
            _const_spec((1, D_MODEL)),
        ],
        out_specs=pl.BlockSpec((tm, D_MODEL), row),
        out_shape=jax.ShapeDtypeStruct((n, D_MODEL), F32),
        compiler_params=pltpu.CompilerParams(
            dimension_semantics=("arbitrary",), vmem_limit_bytes=VMEM_LIMIT),
        name="post",
    )(ya, yr, gates, x2, wba, wbr, wo, gf, wg, wu, wd, gn)


def _rope_tables(pos):
    half = ROPE_DIM // 2
    inv_freq = jnp.power(jnp.float32(ROPE_THETA),
                         -jnp.arange(half, dtype=F32) * (2.0 / ROPE_DIM))
    ang = pos.astype(F32)[:, None] * inv_freq[None, :]
    cos, sin = jnp.cos(ang), jnp.sin(ang)
    n = pos.shape[0]
    one = jnp.ones((n, HEAD_DIM - ROPE_DIM), F32)
    zero8 = jnp.zeros((n, half), F32)
    zero48 = jnp.zeros((n, HEAD_DIM - ROPE_DIM), F32)
    c_head = jnp.concatenate([cos, cos, one], axis=1)
    s1_head = jnp.concatenate([zero8, sin, zero48], axis=1)
    s2_head = jnp.concatenate([-sin, zero8, zero48], axis=1)
    dup = lambda t: jnp.concatenate([t, t], axis=1)
    return dup(c_head), dup(s1_head), dup(s2_head)


def _rwkv_constants(nc):
    tt = np.arange(nc * CHUNK)
    tri = ((tt[None, :] <= tt[:, None])
           & (tt[None, :] // CHUNK == tt[:, None] // CHUNK)).astype(np.float32)
    t = np.arange(CHUNK)
    hh = np.arange(GROUP) // HEAD_DIM
    bd = (hh[:, None] == hh[None, :]).astype(np.float32)
    s = np.arange(GROUP) % CHUNK
    strict = (s[None, :] < t[:, None]).astype(np.float32)
    incl = (s[None, :] <= t[:, None]).astype(np.float32)
    eye = (s[None, :] == t[:, None]).astype(np.float32)
    return (jnp.asarray(tri, BF16), jnp.asarray(bd, BF16), jnp.asarray(strict),
            jnp.asarray(incl), jnp.asarray(eye))


def kernel(x, meta_tokens, norm_mix_g, w_in, b_in, attn_sinks, rwkv_mix, rwkv_w0, rwkv_w2, rwkv_a0, rwkv_a2, rwkv_g2, rwkv_k_k, rwkv_k_a, rwkv_r_k, rwkv_ln_w, rwkv_ln_b, w_br_attn, w_br_rwkv, w_o, norm_ffn_g, w_ffn_gate, w_ffn_up, w_ffn_down, norm_final_g):
    batch, seq, _ = x.shape
    layer = 0
    x2 = x.reshape(batch * seq, D_MODEL)

    w = w_in[layer]
    b = b_in[layer]
    scale = HEAD_DIM ** -0.5
    h64 = HEAD_DIM
    kq, kk0, kk1 = Q_W, Q_W + h64, Q_W + 2 * h64
    vq = Q_W + KV_W

    def qkv_cols(t):
        return jnp.concatenate(
            [t[..., :Q_W] * scale,
             t[..., kq:kq + KV_W],
             t[..., kk0:kk1], t[..., kq:kk0],
             t[..., vq:vq + KV_W],
             t[..., vq + h64:vq + 2 * h64], t[..., vq:vq + h64]], axis=-1)

    def rw_cols(t, n_lead):
        pad = jnp.zeros(t.shape[:n_lead] + (GATE_PAD - GATE_LORA,), t.dtype)
        return jnp.concatenate([t, pad], axis=-1)

    wq = qkv_cols(w).astype(BF16)
    bq = qkv_cols(b)[None]
    wr = rw_cols(w[:, ATTN_PROJ:ATTN_PROJ + RWKV_PROJ], 1).astype(BF16)
    br = rw_cols(b[ATTN_PROJ:ATTN_PROJ + RWKV_PROJ], 0)[None]
    wg = w[:, ATTN_PROJ + RWKV_PROJ:].astype(BF16)
    bg = b[ATTN_PROJ + RWKV_PROJ:][None]
    g_mix = norm_mix_g[layer][None]

    mix = rw_cols(rwkv_mix[layer], 0)[None]
    zl = jnp.zeros((DECAY_LORA, RWKV_DIM), F32)
    w2a = jnp.concatenate(
        [jnp.concatenate([rwkv_w2[layer], zl], axis=1),
         jnp.concatenate([zl, rwkv_a2[layer]], axis=1)], axis=0).astype(BF16)
    g2p = jnp.concatenate(
        [rwkv_g2[layer], jnp.zeros((GATE_PAD - GATE_LORA, RWKV_DIM), F32)], axis=0).astype(BF16)
    vecs = (mix, rwkv_w0[layer][None], rwkv_a0[layer][None], rwkv_k_k[layer][None],
            rwkv_k_a[layer][None], rwkv_r_k[layer].reshape(1, RWKV_DIM),
            rwkv_ln_w[layer][None], rwkv_ln_b[layer][None])

    cos_m, s1_m, s2_m = _rope_tables(jnp.arange(N_META, dtype=jnp.int32))
    qkv_m, p_m, _ = _inproj(meta_tokens, g_mix, wq, wr, wg, bq, br, bg,
                            cos_m, s1_m, s2_m, N_META)
    p_m_pad = jnp.concatenate([jnp.zeros((CHUNK - N_META, RW_W), F32), p_m], axis=0)
    zero_state = jnp.zeros((N_GROUPS, GROUP, GROUP), F32)
    _, s_meta = _rwkv(p_m_pad, jnp.zeros((8, RW_W), F32), zero_state, vecs, w2a, g2p,
                      _rwkv_constants(1), 1, 1, 1)

    cos, s1, s2 = _rope_tables(N_META + jnp.arange(seq, dtype=jnp.int32))
    tm_in = min(512, seq)
    qkv, p, gates = _inproj(x2, g_mix, wq, wr, wg, bq, br, bg, cos, s1, s2, tm_in)
    y_attn = _attn(attn_sinks[layer], qkv, qkv_m, batch, seq, ATTN_BLOCKS_PER_STEP)
    nc = RWKV_CHUNKS_PER_STEP
    y_rwkv, _ = _rwkv(p, p_m[N_META - 8:], s_meta, vecs, w2a, g2p, _rwkv_constants(nc),
                      batch, seq // CHUNK, nc)
    out = _post(y_attn, y_rwkv, gates, x2,
                w_br_attn[layer].astype(BF16), w_br_rwkv[layer].astype(BF16),
                w_o[layer].astype(BF16), norm_ffn_g[layer][None],
                w_ffn_gate[layer].astype(BF16), w_ffn_up[layer].astype(BF16),
                w_ffn_down[layer].astype(BF16), norm_final_g[None], min(256, seq))
    return out.reshape(batch, seq, D_MODEL)
```

```python
import functools

import jax
import jax.numpy as jnp
import numpy as np
from jax import lax
from jax.experimental import pallas as pl
from jax.experimental.pallas import tpu as pltpu

F32 = jnp.float32
BF16 = jnp.bfloat16

D_MODEL = 1024
N_META = 16
HEAD_DIM = 64
Q_HEADS = 8
KV_HEADS = 2
WINDOW = 128
ROPE_THETA = 500000.0
ROPE_DIM = HEAD_DIM // 4
RWKV_HEADS = 8
RWKV_DIM = RWKV_HEADS * HEAD_DIM
DECAY_LORA = 64
AAA_LORA = 64
GATE_LORA = 160
RWKV_LN_EPS = 64e-5
D_FF = 2816
Q_W = Q_HEADS * HEAD_DIM
KV_W = KV_HEADS * HEAD_DIM
ATTN_PROJ = Q_W + 2 * KV_W
RWKV_PROJ = 3 * RWKV_DIM + DECAY_LORA + AAA_LORA + GATE_LORA
RMS_EPS = 1e-6
NEG_INF = -1e30

LANES = 128
QKV_W = Q_W + 4 * KV_W
GATE_PAD = 256
RW_W = 3 * RWKV_DIM + LANES + GATE_PAD
CHUNK = 64
GROUP = 4 * HEAD_DIM
N_GROUPS = RWKV_DIM // GROUP
ATTN_BLOCKS_PER_STEP = 2
RWKV_ROWS_PER_STEP = 4
VMEM_LIMIT = 56 * 1024 * 1024


def _dot(a, b):
    return jnp.dot(a, b, preferred_element_type=F32)


def _dot_nt(a, b):
    return lax.dot_general(a, b, (((1,), (1,)), ((), ())), preferred_element_type=F32)


def _dot_tn(a, b):
    return lax.dot_general(a, b, (((0,), (0,)), ((), ())), preferred_element_type=F32)


def _sigmoid(x):
    return 1.0 / (1.0 + jnp.exp(-x))


def _const_spec(shape):
    nd = len(shape)
    return pl.BlockSpec(shape, lambda *_: (0,) * nd, pipeline_mode=pl.Buffered(1))


def _inproj_kernel(x_ref, g_ref, wq_ref, wr_ref, wg_ref, bq_ref, br_ref, bg_ref,
                   cos_ref, s1_ref, s2_ref, qkv_ref, p_ref, gate_ref):
    x = x_ref[...]
    ms = jnp.mean(x * x, axis=-1, keepdims=True)
    u = (x * lax.rsqrt(ms + RMS_EPS) * g_ref[...]).astype(BF16)

    cos = cos_ref[...]
    s1 = s1_ref[...]
    s2 = s2_ref[...]
    n_rope = (Q_W + 2 * KV_W) // LANES
    for j in range(QKV_W // LANES):
        sl = slice(j * LANES, (j + 1) * LANES)
        t = _dot(u, wq_ref[:, sl]) + bq_ref[:, sl]
        if j < n_rope:
            t = t * cos + pltpu.roll(t, 8, 1) * s1 + pltpu.roll(t, LANES - 8, 1) * s2
        qkv_ref[:, sl] = t.astype(BF16)

    step = 384
    for j in range(RW_W // step):
        sl = slice(j * step, (j + 1) * step)
        p_ref[:, sl] = _dot(u, wr_ref[:, sl]) + br_ref[:, sl]

    step = 512
    for j in range(2 * D_MODEL // step):
        sl = slice(j * step, (j + 1) * step)
        gate_ref[:, sl] = _sigmoid(_dot(u, wg_ref[:, sl]) + bg_ref[:, sl]).astype(BF16)


def _inproj(x2, g, wq, wr, wg, bq, br, bg, cos, s1, s2, tm):
    n = x2.shape[0]
    seq_tiles = cos.shape[0] // tm
    row = lambda i: (i, 0)
    pos = lambda i: (i % seq_tiles, 0)
    return pl.pallas_call(
        _inproj_kernel,
        grid=(n // tm,),
        in_specs=[
            pl.BlockSpec((tm, D_MODEL), row),
            _const_spec((1, D_MODEL)),
            _const_spec((D_MODEL, QKV_W)),
            _const_spec((D_MODEL, RW_W)),
            _const_spec((D_MODEL, 2 * D_MODEL)),
            _const_spec((1, QKV_W)),
            _const_spec((1, RW_W)),
            _const_spec((1, 2 * D_MODEL)),
            pl.BlockSpec((tm, LANES), pos),
            pl.BlockSpec((tm, LANES), pos),
            pl.BlockSpec((tm, LANES), pos),
        ],
        out_specs=[
            pl.BlockSpec((tm, QKV_W), row),
            pl.BlockSpec((tm, RW_W), row),
            pl.BlockSpec((tm, 2 * D_MODEL), row),
        ],
        out_shape=[
            jax.ShapeDtypeStruct((n, QKV_W), BF16),
            jax.ShapeDtypeStruct((n, RW_W), F32),
            jax.ShapeDtypeStruct((n, 2 * D_MODEL), BF16),
        ],
        compiler_params=pltpu.CompilerParams(
            dimension_semantics=("arbitrary",), vmem_limit_bytes=VMEM_LIMIT),
        name="inproj",
    )(x2, g, wq, wr, wg, bq, br, bg, cos, s1, s2)


def _attn_kernel(sink_ref, q_ref, kvc_ref, kvp_ref, kvm_ref, o_ref, *, qb):
    n = pl.program_id(1)
    blk = WINDOW
    lane = lax.broadcasted_iota(jnp.int32, (1, LANES), 1)
    m_lo = (lane < HEAD_DIM).astype(BF16)
    m_hi = (lane >= HEAD_DIM).astype(BF16)

    kv_all = jnp.concatenate([kvp_ref[...], kvc_ref[...]], axis=0)
    kv_meta = kvm_ref[...]

    qi = lax.broadcasted_iota(jnp.int32, (2 * blk, 2 * blk), 0) % blk
    c = lax.broadcasted_iota(jnp.int32, (2 * blk, 2 * blk), 1)
    cur_ok = jnp.where(c - blk <= qi, 1, 0)
    has_prev = jnp.minimum(n, 1)
    ok_first = jnp.where(c < blk, jnp.where(c > qi, has_prev, 0), cur_ok) > 0
    ok_rest = jnp.where(c < blk, jnp.where(c > qi, 1, 0), cur_ok) > 0
    first = lax.broadcasted_iota(jnp.int32, (2 * blk, 2 * N_META), 1) < N_META
    top = lax.broadcasted_iota(jnp.int32, (2 * blk, 1), 0) < blk

    def halves(kv, g):
        k_plain, k_swap = kv[:, 0:LANES], kv[:, LANES:2 * LANES]
        v_plain, v_swap = kv[:, 2 * LANES:3 * LANES], kv[:, 3 * LANES:4 * LANES]
        if g == 0:
            k_lo, k_hi, v_lo, v_hi = k_plain, k_swap, v_plain, v_swap
        else:
            k_lo, k_hi, v_lo, v_hi = k_swap, k_plain, v_swap, v_plain
        return k_lo * m_lo, k_hi * m_hi, v_lo * m_lo, v_hi * m_hi

    band = [halves(kv_all, g) for g in range(KV_HEADS)]
    meta = [tuple(jnp.concatenate(pair, axis=0) for pair in
                  ((h[0], h[1]), (h[2], h[3])))
            for h in (halves(kv_meta, g) for g in range(KV_HEADS))]

    units = [(j, g) for j in range(qb) for g in range(KV_HEADS)]
    st = {}
    for key in units:
        j, g = key
        rows = slice(j * blk, (j + 2) * blk)
        k_lo, k_hi, v_lo, v_hi = band[g]
        kb = jnp.concatenate([k_lo[rows], k_hi[rows]], axis=0)
        vb = jnp.concatenate([v_lo[rows], v_hi[rows]], axis=0)
        q = jnp.concatenate(
            [q_ref[j * blk:(j + 1) * blk, s * LANES:(s + 1) * LANES] for s in (2 * g, 2 * g + 1)],
            axis=0)
        st[key] = dict(vb=vb,
                       sb=_dot_nt(q, kb),
                       sm=_dot_nt(q, meta[g][0]))

    for key in units:
        j, g = key
        u = st[key]
        ok = ok_first if j == 0 else ok_rest
        sink_a = jnp.where(top, sink_ref[4 * g], sink_ref[4 * g + 2])
        sink_b = jnp.where(top, sink_ref[4 * g + 1], sink_ref[4 * g + 3])
        sb, sm = u["sb"], u["sm"]
        sa = jnp.where(ok, sb[:, :2 * blk], NEG_INF)
        sbb = jnp.where(ok, sb[:, 2 * blk:], NEG_INF)
        mx_a = jnp.maximum(
            jnp.maximum(jnp.max(sa, axis=-1, keepdims=True),
                        jnp.max(jnp.where(first, sm, NEG_INF), axis=-1, keepdims=True)),
            sink_a)
        mx_b = jnp.maximum(
            jnp.maximum(jnp.max(sbb, axis=-1, keepdims=True),
                        jnp.max(jnp.where(first, NEG_INF, sm), axis=-1, keepdims=True)),
            sink_b)
        pa = jnp.exp(sa - mx_a)
        pb = jnp.exp(sbb - mx_b)
        pm = jnp.exp(sm - jnp.where(first, mx_a, mx_b))
        den_a = (jnp.sum(pa, axis=-1, keepdims=True)
                 + jnp.sum(jnp.where(first, pm, 0.0), axis=-1, keepdims=True)
                 + jnp.exp(sink_a - mx_a))
        den_b = (jnp.sum(pb, axis=-1, keepdims=True)
                 + jnp.sum(jnp.where(first, 0.0, pm), axis=-1, keepdims=True)
                 + jnp.exp(sink_b - mx_b))
        inv_a = 1.0 / den_a
        inv_b = 1.0 / den_b
        u["p_band"] = jnp.concatenate([pa * inv_a, pb * inv_b], axis=1).astype(BF16)
        u["p_meta"] = (pm * jnp.where(first, inv_a, inv_b)).astype(BF16)

    for key in units:
        j, g = key
        u = st[key]
        o = (_dot(u["p_band"], u["vb"]) + _dot(u["p_meta"], meta[g][1])).astype(BF16)
        for i, s in enumerate((2 * g, 2 * g + 1)):
            o_ref[j * blk:(j + 1) * blk, s * LANES:(s + 1) * LANES] = o[i * blk:(i + 1) * blk]


def _attn(sinks, qkv, qkv_meta, batch, seq, qb):
    nb = seq // WINDOW
    steps = nb // qb
    n = batch * seq
    kvw = 4 * KV_W
    rows = qb * WINDOW
    return pl.pallas_call(
        functools.partial(_attn_kernel, qb=qb),
        grid=(batch, steps),
        in_specs=[
            pl.BlockSpec(memory_space=pltpu.SMEM),
            pl.BlockSpec((rows, Q_W), lambda b, i: (b * steps + i, 0)),
            pl.BlockSpec((rows, kvw), lambda b, i: (b * steps + i, 1)),
            pl.BlockSpec((WINDOW, kvw), lambda b, i: (jnp.maximum((b * steps + i) * qb - 1, 0), 1)),
            pl.BlockSpec((N_META, kvw), lambda b, i: (0, 1)),
        ],
        out_specs=pl.BlockSpec((rows, Q_W), lambda b, i: (b * steps + i, 0)),
        out_shape=jax.ShapeDtypeStruct((n, Q_W), BF16),
        compiler_params=pltpu.CompilerParams(
            dimension_semantics=("arbitrary", "arbitrary"), vmem_limit_bytes=VMEM_LIMIT),
        name="attn",
    )(sinks, qkv, qkv, qkv, qkv_meta)


def _split2(x):
    hi = x.astype(BF16)
    lo = (x - hi.astype(F32)).astype(BF16)
    return hi, lo


def _rwkv_kernel(p_ref, pprev_ref, s0_ref, mix_ref, w0_ref, a0_ref, w2a_ref, g2_ref,
                 kk_ref, ka_ref, rk_ref, lnw_ref, lnb_ref,
                 tri_ref, bd_ref, strict_ref, incl_ref, eye_ref,
                 y_ref, sfin_ref, pbuf, s_scr, *, nc):
    c = pl.program_id(1)
    tok = nc * CHUNK

    @pl.when(c == 0)
    def _():
        for bi in range(nc):
            pbuf[bi, 0:8, :] = pprev_ref[...]
            s_scr[bi] = s0_ref[...]

    p3 = p_ref[...]
    pbuf[:, 8:8 + CHUNK, :] = p3
    psh = pbuf[:, 7:7 + CHUNK, :].reshape(tok, RW_W)
    pbuf[:, 0:8, :] = p3[:, CHUNK - 8:CHUNK, :]
    p = p3.reshape(tok, RW_W)
    pf = p + (psh - p) * mix_ref[...]

    o1, o2, o3 = RWKV_DIM, 2 * RWKV_DIM, 3 * RWKV_DIM
    r = pf[:, :o1]
    k = pf[:, o1:o2]
    v = pf[:, o2:o3]
    dwa = pf[:, o3:o3 + LANES]
    dg = pf[:, o3 + LANES:]

    lane = lax.broadcasted_iota(jnp.int32, (tok, LANES), 1)
    z = jnp.where(lane < DECAY_LORA, jnp.tanh(dwa), dwa).astype(BF16)
    wa = _dot(z, w2a_ref[...])
    zw = -(w0_ref[...] + wa[:, :o1])
    softplus = jnp.maximum(zw, 0.0) + jnp.log(1.0 + jnp.exp(-jnp.abs(zw)))
    logw = -jnp.exp(-softplus - 0.5)
    a = _sigmoid(a0_ref[...] + wa[:, o1:])
    gate = _dot(_sigmoid(dg).astype(BF16), g2_ref[...])

    bd = bd_ref[...]
    bd_f = bd.astype(F32)
    kkv = k * kk_ref[...]
    kp = k * (1.0 + (a - 1.0) * ka_ref[...])
    rkk = r * kp * rk_ref[...]

    def seg_sums(xs):
        parts = []
        for x in xs:
            parts.extend(_split2(x))
        out = _dot(jnp.concatenate(parts, axis=0), bd)
        return [out[2 * i * tok:(2 * i + 1) * tok] + out[(2 * i + 1) * tok:(2 * i + 2) * tok]
                for i in range(len(xs))]

    def grp(x, gi):
        return x[:, gi * GROUP:(gi + 1) * GROUP]

    sq = kkv * kkv
    sums = seg_sums([grp(sq, 0), grp(sq, 1), grp(rkk, 0), grp(rkk, 1)])
    ss = jnp.concatenate(sums[0:2], axis=1)
    bonus = jnp.concatenate(sums[2:4], axis=1)
    kkn = kkv / jnp.maximum(jnp.sqrt(ss), 1e-12)
    aa = -kkn
    bb = kkn * a

    tri = tri_ref[...]
    strict = strict_ref[...]
    incl = incl_ref[...]
    eye = eye_ref[...]

    def blk(x):
        xb = x.astype(BF16)
        return jnp.concatenate([xb, xb, xb, xb], axis=0) * bd

    lp_grp, lw_grp = [], []
    for gi in range(N_GROUPS):
        lw_all = grp(logw, gi)
        hi = lw_all.astype(BF16)
        r1 = lw_all - hi.astype(F32)
        mid = r1.astype(BF16)
        lo = (r1 - mid.astype(F32)).astype(BF16)
        cs = _dot(tri, jnp.concatenate([hi, mid, lo], axis=1))
        lp_grp.append(cs[:, :GROUP] + cs[:, GROUP:2 * GROUP] + cs[:, 2 * GROUP:])
        lw_grp.append(lw_all)

    pairs = [(gi, ci) for ci in range(nc) for gi in range(N_GROUPS)]
    st = {}
    for key in pairs:
        gi, ci = key
        rows = slice(ci * CHUNK, (ci + 1) * CHUNK)
        cut = lambda x: x[rows, gi * GROUP:(gi + 1) * GROUP]
        lp = lp_grp[gi][rows]
        lw = lw_grp[gi][rows]
        lpc = lp[CHUNK - 1:CHUNK, :]
        e_neg = jnp.exp(-lp)
        e_end = jnp.exp(lpc - lp)
        rg, kg, vg, ag, bg = cut(r), cut(kp), cut(v), cut(aa), cut(bb)
        at = ag * jnp.exp(lp - lw)
        rt = rg * jnp.exp(lp)
        st[key] = dict(
            at=at, rt=rt.astype(BF16), vg=vg, vblk=blk(vg), pc=jnp.exp(lpc),
            bt=blk(bg * e_neg), kt=blk(kg * e_neg),
            ar=jnp.concatenate([at, rt], axis=0).astype(BF16),
            bk_end=jnp.concatenate([bg * e_end, kg * e_end], axis=0).astype(BF16))

    for key in pairs:
        q = st[key]
        sb = _dot_nt(q["ar"], q["bt"])
        sk = _dot_nt(q["ar"], q["kt"])
        q["a_ab"] = sb[:CHUNK] * strict
        q["a_ak"] = (sk[:CHUNK] * strict).astype(BF16)
        q["arr"] = jnp.concatenate([sb[CHUNK:] * incl, sk[CHUNK:] * incl], axis=1).astype(BF16)

    for key in pairs:
        q = st[key]
        q["pw"] = _dot(q["a_ab"].astype(BF16), blk(q["a_ab"]))
        q["minv"] = eye + q["a_ab"]
    for j in range(1, 6):
        for key in pairs:
            q = st[key]
            if j < 5:
                out = _dot(q["pw"].astype(BF16),
                           jnp.concatenate([blk(q["pw"]), blk(q["minv"])], axis=1))
                q["pw"] = out[:, :GROUP]
                q["minv"] = q["minv"] + out[:, GROUP:]
            else:
                q["minv"] = q["minv"] + _dot(q["pw"].astype(BF16), blk(q["minv"]))

    for key in pairs:
        q = st[key]
        q["av"] = _dot(q["a_ak"], q["vblk"])
    for key in pairs:
        q = st[key]
        wu = _dot(q["minv"].astype(BF16), jnp.concatenate([blk(q["at"]), blk(q["av"])], axis=1))
        q["w"] = wu[:, :GROUP].astype(BF16)
        q["u0"] = wu[:, GROUP:]

    for key in pairs:
        gi, ci = key
        q = st[key]
        q["s"] = s_scr[ci, gi]
        q["s_b"] = q["s"].astype(BF16)
        q["u"] = _dot_nt(q["w"], q["s_b"]) + q["u0"]
    for key in pairs:
        q = st[key]
        q["y"] = (_dot_nt(q["rt"], q["s_b"])
                  + _dot(q["arr"], jnp.concatenate([blk(q["u"]), q["vblk"]], axis=0)))
    for key in pairs:
        gi, ci = key
        q = st[key]
        upd = _dot_tn(jnp.concatenate([q["u"], q["vg"]], axis=0).astype(BF16), q["bk_end"])
        s_scr[ci, gi] = q["s"] * q["pc"] + upd * bd_f
    ys = []
    for gi in range(N_GROUPS):
        y_rows = [st[gi, ci]["y"] for ci in range(nc)]
        ys.append(jnp.concatenate(y_rows, axis=0) if nc > 1 else y_rows[0])

    means = seg_sums(ys)
    ds = [ys[i] - means[i] * (1.0 / HEAD_DIM) for i in range(N_GROUPS)]
    vars_ = seg_sums([d * d for d in ds])
    yn = jnp.concatenate(
        [ds[i] * lax.rsqrt(vars_[i] * (1.0 / HEAD_DIM) + RWKV_LN_EPS) for i in range(N_GROUPS)],
        axis=1)
    yn = yn * lnw_ref[...] + lnb_ref[...]
    y_ref[...] = ((yn + bonus * v) * gate).astype(BF16).reshape(nc, CHUNK, RWKV_DIM)

    @pl.when(c == pl.num_programs(1) - 1)
    def _():
        sfin_ref[...] = s_scr[...]


def _rwkv(p, pprev, s0, vecs, w2a, g2p, consts, batch, n_chunks, nc):
    mix, w0, a0, k_k, k_a, r_k, ln_w, ln_b = vecs
    tri, bd, strict, incl, eye = consts
    tok = nc * CHUNK
    vec = _const_spec((1, RWKV_DIM))
    return pl.pallas_call(
        functools.partial(_rwkv_kernel, nc=nc),
        grid=(batch // nc, n_chunks),
        in_specs=[
            pl.BlockSpec((nc, CHUNK, RW_W), lambda b, c: (b, c, 0)),
            _const_spec((8, RW_W)),
            _const_spec((N_GROUPS, GROUP, GROUP)),
            _const_spec((1, RW_W)),
            vec, vec,
            _const_spec((LANES, 2 * RWKV_DIM)),
            _const_spec((GATE_PAD, RWKV_DIM)),
            vec, vec, vec, vec, vec,
            _const_spec((tok, tok)),
            _const_spec((GROUP, GROUP)),
            _const_spec((CHUNK, GROUP)),
            _const_spec((CHUNK, GROUP)),
            _const_spec((CHUNK, GROUP)),
        ],
        out_specs=[
            pl.BlockSpec((nc, CHUNK, RWKV_DIM), lambda b, c: (b, c, 0)),
            pl.BlockSpec((nc, N_GROUPS, GROUP, GROUP), lambda b, c: (b, 0, 0, 0)),
        ],
        out_shape=[
            jax.ShapeDtypeStruct((batch, n_chunks * CHUNK, RWKV_DIM), BF16),
            jax.ShapeDtypeStruct((batch, N_GROUPS, GROUP, GROUP), F32),
        ],
        scratch_shapes=[
            pltpu.VMEM((nc, 8 + CHUNK, RW_W), F32),
            pltpu.VMEM((nc, N_GROUPS, GROUP, GROUP), F32),
        ],
        compiler_params=pltpu.CompilerParams(
            dimension_semantics=("arbitrary", "arbitrary"), vmem_limit_bytes=VMEM_LIMIT),
        name="rwkv",
    )(p, pprev, s0, mix, w0, a0, w2a, g2p, k_k, k_a, r_k, ln_w, ln_b,
      tri, bd, strict, incl, eye)


def _post_kernel(ya_ref, yr_ref, gate_ref, x_ref, wba_ref, wbr_ref, wo_ref, gf_ref,
                 wg_ref, wu_ref, wd_ref, gn_ref, o_ref):
    gates = gate_ref[...].astype(F32)
    merged = (gates[:, :D_MODEL] * _dot(ya_ref[...], wba_ref[...])
              + gates[:, D_MODEL:] * _dot(yr_ref[...], wbr_ref[...]))
    h = x_ref[...] + _dot(merged.astype(BF16), wo_ref[...])
    ms = jnp.mean(h * h, axis=-1, keepdims=True)
    f = (h * lax.rsqrt(ms + RMS_EPS) * gf_ref[...]).astype(BF16)
    gt = _dot(f, wg_ref[...])
    up = _dot(f, wu_ref[...])
    act = (gt * _sigmoid(gt) * up).astype(BF16)
    h = h + _dot(act, wd_ref[...])
    ms = jnp.mean(h * h, axis=-1, keepdims=True)
    o_ref[...] = h * lax.rsqrt(ms + RMS_EPS) * gn_ref[...]


def _post(ya, yr, gates, x2, wba, wbr, wo, gf, wg, wu, wd, gn, tm):
    n = x2.shape[0]
    row = lambda i: (i, 0)
    return pl.pallas_call(
        _post_kernel,
        grid=(n // tm,),
        in_specs=[
            pl.BlockSpec((tm, Q_W), row),
            pl.BlockSpec((tm, RWKV_DIM), row),
            pl.BlockSpec((tm, 2 * D_MODEL), row),
            pl.BlockSpec((tm, D_MODEL), row),
            _const_spec((Q_W, D_MODEL)),
            _const_spec((RWKV_DIM, D_MODEL)),
            _const_spec((D_MODEL, D_MODEL)),
            _const_spec((1, D_MODEL)),
            _const_spec((D_MODEL, D_FF)),
            _const_spec((D_MODEL, D_FF)),
            _const_spec((D_FF, D_MODEL)),
            _const_spec((1, D_MODEL)),
        ],
        out_specs=pl.BlockSpec((tm, D_MODEL), row),
        out_shape=jax.ShapeDtypeStruct((n, D_MODEL), F32),
        compiler_params=pltpu.CompilerParams(
            dimension_semantics=("arbitrary",), vmem_limit_bytes=VMEM_LIMIT),
        name="post",
    )(ya, yr, gates, x2, wba, wbr, wo, gf, wg, wu, wd, gn)


def _rope_tables(pos):
    half = ROPE_DIM // 2
    inv_freq = jnp.power(jnp.float32(ROPE_THETA),
                         -jnp.arange(half, dtype=F32) * (2.0 / ROPE_DIM))
    ang = pos.astype(F32)[:, None] * inv_freq[None, :]
    cos, sin = jnp.cos(ang), jnp.sin(ang)
    n = pos.shape[0]
    one = jnp.ones((n, HEAD_DIM - ROPE_DIM), F32)
    zero8 = jnp.zeros((n, half), F32)
    zero48 = jnp.zeros((n, HEAD_DIM - ROPE_DIM), F32)
    c_head = jnp.concatenate([cos, cos, one], axis=1)
    s1_head = jnp.concatenate([zero8, sin, zero48], axis=1)
    s2_head = jnp.concatenate([-sin, zero8, zero48], axis=1)
    dup = lambda t: jnp.concatenate([t, t], axis=1)
    return dup(c_head), dup(s1_head), dup(s2_head)


def _rwkv_constants(nc):
    tt = np.arange(nc * CHUNK)
    tri = ((tt[None, :] <= tt[:, None])
           & (tt[None, :] // CHUNK == tt[:, None] // CHUNK)).astype(np.float32)
    t = np.arange(CHUNK)
    hh = np.arange(GROUP) // HEAD_DIM
    bd = (hh[:, None] == hh[None, :]).astype(np.float32)
    s = np.arange(GROUP) % CHUNK
    strict = (s[None, :] < t[:, None]).astype(np.float32)
    incl = (s[None, :] <= t[:, None]).astype(np.float32)
    eye = (s[None, :] == t[:, None]).astype(np.float32)
    return (jnp.asarray(tri, BF16), jnp.asarray(bd, BF16), jnp.asarray(strict),
            jnp.asarray(incl), jnp.asarray(eye))


def kernel(x, meta_tokens, norm_mix_g, w_in, b_in, attn_sinks, rwkv_mix, rwkv_w0, rwkv_w2, rwkv_a0, rwkv_a2, rwkv_g2, rwkv_k_k, rwkv_k_a, rwkv_r_k, rwkv_ln_w, rwkv_ln_b, w_br_attn, w_br_rwkv, w_o, norm_ffn_g, w_ffn_gate, w_ffn_up, w_ffn_down, norm_final_g):
    batch, seq, _ = x.shape
    layer = 0
    x2 = x.reshape(batch * seq, D_MODEL)

    w = w_in[layer]
    b = b_in[layer]
    scale = HEAD_DIM ** -0.5
    h64 = HEAD_DIM
    kq, kk0, kk1 = Q_W, Q_W + h64, Q_W + 2 * h64
    vq = Q_W + KV_W

    def qkv_cols(t):
        return jnp.concatenate(
            [t[..., :Q_W] * scale,
             t[..., kq:kq + KV_W],
             t[..., kk0:kk1], t[..., kq:kk0],
             t[..., vq:vq + KV_W],
             t[..., vq + h64:vq + 2 * h64], t[..., vq:vq + h64]], axis=-1)

    def rw_cols(t, n_lead):
        pad = jnp.zeros(t.shape[:n_lead] + (GATE_PAD - GATE_LORA,), t.dtype)
        return jnp.concatenate([t, pad], axis=-1)

    wq = qkv_cols(w).astype(BF16)
    bq = qkv_cols(b)[None]
    wr = rw_cols(w[:, ATTN_PROJ:ATTN_PROJ + RWKV_PROJ], 1).astype(BF16)
    br = rw_cols(b[ATTN_PROJ:ATTN_PROJ + RWKV_PROJ], 0)[None]
    wg = w[:, ATTN_PROJ + RWKV_PROJ:].astype(BF16)
    bg = b[ATTN_PROJ + RWKV_PROJ:][None]
    g_mix = norm_mix_g[layer][None]

    mix = rw_cols(rwkv_mix[layer], 0)[None]
    zl = jnp.zeros((DECAY_LORA, RWKV_DIM), F32)
    w2a = jnp.concatenate(
        [jnp.concatenate([rwkv_w2[layer], zl], axis=1),
         jnp.concatenate([zl, rwkv_a2[layer]], axis=1)], axis=0).astype(BF16)
    g2p = jnp.concatenate(
        [rwkv_g2[layer], jnp.zeros((GATE_PAD - GATE_LORA, RWKV_DIM), F32)], axis=0).astype(BF16)
    vecs = (mix, rwkv_w0[layer][None], rwkv_a0[layer][None], rwkv_k_k[layer][None],
            rwkv_k_a[layer][None], rwkv_r_k[layer].reshape(1, RWKV_DIM),
            rwkv_ln_w[layer][None], rwkv_ln_b[layer][None])

    cos_m, s1_m, s2_m = _rope_tables(jnp.arange(N_META, dtype=jnp.int32))
    qkv_m, p_m, _ = _inproj(meta_tokens, g_mix, wq, wr, wg, bq, br, bg,
                            cos_m, s1_m, s2_m, N_META)
    p_m_pad = jnp.concatenate([jnp.zeros((CHUNK - N_META, RW_W), F32), p_m], axis=0)
    zero_state = jnp.zeros((N_GROUPS, GROUP, GROUP), F32)
    _, s_meta = _rwkv(p_m_pad[None], jnp.zeros((8, RW_W), F32), zero_state, vecs, w2a, g2p,
                      _rwkv_constants(1), 1, 1, 1)

    cos, s1, s2 = _rope_tables(N_META + jnp.arange(seq, dtype=jnp.int32))
    tm_in = min(512, seq)
    qkv, p, gates = _inproj(x2, g_mix, wq, wr, wg, bq, br, bg, cos, s1, s2, tm_in)
    y_attn = _attn(attn_sinks[layer], qkv, qkv_m, batch, seq, ATTN_BLOCKS_PER_STEP)
    nc = min(RWKV_ROWS_PER_STEP, batch)
    y_rwkv, _ = _rwkv(p.reshape(batch, seq, RW_W), p_m[N_META - 8:], s_meta[0], vecs, w2a, g2p,
                      _rwkv_constants(nc), batch, seq // CHUNK, nc)
    out = _post(y_attn, y_rwkv.reshape(batch * seq, RWKV_DIM), gates, x2,
                w_br_attn[layer].astype(BF16), w_br_rwkv[layer].astype(BF16),
                w_o[layer].astype(BF16), norm_ffn_g[layer][None],
                w_ffn_gate[layer].astype(BF16), w_ffn_up[layer].astype(BF16),
                w_ffn_down[layer].astype(BF16), norm_final_g[None], min(256, seq))
    return out.reshape(batch, seq, D_MODEL)
```

```python
import functools

import jax
import jax.numpy as jnp
import numpy as np
from jax import lax
from jax.experimental import pallas as pl
from jax.experimental.pallas import tpu as pltpu

F32 = jnp.float32
BF16 = jnp.bfloat16

D_MODEL = 1024
N_META = 16
HEAD_DIM = 64
Q_HEADS = 8
KV_HEADS = 2
WINDOW = 128
ROPE_THETA = 500000.0
ROPE_DIM = HEAD_DIM // 4
RWKV_HEADS = 8
RWKV_DIM = RWKV_HEADS * HEAD_DIM
DECAY_LORA = 64
AAA_LORA = 64
GATE_LORA = 160
RWKV_LN_EPS = 64e-5
D_FF = 2816
Q_W = Q_HEADS * HEAD_DIM
KV_W = KV_HEADS * HEAD_DIM
ATTN_PROJ = Q_W + 2 * KV_W
RWKV_PROJ = 3 * RWKV_DIM + DECAY_LORA + AAA_LORA + GATE_LORA
RMS_EPS = 1e-6
NEG_INF = -1e30

LANES = 128
MXU_COLS = 256
QKV_W = Q_W + 4 * KV_W
GATE_PAD = 256
RW_W = 3 * RWKV_DIM + LANES + GATE_PAD
CHUNK = 64
GROUP = 4 * HEAD_DIM
N_GROUPS = RWKV_DIM // GROUP
ATTN_BLOCKS_PER_STEP = 2
RWKV_ROWS_PER_STEP = 4
VMEM_LIMIT = 56 * 1024 * 1024


def _dot(a, b):
    return jnp.dot(a, b, preferred_element_type=F32)


def _dot_nt(a, b):
    return lax.dot_general(a, b, (((1,), (1,)), ((), ())), preferred_element_type=F32)


def _dot_tn(a, b):
    return lax.dot_general(a, b, (((0,), (0,)), ((), ())), preferred_element_type=F32)


def _sigmoid(x):
    return 1.0 / (1.0 + jnp.exp(-x))


def _const_spec(shape):
    nd = len(shape)
    return pl.BlockSpec(shape, lambda *_: (0,) * nd, pipeline_mode=pl.Buffered(1))


def _inproj_kernel(x_ref, g_ref, wq_ref, wr_ref, wg_ref, bq_ref, br_ref, bg_ref,
                   cos_ref, s1_ref, s2_ref, pprev_ref, mix_ref,
                   qkv_ref, p_ref, gate_ref, plast_ref, pbuf, *, tm, seq_tiles):
    @pl.when(pl.program_id(0) % seq_tiles == 0)
    def _():
        pbuf[0:8, :] = pprev_ref[...]

    x = x_ref[...]
    ms = jnp.mean(x * x, axis=-1, keepdims=True)
    u = (x * lax.rsqrt(ms + RMS_EPS) * g_ref[...]).astype(BF16)

    cos = cos_ref[...]
    s1 = s1_ref[...]
    s2 = s2_ref[...]
    n_rope = (Q_W + 2 * KV_W) // LANES
    for j in range(QKV_W // MXU_COLS):
        sl = slice(j * MXU_COLS, (j + 1) * MXU_COLS)
        t2 = _dot(u, wq_ref[:, sl]) + bq_ref[:, sl]
        for h in range(MXU_COLS // LANES):
            slab = j * (MXU_COLS // LANES) + h
            t = t2[:, h * LANES:(h + 1) * LANES]
            if slab < n_rope:
                t = t * cos + pltpu.roll(t, 8, 1) * s1 + pltpu.roll(t, LANES - 8, 1) * s2
            qkv_ref[:, slab * LANES:(slab + 1) * LANES] = t.astype(BF16)

    for lo in range(0, RW_W, MXU_COLS):
        sl = slice(lo, min(lo + MXU_COLS, RW_W))
        pc = _dot(u, wr_ref[:, sl]) + br_ref[:, sl]
        pbuf[8:8 + tm, sl] = pc
        psh = pbuf[7:7 + tm, sl]
        p_ref[:, sl] = pc + (psh - pc) * mix_ref[:, sl]
        pbuf[0:8, sl] = pc[tm - 8:tm, :]
        plast_ref[:, sl] = pc[tm - 8:tm, :]

    for lo in range(0, 2 * D_MODEL, MXU_COLS):
        sl = slice(lo, lo + MXU_COLS)
        gate_ref[:, sl] = _sigmoid(_dot(u, wg_ref[:, sl]) + bg_ref[:, sl]).astype(BF16)


def _inproj(x2, g, wq, wr, wg, bq, br, bg, cos, s1, s2, pprev, mix, tm):
    n = x2.shape[0]
    seq_tiles = cos.shape[0] // tm
    row = lambda i: (i, 0)
    pos = lambda i: (i % seq_tiles, 0)
    return pl.pallas_call(
        functools.partial(_inproj_kernel, tm=tm, seq_tiles=seq_tiles),
        grid=(n // tm,),
        in_specs=[
            pl.BlockSpec((tm, D_MODEL), row),
            _const_spec((1, D_MODEL)),
            _const_spec((D_MODEL, QKV_W)),
            _const_spec((D_MODEL, RW_W)),
            _const_spec((D_MODEL, 2 * D_MODEL)),
            _const_spec((1, QKV_W)),
            _const_spec((1, RW_W)),
            _const_spec((1, 2 * D_MODEL)),
            pl.BlockSpec((tm, LANES), pos),
            pl.BlockSpec((tm, LANES), pos),
            pl.BlockSpec((tm, LANES), pos),
            _const_spec((8, RW_W)),
            _const_spec((1, RW_W)),
        ],
        out_specs=[
            pl.BlockSpec((tm, QKV_W), row),
            pl.BlockSpec((tm, RW_W), row),
            pl.BlockSpec((tm, 2 * D_MODEL), row),
            pl.BlockSpec((8, RW_W), lambda i: (0, 0)),
        ],
        out_shape=[
            jax.ShapeDtypeStruct((n, QKV_W), BF16),
            jax.ShapeDtypeStruct((n, RW_W), F32),
            jax.ShapeDtypeStruct((n, 2 * D_MODEL), BF16),
            jax.ShapeDtypeStruct((8, RW_W), F32),
        ],
        scratch_shapes=[pltpu.VMEM((8 + tm, RW_W), F32)],
        compiler_params=pltpu.CompilerParams(
            dimension_semantics=("arbitrary",), vmem_limit_bytes=VMEM_LIMIT),
        name="inproj",
    )(x2, g, wq, wr, wg, bq, br, bg, cos, s1, s2, pprev, mix)


def _attn_kernel(sink_ref, q_ref, kvc_ref, kvp_ref, kvm_ref, o_ref, *, qb):
    n = pl.program_id(1)
    blk = WINDOW
    lane = lax.broadcasted_iota(jnp.int32, (1, LANES), 1)
    m_lo = (lane < HEAD_DIM).astype(BF16)
    m_hi = (lane >= HEAD_DIM).astype(BF16)

    kv_all = jnp.concatenate([kvp_ref[...], kvc_ref[...]], axis=0)
    kv_meta = kvm_ref[...]

    qi = lax.broadcasted_iota(jnp.int32, (2 * blk, 2 * blk), 0) % blk
    c = lax.broadcasted_iota(jnp.int32, (2 * blk, 2 * blk), 1)
    cur_ok = jnp.where(c - blk <= qi, 1, 0)
    has_prev = jnp.minimum(n, 1)
    ok_first = jnp.where(c < blk, jnp.where(c > qi, has_prev, 0), cur_ok) > 0
    ok_rest = jnp.where(c < blk, jnp.where(c > qi, 1, 0), cur_ok) > 0
    first = lax.broadcasted_iota(jnp.int32, (2 * blk, 2 * N_META), 1) < N_META
    top = lax.broadcasted_iota(jnp.int32, (2 * blk, 1), 0) < blk

    def halves(kv, g):
        k_plain, k_swap = kv[:, 0:LANES], kv[:, LANES:2 * LANES]
        v_plain, v_swap = kv[:, 2 * LANES:3 * LANES], kv[:, 3 * LANES:4 * LANES]
        if g == 0:
            k_lo, k_hi, v_lo, v_hi = k_plain, k_swap, v_plain, v_swap
        else:
            k_lo, k_hi, v_lo, v_hi = k_swap, k_plain, v_swap, v_plain
        return k_lo * m_lo, k_hi * m_hi, v_lo * m_lo, v_hi * m_hi

    band = [halves(kv_all, g) for g in range(KV_HEADS)]
    meta = [tuple(jnp.concatenate(pair, axis=0) for pair in
                  ((h[0], h[1]), (h[2], h[3])))
            for h in (halves(kv_meta, g) for g in range(KV_HEADS))]

    units = [(j, g) for j in range(qb) for g in range(KV_HEADS)]
    st = {}
    for key in units:
        j, g = key
        rows = slice(j * blk, (j + 2) * blk)
        k_lo, k_hi, v_lo, v_hi = band[g]
        kb = jnp.concatenate([k_lo[rows], k_hi[rows]], axis=0)
        vb = jnp.concatenate([v_lo[rows], v_hi[rows]], axis=0)
        q = jnp.concatenate(
            [q_ref[j * blk:(j + 1) * blk, s * LANES:(s + 1) * LANES] for s in (2 * g, 2 * g + 1)],
            axis=0)
        st[key] = dict(vb=vb,
                       sb=_dot_nt(q, kb),
                       sm=_dot_nt(q, meta[g][0]))

    for key in units:
        j, g = key
        u = st[key]
        ok = ok_first if j == 0 else ok_rest
        sink_a = jnp.where(top, sink_ref[4 * g], sink_ref[4 * g + 2])
        sink_b = jnp.where(top, sink_ref[4 * g + 1], sink_ref[4 * g + 3])
        sb, sm = u["sb"], u["sm"]
        sa = jnp.where(ok, sb[:, :2 * blk], NEG_INF)
        sbb = jnp.where(ok, sb[:, 2 * blk:], NEG_INF)
        mx_a = jnp.maximum(
            jnp.maximum(jnp.max(sa, axis=-1, keepdims=True),
                        jnp.max(jnp.where(first, sm, NEG_INF), axis=-1, keepdims=True)),
            sink_a)
        mx_b = jnp.maximum(
            jnp.maximum(jnp.max(sbb, axis=-1, keepdims=True),
                        jnp.max(jnp.where(first, NEG_INF, sm), axis=-1, keepdims=True)),
            sink_b)
        pa = jnp.exp(sa - mx_a)
        pb = jnp.exp(sbb - mx_b)
        pm = jnp.exp(sm - jnp.where(first, mx_a, mx_b))
        den_a = (jnp.sum(pa, axis=-1, keepdims=True)
                 + jnp.sum(jnp.where(first, pm, 0.0), axis=-1, keepdims=True)
                 + jnp.exp(sink_a - mx_a))
        den_b = (jnp.sum(pb, axis=-1, keepdims=True)
                 + jnp.sum(jnp.where(first, 0.0, pm), axis=-1, keepdims=True)
                 + jnp.exp(sink_b - mx_b))
        inv_a = 1.0 / den_a
        inv_b = 1.0 / den_b
        u["p_band"] = jnp.concatenate([pa * inv_a, pb * inv_b], axis=1).astype(BF16)
        u["p_meta"] = (pm * jnp.where(first, inv_a, inv_b)).astype(BF16)

    for key in units:
        j, g = key
        u = st[key]
        o = (_dot(u["p_band"], u["vb"]) + _dot(u["p_meta"], meta[g][1])).astype(BF16)
        for i, s in enumerate((2 * g, 2 * g + 1)):
            o_ref[j * blk:(j + 1) * blk, s * LANES:(s + 1) * LANES] = o[i * blk:(i + 1) * blk]


def _attn(sinks, qkv, qkv_meta, batch, seq, qb):
    nb = seq // WINDOW
    steps = nb // qb
    n = batch * seq
    kvw = 4 * KV_W
    rows = qb * WINDOW
    return pl.pallas_call(
        functools.partial(_attn_kernel, qb=qb),
        grid=(batch, steps),
        in_specs=[
            pl.BlockSpec(memory_space=pltpu.SMEM),
            pl.BlockSpec((rows, Q_W), lambda b, i: (b * steps + i, 0)),
            pl.BlockSpec((rows, kvw), lambda b, i: (b * steps + i, 1)),
            pl.BlockSpec((WINDOW, kvw), lambda b, i: (jnp.maximum((b * steps + i) * qb - 1, 0), 1)),
            pl.BlockSpec((N_META, kvw), lambda b, i: (0, 1)),
        ],
        out_specs=pl.BlockSpec((rows, Q_W), lambda b, i: (b * steps + i, 0)),
        out_shape=jax.ShapeDtypeStruct((n, Q_W), BF16),
        compiler_params=pltpu.CompilerParams(
            dimension_semantics=("arbitrary", "arbitrary"), vmem_limit_bytes=VMEM_LIMIT),
        name="attn",
    )(sinks, qkv, qkv, qkv, qkv_meta)


def _split2(x):
    hi = x.astype(BF16)
    lo = (x - hi.astype(F32)).astype(BF16)
    return hi, lo


def _rwkv_kernel(p_ref, s0_ref, w0_ref, a0_ref, w2a_ref, g2_ref,
                 kk_ref, ka_ref, rk_ref, lnw_ref, lnb_ref,
                 tri_ref, bd_ref, strict_ref, incl_ref, eye_ref,
                 y_ref, sfin_ref, s_scr, *, nc):
    c = pl.program_id(1)
    tok = nc * CHUNK

    @pl.when(c == 0)
    def _():
        for bi in range(nc):
            s_scr[bi] = s0_ref[...]

    pf = p_ref[...].reshape(tok, RW_W)

    o1, o2, o3 = RWKV_DIM, 2 * RWKV_DIM, 3 * RWKV_DIM
    r = pf[:, :o1]
    k = pf[:, o1:o2]
    v = pf[:, o2:o3]
    dwa = pf[:, o3:o3 + LANES]
    dg = pf[:, o3 + LANES:]

    lane = lax.broadcasted_iota(jnp.int32, (tok, LANES), 1)
    z = jnp.where(lane < DECAY_LORA, jnp.tanh(dwa), dwa).astype(BF16)
    wa = _dot(z, w2a_ref[...])
    zw = -(w0_ref[...] + wa[:, :o1])
    softplus = jnp.maximum(zw, 0.0) + jnp.log(1.0 + jnp.exp(-jnp.abs(zw)))
    logw = -jnp.exp(-softplus - 0.5)
    a = _sigmoid(a0_ref[...] + wa[:, o1:])
    gate = _dot(_sigmoid(dg).astype(BF16), g2_ref[...])

    bd = bd_ref[...]
    bd_f = bd.astype(F32)
    kkv = k * kk_ref[...]
    kp = k * (1.0 + (a - 1.0) * ka_ref[...])
    rkk = r * kp * rk_ref[...]

    def seg_sums(xs):
        parts = []
        for x in xs:
            parts.extend(_split2(x))
        out = _dot(jnp.concatenate(parts, axis=0), bd)
        return [out[2 * i * tok:(2 * i + 1) * tok] + out[(2 * i + 1) * tok:(2 * i + 2) * tok]
                for i in range(len(xs))]

    def grp(x, gi):
        return x[:, gi * GROUP:(gi + 1) * GROUP]

    sq = kkv * kkv
    sums = seg_sums([grp(sq, 0), grp(sq, 1), grp(rkk, 0), grp(rkk, 1)])
    ss = jnp.concatenate(sums[0:2], axis=1)
    bonus = jnp.concatenate(sums[2:4], axis=1)
    kkn = kkv / jnp.maximum(jnp.sqrt(ss), 1e-12)
    aa = -kkn
    bb = kkn * a

    tri = tri_ref[...]
    strict = strict_ref[...]
    incl = incl_ref[...]
    eye = eye_ref[...]

    lane1 = lax.broadcasted_iota(jnp.int32, (1, LANES), 1)
    head_lo = (lane1 < HEAD_DIM).astype(BF16)
    head_hi = (lane1 >= HEAD_DIM).astype(BF16)
    zero_slab = jnp.zeros((CHUNK, LANES), BF16)

    def blk(x):
        xb = x.astype(BF16)
        left, right = xb[:, :LANES], xb[:, LANES:]
        return jnp.concatenate(
            [jnp.concatenate([left * head_lo, zero_slab], axis=1),
             jnp.concatenate([left * head_hi, zero_slab], axis=1),
             jnp.concatenate([zero_slab, right * head_lo], axis=1),
             jnp.concatenate([zero_slab, right * head_hi], axis=1)], axis=0)

    lp_grp, lw_grp = [], []
    for gi in range(N_GROUPS):
        lw_all = grp(logw, gi)
        hi = lw_all.astype(BF16)
        r1 = lw_all - hi.astype(F32)
        mid = r1.astype(BF16)
        lo = (r1 - mid.astype(F32)).astype(BF16)
        cs = _dot(tri, jnp.concatenate([hi, mid, lo], axis=1))
        lp_grp.append(cs[:, :GROUP] + cs[:, GROUP:2 * GROUP] + cs[:, 2 * GROUP:])
        lw_grp.append(lw_all)

    pairs = [(gi, ci) for ci in range(nc) for gi in range(N_GROUPS)]
    st = {}
    for key in pairs:
        gi, ci = key
        rows = slice(ci * CHUNK, (ci + 1) * CHUNK)
        cut = lambda x: x[rows, gi * GROUP:(gi + 1) * GROUP]
        lp = lp_grp[gi][rows]
        lw = lw_grp[gi][rows]
        lpc = lp[CHUNK - 1:CHUNK, :]
        e_neg = jnp.exp(-lp)
        e_end = jnp.exp(lpc - lp)
        rg, kg, vg, ag, bg = cut(r), cut(kp), cut(v), cut(aa), cut(bb)
        at = ag * jnp.exp(lp - lw)
        rt = rg * jnp.exp(lp)
        st[key] = dict(
            at=at, rt=rt.astype(BF16), vg=vg, vblk=blk(vg), pc=jnp.exp(lpc),
            bt=blk(bg * e_neg), kt=blk(kg * e_neg),
            ar=jnp.concatenate([at, rt], axis=0).astype(BF16),
            bk_end=jnp.concatenate([bg * e_end, kg * e_end], axis=0).astype(BF16))

    for key in pairs:
        q = st[key]
        sb = _dot_nt(q["ar"], q["bt"])
        sk = _dot_nt(q["ar"], q["kt"])
        q["a_ab"] = sb[:CHUNK] * strict
        q["a_ak"] = (sk[:CHUNK] * strict).astype(BF16)
        q["arr"] = jnp.concatenate([sb[CHUNK:] * incl, sk[CHUNK:] * incl], axis=1).astype(BF16)

    for key in pairs:
        q = st[key]
        q["pw"] = _dot(q["a_ab"].astype(BF16), blk(q["a_ab"]))
        q["minv"] = eye + q["a_ab"]
    for j in range(1, 6):
        for key in pairs:
            q = st[key]
            if j < 5:
                out = _dot(q["pw"].astype(BF16),
                           jnp.concatenate([blk(q["pw"]), blk(q["minv"])], axis=1))
                q["pw"] = out[:, :GROUP]
                q["minv"] = q["minv"] + out[:, GROUP:]
            else:
                q["minv"] = q["minv"] + _dot(q["pw"].astype(BF16), blk(q["minv"]))

    for key in pairs:
        q = st[key]
        q["av"] = _dot(q["a_ak"], q["vblk"])
    for key in pairs:
        q = st[key]
        wu = _dot(q["minv"].astype(BF16), jnp.concatenate([blk(q["at"]), blk(q["av"])], axis=1))
        q["w"] = wu[:, :GROUP].astype(BF16)
        q["u0"] = wu[:, GROUP:]

    for key in pairs:
        gi, ci = key
        q = st[key]
        q["s"] = s_scr[ci, gi]
        q["s_b"] = q["s"].astype(BF16)
        q["u"] = _dot_nt(q["w"], q["s_b"]) + q["u0"]
    for key in pairs:
        q = st[key]
        q["y"] = (_dot_nt(q["rt"], q["s_b"])
                  + _dot(q["arr"], jnp.concatenate([blk(q["u"]), q["vblk"]], axis=0)))
    for key in pairs:
        gi, ci = key
        q = st[key]
        upd = _dot_tn(jnp.concatenate([q["u"], q["vg"]], axis=0).astype(BF16), q["bk_end"])
        s_scr[ci, gi] = q["s"] * q["pc"] + upd * bd_f
    ys = []
    for gi in range(N_GROUPS):
        y_rows = [st[gi, ci]["y"] for ci in range(nc)]
        ys.append(jnp.concatenate(y_rows, axis=0) if nc > 1 else y_rows[0])

    means = seg_sums(ys)
    ds = [ys[i] - means[i] * (1.0 / HEAD_DIM) for i in range(N_GROUPS)]
    vars_ = seg_sums([d * d for d in ds])
    yn = jnp.concatenate(
        [ds[i] * lax.rsqrt(vars_[i] * (1.0 / HEAD_DIM) + RWKV_LN_EPS) for i in range(N_GROUPS)],
        axis=1)
    yn = yn * lnw_ref[...] + lnb_ref[...]
    y_ref[...] = ((yn + bonus * v) * gate).astype(BF16).reshape(nc, CHUNK, RWKV_DIM)

    @pl.when(c == pl.num_programs(1) - 1)
    def _():
        sfin_ref[...] = s_scr[...]


def _rwkv(p, s0, vecs, w2a, g2p, consts, batch, n_chunks, nc):
    w0, a0, k_k, k_a, r_k, ln_w, ln_b = vecs
    tri, bd, strict, incl, eye = consts
    tok = nc * CHUNK
    vec = _const_spec((1, RWKV_DIM))
    return pl.pallas_call(
        functools.partial(_rwkv_kernel, nc=nc),
        grid=(batch // nc, n_chunks),
        in_specs=[
            pl.BlockSpec((nc, CHUNK, RW_W), lambda b, c: (b, c, 0)),
            _const_spec((N_GROUPS, GROUP, GROUP)),
            vec, vec,
            _const_spec((LANES, 2 * RWKV_DIM)),
            _const_spec((GATE_PAD, RWKV_DIM)),
            vec, vec, vec, vec, vec,
            _const_spec((tok, tok)),
            _const_spec((GROUP, GROUP)),
            _const_spec((CHUNK, GROUP)),
            _const_spec((CHUNK, GROUP)),
            _const_spec((CHUNK, GROUP)),
        ],
        out_specs=[
            pl.BlockSpec((nc, CHUNK, RWKV_DIM), lambda b, c: (b, c, 0)),
            pl.BlockSpec((nc, N_GROUPS, GROUP, GROUP), lambda b, c: (b, 0, 0, 0)),
        ],
        out_shape=[
            jax.ShapeDtypeStruct((batch, n_chunks * CHUNK, RWKV_DIM), BF16),
            jax.ShapeDtypeStruct((batch, N_GROUPS, GROUP, GROUP), F32),
        ],
        scratch_shapes=[
            pltpu.VMEM((nc, N_GROUPS, GROUP, GROUP), F32),
        ],
        compiler_params=pltpu.CompilerParams(
            dimension_semantics=("arbitrary", "arbitrary"), vmem_limit_bytes=VMEM_LIMIT),
        name="rwkv",
    )(p, s0, w0, a0, w2a, g2p, k_k, k_a, r_k, ln_w, ln_b,
      tri, bd, strict, incl, eye)


def _post_kernel(ya_ref, yr_ref, gate_ref, x_ref, wba_ref, wbr_ref, wo_ref, gf_ref,
                 wg_ref, wu_ref, wd_ref, gn_ref, o_ref):
    gates = gate_ref[...].astype(F32)
    merged = (gates[:, :D_MODEL] * _dot(ya_ref[...], wba_ref[...])
              + gates[:, D_MODEL:] * _dot(yr_ref[...], wbr_ref[...]))
    h = x_ref[...] + _dot(merged.astype(BF16), wo_ref[...])
    ms = jnp.mean(h * h, axis=-1, keepdims=True)
    f = (h * lax.rsqrt(ms + RMS_EPS) * gf_ref[...]).astype(BF16)
    gt = _dot(f, wg_ref[...])
    up = _dot(f, wu_ref[...])
    act = (gt * _sigmoid(gt) * up).astype(BF16)
    h = h + _dot(act, wd_ref[...])
    ms = jnp.mean(h * h, axis=-1, keepdims=True)
    o_ref[...] = h * lax.rsqrt(ms + RMS_EPS) * gn_ref[...]


def _post(ya, yr, gates, x2, wba, wbr, wo, gf, wg, wu, wd, gn, tm):
    n = x2.shape[0]
    row = lambda i: (i, 0)
    return pl.pallas_call(
        _post_kernel,
        grid=(n // tm,),
        in_specs=[
            pl.BlockSpec((tm, Q_W), row),
            pl.BlockSpec((tm, RWKV_DIM), row),
            pl.BlockSpec((tm, 2 * D_MODEL), row),
            pl.BlockSpec((tm, D_MODEL), row),
            _const_spec((Q_W, D_MODEL)),
            _const_spec((RWKV_DIM, D_MODEL)),
            _const_spec((D_MODEL, D_MODEL)),
            _const_spec((1, D_MODEL)),
            _const_spec((D_MODEL, D_FF)),
            _const_spec((D_MODEL, D_FF)),
            _const_spec((D_FF, D_MODEL)),
            _const_spec((1, D_MODEL)),
        ],
        out_specs=pl.BlockSpec((tm, D_MODEL), row),
        out_shape=jax.ShapeDtypeStruct((n, D_MODEL), F32),
        compiler_params=pltpu.CompilerParams(
            dimension_semantics=("arbitrary",), vmem_limit_bytes=VMEM_LIMIT),
        name="post",
    )(ya, yr, gates, x2, wba, wbr, wo, gf, wg, wu, wd, gn)


def _rope_tables(pos):
    half = ROPE_DIM // 2
    inv_freq = jnp.power(jnp.float32(ROPE_THETA),
                         -jnp.arange(half, dtype=F32) * (2.0 / ROPE_DIM))
    ang = pos.astype(F32)[:, None] * inv_freq[None, :]
    cos, sin = jnp.cos(ang), jnp.sin(ang)
    n = pos.shape[0]
    one = jnp.ones((n, HEAD_DIM - ROPE_DIM), F32)
    zero8 = jnp.zeros((n, half), F32)
    zero48 = jnp.zeros((n, HEAD_DIM - ROPE_DIM), F32)
    c_head = jnp.concatenate([cos, cos, one], axis=1)
    s1_head = jnp.concatenate([zero8, sin, zero48], axis=1)
    s2_head = jnp.concatenate([-sin, zero8, zero48], axis=1)
    dup = lambda t: jnp.concatenate([t, t], axis=1)
    return dup(c_head), dup(s1_head), dup(s2_head)


def _rwkv_constants(nc):
    tt = np.arange(nc * CHUNK)
    tri = ((tt[None, :] <= tt[:, None])
           & (tt[None, :] // CHUNK == tt[:, None] // CHUNK)).astype(np.float32)
    t = np.arange(CHUNK)
    hh = np.arange(GROUP) // HEAD_DIM
    bd = (hh[:, None] == hh[None, :]).astype(np.float32)
    s = np.arange(GROUP) % CHUNK
    strict = (s[None, :] < t[:, None]).astype(np.float32)
    incl = (s[None, :] <= t[:, None]).astype(np.float32)
    eye = (s[None, :] == t[:, None]).astype(np.float32)
    return (jnp.asarray(tri, BF16), jnp.asarray(bd, BF16), jnp.asarray(strict),
            jnp.asarray(incl), jnp.asarray(eye))


def kernel(x, meta_tokens, norm_mix_g, w_in, b_in, attn_sinks, rwkv_mix, rwkv_w0, rwkv_w2, rwkv_a0, rwkv_a2, rwkv_g2, rwkv_k_k, rwkv_k_a, rwkv_r_k, rwkv_ln_w, rwkv_ln_b, w_br_attn, w_br_rwkv, w_o, norm_ffn_g, w_ffn_gate, w_ffn_up, w_ffn_down, norm_final_g):
    batch, seq, _ = x.shape
    layer = 0
    x2 = x.reshape(batch * seq, D_MODEL)

    w = w_in[layer]
    b = b_in[layer]
    scale = HEAD_DIM ** -0.5
    h64 = HEAD_DIM
    kq, kk0, kk1 = Q_W, Q_W + h64, Q_W + 2 * h64
    vq = Q_W + KV_W

    def qkv_cols(t):
        return jnp.concatenate(
            [t[..., :Q_W] * scale,
             t[..., kq:kq + KV_W],
             t[..., kk0:kk1], t[..., kq:kk0],
             t[..., vq:vq + KV_W],
             t[..., vq + h64:vq + 2 * h64], t[..., vq:vq + h64]], axis=-1)

    def rw_cols(t, n_lead):
        pad = jnp.zeros(t.shape[:n_lead] + (GATE_PAD - GATE_LORA,), t.dtype)
        return jnp.concatenate([t, pad], axis=-1)

    wq = qkv_cols(w).astype(BF16)
    bq = qkv_cols(b)[None]
    wr = rw_cols(w[:, ATTN_PROJ:ATTN_PROJ + RWKV_PROJ], 1).astype(BF16)
    br = rw_cols(b[ATTN_PROJ:ATTN_PROJ + RWKV_PROJ], 0)[None]
    wg = w[:, ATTN_PROJ + RWKV_PROJ:].astype(BF16)
    bg = b[ATTN_PROJ + RWKV_PROJ:][None]
    g_mix = norm_mix_g[layer][None]

    mix = rw_cols(rwkv_mix[layer], 0)[None]
    zl = jnp.zeros((DECAY_LORA, RWKV_DIM), F32)
    w2a = jnp.concatenate(
        [jnp.concatenate([rwkv_w2[layer], zl], axis=1),
         jnp.concatenate([zl, rwkv_a2[layer]], axis=1)], axis=0).astype(BF16)
    g2p = jnp.concatenate(
        [rwkv_g2[layer], jnp.zeros((GATE_PAD - GATE_LORA, RWKV_DIM), F32)], axis=0).astype(BF16)
    vecs = (rwkv_w0[layer][None], rwkv_a0[layer][None], rwkv_k_k[layer][None],
            rwkv_k_a[layer][None], rwkv_r_k[layer].reshape(1, RWKV_DIM),
            rwkv_ln_w[layer][None], rwkv_ln_b[layer][None])

    cos_m, s1_m, s2_m = _rope_tables(jnp.arange(N_META, dtype=jnp.int32))
    qkv_m, p_m, _, p_m_tail = _inproj(meta_tokens, g_mix, wq, wr, wg, bq, br, bg,
                                      cos_m, s1_m, s2_m, jnp.zeros((8, RW_W), F32), mix, N_META)
    p_m_pad = jnp.concatenate([jnp.zeros((CHUNK - N_META, RW_W), F32), p_m], axis=0)
    zero_state = jnp.zeros((N_GROUPS, GROUP, GROUP), F32)
    _, s_meta = _rwkv(p_m_pad[None], zero_state, vecs, w2a, g2p, _rwkv_constants(1), 1, 1, 1)

    cos, s1, s2 = _rope_tables(N_META + jnp.arange(seq, dtype=jnp.int32))
    tm_in = min(512, seq)
    qkv, p, gates, _ = _inproj(x2, g_mix, wq, wr, wg, bq, br, bg, cos, s1, s2,
                               p_m_tail, mix, tm_in)
    y_attn = _attn(attn_sinks[layer], qkv, qkv_m, batch, seq, ATTN_BLOCKS_PER_STEP)
    nc = min(RWKV_ROWS_PER_STEP, batch)
    y_rwkv, _ = _rwkv(p.reshape(batch, seq, RW_W), s_meta[0], vecs, w2a, g2p,
                      _rwkv_constants(nc), batch, seq // CHUNK, nc)
    out = _post(y_attn, y_rwkv.reshape(batch * seq, RWKV_DIM), gates, x2,
                w_br_attn[layer].astype(BF16), w_br_rwkv[layer].astype(BF16),
                w_o[layer].astype(BF16), norm_ffn_g[layer][None],
                w_ffn_gate[layer].astype(BF16), w_ffn_up[layer].astype(BF16),
                w_ffn_down[layer].astype(BF16), norm_final_g[None], min(256, seq))
    return out.reshape(batch, seq, D_MODEL)
```

```python
import functools

import jax
import jax.numpy as jnp
import numpy as np
from jax import lax
from jax.experimental import pallas as pl
from jax.experimental.pallas import tpu as pltpu

F32 = jnp.float32
BF16 = jnp.bfloat16

D_MODEL = 1024
N_META = 16
HEAD_DIM = 64
Q_HEADS = 8
KV_HEADS = 2
WINDOW = 128
ROPE_THETA = 500000.0
ROPE_DIM = HEAD_DIM // 4
RWKV_HEADS = 8
RWKV_DIM = RWKV_HEADS * HEAD_DIM
DECAY_LORA = 64
AAA_LORA = 64
GATE_LORA = 160
RWKV_LN_EPS = 64e-5
D_FF = 2816
Q_W = Q_HEADS * HEAD_DIM
KV_W = KV_HEADS * HEAD_DIM
ATTN_PROJ = Q_W + 2 * KV_W
RWKV_PROJ = 3 * RWKV_DIM + DECAY_LORA + AAA_LORA + GATE_LORA
RMS_EPS = 1e-6
NEG_INF = -1e30

LANES = 128
MXU_COLS = 256
QKV_W = Q_W + 4 * KV_W
GATE_PAD = 256
RW_W = 3 * RWKV_DIM + LANES + GATE_PAD
CHUNK = 64
GROUP = 4 * HEAD_DIM
N_GROUPS = RWKV_DIM // GROUP
ATTN_BLOCKS_PER_STEP = 2
RWKV_ROWS_PER_STEP = 4
VMEM_LIMIT = 56 * 1024 * 1024


def _dot(a, b):
    return jnp.dot(a, b, preferred_element_type=F32)


def _dot_nt(a, b):
    return lax.dot_general(a, b, (((1,), (1,)), ((), ())), preferred_element_type=F32)


def _dot_tn(a, b):
    return lax.dot_general(a, b, (((0,), (0,)), ((), ())), preferred_element_type=F32)


def _sigmoid(x):
    return 1.0 / (1.0 + jnp.exp(-x))


def _params(n_grid_dims):
    return pltpu.CompilerParams(
        dimension_semantics=("arbitrary",) * n_grid_dims,
        vmem_limit_bytes=VMEM_LIMIT)


def _const_spec(shape):
    nd = len(shape)
    return pl.BlockSpec(shape, lambda *_: (0,) * nd, pipeline_mode=pl.Buffered(1))


def _inproj_kernel(x_ref, g_ref, wq_ref, wr_ref, wg_ref, bq_ref, br_ref, bg_ref,
                   cos_ref, s1_ref, s2_ref, pprev_ref, mix_ref,
                   qkv_ref, p_ref, gate_ref, plast_ref, pbuf, *, tm, seq_tiles):
    @pl.when(pl.program_id(0) % seq_tiles == 0)
    def _():
        pbuf[0:8, :] = pprev_ref[...]

    x = x_ref[...]
    ms = jnp.mean(x * x, axis=-1, keepdims=True)
    u = (x * lax.rsqrt(ms + RMS_EPS) * g_ref[...]).astype(BF16)

    cos = cos_ref[...]
    s1 = s1_ref[...]
    s2 = s2_ref[...]
    n_rope = (Q_W + 2 * KV_W) // LANES
    for j in range(QKV_W // MXU_COLS):
        sl = slice(j * MXU_COLS, (j + 1) * MXU_COLS)
        t2 = _dot(u, wq_ref[:, sl]) + bq_ref[:, sl]
        for h in range(MXU_COLS // LANES):
            slab = j * (MXU_COLS // LANES) + h
            t = t2[:, h * LANES:(h + 1) * LANES]
            if slab < n_rope:
                t = t * cos + pltpu.roll(t, 8, 1) * s1 + pltpu.roll(t, LANES - 8, 1) * s2
            qkv_ref[:, slab * LANES:(slab + 1) * LANES] = t.astype(BF16)

    for lo in range(0, RW_W, MXU_COLS):
        sl = slice(lo, min(lo + MXU_COLS, RW_W))
        pc = _dot(u, wr_ref[:, sl]) + br_ref[:, sl]
        pbuf[8:8 + tm, sl] = pc
        psh = pbuf[7:7 + tm, sl]
        p_ref[:, sl] = pc + (psh - pc) * mix_ref[:, sl]
        pbuf[0:8, sl] = pc[tm - 8:tm, :]
        plast_ref[:, sl] = pc[tm - 8:tm, :]

    for lo in range(0, 2 * D_MODEL, MXU_COLS):
        sl = slice(lo, lo + MXU_COLS)
        gate_ref[:, sl] = _sigmoid(_dot(u, wg_ref[:, sl]) + bg_ref[:, sl]).astype(BF16)


def _inproj(x2, g, wq, wr, wg, bq, br, bg, cos, s1, s2, pprev, mix, tm):
    n = x2.shape[0]
    seq_tiles = cos.shape[0] // tm
    row = lambda i: (i, 0)
    pos = lambda i: (i % seq_tiles, 0)
    return pl.pallas_call(
        functools.partial(_inproj_kernel, tm=tm, seq_tiles=seq_tiles),
        grid=(n // tm,),
        in_specs=[
            pl.BlockSpec((tm, D_MODEL), row),
            _const_spec((1, D_MODEL)),
            _const_spec((D_MODEL, QKV_W)),
            _const_spec((D_MODEL, RW_W)),
            _const_spec((D_MODEL, 2 * D_MODEL)),
            _const_spec((1, QKV_W)),
            _const_spec((1, RW_W)),
            _const_spec((1, 2 * D_MODEL)),
            pl.BlockSpec((tm, LANES), pos),
            pl.BlockSpec((tm, LANES), pos),
            pl.BlockSpec((tm, LANES), pos),
            _const_spec((8, RW_W)),
            _const_spec((1, RW_W)),
        ],
        out_specs=[
            pl.BlockSpec((tm, QKV_W), row),
            pl.BlockSpec((tm, RW_W), row),
            pl.BlockSpec((tm, 2 * D_MODEL), row),
            pl.BlockSpec((8, RW_W), lambda i: (0, 0)),
        ],
        out_shape=[
            jax.ShapeDtypeStruct((n, QKV_W), BF16),
            jax.ShapeDtypeStruct((n, RW_W), F32),
            jax.ShapeDtypeStruct((n, 2 * D_MODEL), BF16),
            jax.ShapeDtypeStruct((8, RW_W), F32),
        ],
        scratch_shapes=[pltpu.VMEM((8 + tm, RW_W), F32)],
        compiler_params=_params(1),
        name="inproj",
    )(x2, g, wq, wr, wg, bq, br, bg, cos, s1, s2, pprev, mix)


def _attn_kernel(sink_ref, q_ref, kvc_ref, kvp_ref, kvm_ref, o_ref, *, qb):
    n = pl.program_id(1)
    blk = WINDOW
    lane = lax.broadcasted_iota(jnp.int32, (1, LANES), 1)
    m_lo = (lane < HEAD_DIM).astype(BF16)
    m_hi = (lane >= HEAD_DIM).astype(BF16)

    kv_all = jnp.concatenate([kvp_ref[...], kvc_ref[...]], axis=0)
    kv_meta = kvm_ref[...]

    qi = lax.broadcasted_iota(jnp.int32, (2 * blk, 2 * blk), 0) % blk
    c = lax.broadcasted_iota(jnp.int32, (2 * blk, 2 * blk), 1)
    cur_ok = jnp.where(c - blk <= qi, 1, 0)
    has_prev = jnp.minimum(n, 1)
    ok_first = jnp.where(c < blk, jnp.where(c > qi, has_prev, 0), cur_ok) > 0
    ok_rest = jnp.where(c < blk, jnp.where(c > qi, 1, 0), cur_ok) > 0
    first = lax.broadcasted_iota(jnp.int32, (2 * blk, 2 * N_META), 1) < N_META
    top = lax.broadcasted_iota(jnp.int32, (2 * blk, 1), 0) < blk

    def halves(kv, g):
        k_plain, k_swap = kv[:, 0:LANES], kv[:, LANES:2 * LANES]
        v_plain, v_swap = kv[:, 2 * LANES:3 * LANES], kv[:, 3 * LANES:4 * LANES]
        if g == 0:
            k_lo, k_hi, v_lo, v_hi = k_plain, k_swap, v_plain, v_swap
        else:
            k_lo, k_hi, v_lo, v_hi = k_swap, k_plain, v_swap, v_plain
        return k_lo * m_lo, k_hi * m_hi, v_lo * m_lo, v_hi * m_hi

    band = [halves(kv_all, g) for g in range(KV_HEADS)]
    meta = [tuple(jnp.concatenate(pair, axis=0) for pair in
                  ((h[0], h[1]), (h[2], h[3])))
            for h in (halves(kv_meta, g) for g in range(KV_HEADS))]

    units = [(j, g) for j in range(qb) for g in range(KV_HEADS)]
    st = {}
    for key in units:
        j, g = key
        rows = slice(j * blk, (j + 2) * blk)
        k_lo, k_hi, v_lo, v_hi = band[g]
        kb = jnp.concatenate([k_lo[rows], k_hi[rows]], axis=0)
        vb = jnp.concatenate([v_lo[rows], v_hi[rows]], axis=0)
        q = jnp.concatenate(
            [q_ref[j * blk:(j + 1) * blk, s * LANES:(s + 1) * LANES] for s in (2 * g, 2 * g + 1)],
            axis=0)
        st[key] = dict(vb=vb,
                       sb=_dot_nt(q, kb),
                       sm=_dot_nt(q, meta[g][0]))

    for key in units:
        j, g = key
        u = st[key]
        ok = ok_first if j == 0 else ok_rest
        sink_a = jnp.where(top, sink_ref[4 * g], sink_ref[4 * g + 2])
        sink_b = jnp.where(top, sink_ref[4 * g + 1], sink_ref[4 * g + 3])
        sb, sm = u["sb"], u["sm"]
        sa = jnp.where(ok, sb[:, :2 * blk], NEG_INF)
        sbb = jnp.where(ok, sb[:, 2 * blk:], NEG_INF)
        mx_a = jnp.maximum(
            jnp.maximum(jnp.max(sa, axis=-1, keepdims=True),
                        jnp.max(jnp.where(first, sm, NEG_INF), axis=-1, keepdims=True)),
            sink_a)
        mx_b = jnp.maximum(
            jnp.maximum(jnp.max(sbb, axis=-1, keepdims=True),
                        jnp.max(jnp.where(first, NEG_INF, sm), axis=-1, keepdims=True)),
            sink_b)
        pa = jnp.exp(sa - mx_a)
        pb = jnp.exp(sbb - mx_b)
        pm = jnp.exp(sm - jnp.where(first, mx_a, mx_b))
        den_a = (jnp.sum(pa, axis=-1, keepdims=True)
                 + jnp.sum(jnp.where(first, pm, 0.0), axis=-1, keepdims=True)
                 + jnp.exp(sink_a - mx_a))
        den_b = (jnp.sum(pb, axis=-1, keepdims=True)
                 + jnp.sum(jnp.where(first, 0.0, pm), axis=-1, keepdims=True)
                 + jnp.exp(sink_b - mx_b))
        inv_a = 1.0 / den_a
        inv_b = 1.0 / den_b
        u["p_band"] = jnp.concatenate([pa * inv_a, pb * inv_b], axis=1).astype(BF16)
        u["p_meta"] = (pm * jnp.where(first, inv_a, inv_b)).astype(BF16)

    for key in units:
        j, g = key
        u = st[key]
        o = (_dot(u["p_band"], u["vb"]) + _dot(u["p_meta"], meta[g][1])).astype(BF16)
        for i, s in enumerate((2 * g, 2 * g + 1)):
            o_ref[j * blk:(j + 1) * blk, s * LANES:(s + 1) * LANES] = o[i * blk:(i + 1) * blk]


def _attn(sinks, qkv, qkv_meta, batch, seq, qb):
    nb = seq // WINDOW
    steps = nb // qb
    n = batch * seq
    kvw = 4 * KV_W
    rows = qb * WINDOW
    return pl.pallas_call(
        functools.partial(_attn_kernel, qb=qb),
        grid=(batch, steps),
        in_specs=[
            pl.BlockSpec(memory_space=pltpu.SMEM),
            pl.BlockSpec((rows, Q_W), lambda b, i: (b * steps + i, 0)),
            pl.BlockSpec((rows, kvw), lambda b, i: (b * steps + i, 1)),
            pl.BlockSpec((WINDOW, kvw), lambda b, i: (jnp.maximum((b * steps + i) * qb - 1, 0), 1)),
            pl.BlockSpec((N_META, kvw), lambda b, i: (0, 1)),
        ],
        out_specs=pl.BlockSpec((rows, Q_W), lambda b, i: (b * steps + i, 0)),
        out_shape=jax.ShapeDtypeStruct((n, Q_W), BF16),
        compiler_params=_params(2),
        name="attn",
    )(sinks, qkv, qkv, qkv, qkv_meta)


def _split2(x):
    hi = x.astype(BF16)
    lo = (x - hi.astype(F32)).astype(BF16)
    return hi, lo


def _emit_interleaved(*segment_lists):
    keyed = []
    for li, segs in enumerate(segment_lists):
        for j, seg in enumerate(segs):
            keyed.append(((j + 0.5) / len(segs), li, j, seg))
    for _, _, _, seg in sorted(keyed, key=lambda t: t[:3]):
        seg()


def _rwkv_kernel(*refs, nc, pipelined):
    if pipelined:
        (p0_ref, p1_ref, p2_ref, s0_ref, w0_ref, a0_ref, w2a_ref, g2_ref, kk_ref, ka_ref, rk_ref,
         lnw_ref, lnb_ref, tri_ref, bd_ref, strict_ref, incl_ref, eye_ref,
         y_ref, sfin_ref, s_scr, c_f32, c_pc, c_blk, c_rows, c_chunk) = refs
    else:
        (p0_ref, s0_ref, w0_ref, a0_ref, w2a_ref, g2_ref, kk_ref, ka_ref, rk_ref,
         lnw_ref, lnb_ref, tri_ref, bd_ref, strict_ref, incl_ref, eye_ref,
         y_ref, sfin_ref, s_scr) = refs
    c = pl.program_id(1)
    tok = nc * CHUNK
    o1, o2, o3 = RWKV_DIM, 2 * RWKV_DIM, 3 * RWKV_DIM
    pairs = [(gi, ri) for ri in range(nc) for gi in range(N_GROUPS)]

    bd = bd_ref[...]
    bd_f = bd.astype(F32)
    strict = strict_ref[...]
    incl = incl_ref[...]
    eye = eye_ref[...]
    lane = lax.broadcasted_iota(jnp.int32, (tok, LANES), 1)
    lane1 = lax.broadcasted_iota(jnp.int32, (1, LANES), 1)
    head_lo = (lane1 < HEAD_DIM).astype(BF16)
    head_hi = (lane1 >= HEAD_DIM).astype(BF16)
    zero_slab = jnp.zeros((CHUNK, LANES), BF16)

    def blk(x):
        xb = x.astype(BF16)
        left, right = xb[:, :LANES], xb[:, LANES:]
        return jnp.concatenate(
            [jnp.concatenate([left * head_lo, zero_slab], axis=1),
             jnp.concatenate([left * head_hi, zero_slab], axis=1),
             jnp.concatenate([zero_slab, right * head_lo], axis=1),
             jnp.concatenate([zero_slab, right * head_hi], axis=1)], axis=0)

    def seg_sums(xs):
        pieces = []
        for x in xs:
            pieces.extend(_split2(x))
        out = _dot(jnp.concatenate(pieces, axis=0), bd)
        return [out[2 * i * tok:(2 * i + 1) * tok] + out[(2 * i + 1) * tok:(2 * i + 2) * tok]
                for i in range(len(xs))]

    def grp(x, gi):
        return x[:, gi * GROUP:(gi + 1) * GROUP]

    def val(q, name):
        item = q[name]
        return item() if callable(item) else item

    def front_segments(p_ref, out, to_carry):
        f = {}

        def lora():
            pf = p_ref[...].reshape(tok, RW_W)
            f["r"], f["k"], f["v"] = pf[:, :o1], pf[:, o1:o2], pf[:, o2:o3]
            dwa = pf[:, o3:o3 + LANES]
            f["dg"] = pf[:, o3 + LANES:]
            z = jnp.where(lane < DECAY_LORA, jnp.tanh(dwa), dwa).astype(BF16)
            f["wa"] = _dot(z, w2a_ref[...])

        def decay_gate():
            zw = -(w0_ref[...] + f["wa"][:, :o1])
            softplus = jnp.maximum(zw, 0.0) + jnp.log(1.0 + jnp.exp(-jnp.abs(zw)))
            f["logw"] = -jnp.exp(-softplus - 0.5)
            f["a"] = _sigmoid(a0_ref[...] + f["wa"][:, o1:])
            f["gate"] = _dot(_sigmoid(f["dg"]).astype(BF16), g2_ref[...])

        def norms():
            f["kkv"] = f["k"] * kk_ref[...]
            f["kp"] = f["k"] * (1.0 + (f["a"] - 1.0) * ka_ref[...])
            rkk = f["r"] * f["kp"] * rk_ref[...]
            sq = f["kkv"] * f["kkv"]
            f["sums"] = seg_sums([grp(sq, 0), grp(sq, 1), grp(rkk, 0), grp(rkk, 1)])

        def cumulative():
            ss = jnp.concatenate(f["sums"][0:2], axis=1)
            f["bonus"] = jnp.concatenate(f["sums"][2:4], axis=1)
            kkn = f["kkv"] / jnp.maximum(jnp.sqrt(ss), 1e-12)
            f["aa"] = -kkn
            f["bb"] = kkn * f["a"]
            tri = tri_ref[...]
            f["lps"] = []
            for gi in range(N_GROUPS):
                lw_all = grp(f["logw"], gi)
                hi = lw_all.astype(BF16)
                r1 = lw_all - hi.astype(F32)
                mid = r1.astype(BF16)
                lo = (r1 - mid.astype(F32)).astype(BF16)
                cs = _dot(tri, jnp.concatenate([hi, mid, lo], axis=1))
                f["lps"].append(cs[:, :GROUP] + cs[:, GROUP:2 * GROUP] + cs[:, 2 * GROUP:])
            chunk = dict(gate=f["gate"], bonus=f["bonus"], v=f["v"])
            if to_carry is not None:
                c_chunk[to_carry, 0] = chunk["gate"]
                c_chunk[to_carry, 1] = chunk["bonus"]
                c_chunk[to_carry, 2] = chunk["v"]
            out["chunk"] = chunk
            out["pairs"] = {}

        def pair_prep(i, key):
            def run():
                gi, ri = key
                rows = slice(ri * CHUNK, (ri + 1) * CHUNK)
                cut = lambda x: x[rows, gi * GROUP:(gi + 1) * GROUP]
                lp = f["lps"][gi][rows]
                lw = cut(f["logw"])
                lpc = lp[CHUNK - 1:CHUNK, :]
                e_neg = jnp.exp(-lp)
                e_end = jnp.exp(lpc - lp)
                rg, kg, vg = cut(f["r"]), cut(f["kp"]), cut(f["v"])
                ag, bg = cut(f["aa"]), cut(f["bb"])
                at = ag * jnp.exp(lp - lw)
                rt = rg * jnp.exp(lp)
                q = dict(
                    at=at, vg=vg, pc=jnp.exp(lpc), vblk=blk(vg),
                    bt=blk(bg * e_neg), kt=blk(kg * e_neg),
                    ar=jnp.concatenate([at, rt], axis=0).astype(BF16),
                    bk_end=jnp.concatenate([bg * e_end, kg * e_end], axis=0).astype(BF16))
                if to_carry is not None:
                    c_f32[to_carry, i, 0], c_f32[to_carry, i, 1] = q["at"], q["vg"]
                    c_pc[to_carry, i, 0:1, :] = q["pc"]
                    c_blk[to_carry, i, 0], c_blk[to_carry, i, 1] = q["bt"], q["kt"]
                    c_blk[to_carry, i, 2] = q["vblk"]
                    c_rows[to_carry, i, 0], c_rows[to_carry, i, 1] = q["ar"], q["bk_end"]
                out["pairs"][key] = q
            return run

        return [lora, decay_gate, norms, cumulative] + [pair_prep(i, k) for i, k in enumerate(pairs)]

    def carried(slot):
        def pair(i):
            return dict(
                at=lambda: c_f32[slot, i, 0], vg=lambda: c_f32[slot, i, 1],
                pc=lambda: c_pc[slot, i, 0:1, :],
                bt=lambda: c_blk[slot, i, 0], kt=lambda: c_blk[slot, i, 1],
                vblk=lambda: c_blk[slot, i, 2],
                ar=lambda: c_rows[slot, i, 0], bk_end=lambda: c_rows[slot, i, 1])
        chunk = dict(gate=lambda: c_chunk[slot, 0], bonus=lambda: c_chunk[slot, 1],
                     v=lambda: c_chunk[slot, 2])
        return {key: pair(i) for i, key in enumerate(pairs)}, chunk

    def intra_segments(st):
        def scores():
            for key in pairs:
                q = st[key]
                ar = val(q, "ar")
                sb = _dot_nt(ar, val(q, "bt"))
                sk = _dot_nt(ar, val(q, "kt"))
                q["a_ab"] = sb[:CHUNK] * strict
                q["a_ak"] = (sk[:CHUNK] * strict).astype(BF16)
                q["arr"] = jnp.concatenate([sb[CHUNK:] * incl, sk[CHUNK:] * incl],
                                           axis=1).astype(BF16)

        def square():
            for key in pairs:
                q = st[key]
                q["pw"] = _dot(q["a_ab"].astype(BF16), blk(q["a_ab"]))
                q["minv"] = eye + q["a_ab"]

        def neumann(j):
            def run():
                for key in pairs:
                    q = st[key]
                    if j < 5:
                        out = _dot(q["pw"].astype(BF16),
                                   jnp.concatenate([blk(q["pw"]), blk(q["minv"])], axis=1))
                        q["pw"] = out[:, :GROUP]
                        q["minv"] = q["minv"] + out[:, GROUP:]
                    else:
                        q["minv"] = q["minv"] + _dot(q["pw"].astype(BF16), blk(q["minv"]))
            return run

        def values():
            for key in pairs:
                q = st[key]
                q["av"] = _dot(q["a_ak"], val(q, "vblk"))

        def solve():
            for key in pairs:
                q = st[key]
                wu = _dot(q["minv"].astype(BF16),
                          jnp.concatenate([blk(val(q, "at")), blk(q["av"])], axis=1))
                q["w"] = wu[:, :GROUP].astype(BF16)
                q["u0"] = wu[:, GROUP:]

        return [scores, square] + [neumann(j) for j in range(1, 6)] + [values, solve]

    def state_back_segments(st, chunk, write_y):
        def apply_state():
            for key in pairs:
                gi, ri = key
                q = st[key]
                q["s"] = s_scr[ri, gi]
                q["s_b"] = q["s"].astype(BF16)
                q["u"] = _dot_nt(q["w"], q["s_b"]) + q["u0"]

        def outputs():
            for key in pairs:
                q = st[key]
                rt = val(q, "ar")[CHUNK:]
                q["y"] = (_dot_nt(rt, q["s_b"])
                          + _dot(q["arr"], jnp.concatenate([blk(q["u"]), val(q, "vblk")], axis=0)))

        def update_state():
            for key in pairs:
                gi, ri = key
                q = st[key]
                upd = _dot_tn(jnp.concatenate([q["u"], val(q, "vg")], axis=0).astype(BF16),
                              val(q, "bk_end"))
                s_scr[ri, gi] = q["s"] * val(q, "pc") + upd * bd_f

        g = {}

        def centre():
            g["ys"] = [jnp.concatenate([st[gi, ri]["y"] for ri in range(nc)], axis=0)
                       if nc > 1 else st[gi, 0]["y"] for gi in range(N_GROUPS)]
            means = seg_sums(g["ys"])
            g["ds"] = [g["ys"][i] - means[i] * (1.0 / HEAD_DIM) for i in range(N_GROUPS)]

        def variance():
            g["vars"] = seg_sums([d * d for d in g["ds"]])

        def finish():
            yn = jnp.concatenate(
                [g["ds"][i] * lax.rsqrt(g["vars"][i] * (1.0 / HEAD_DIM) + RWKV_LN_EPS)
                 for i in range(N_GROUPS)], axis=1)
            yn = yn * lnw_ref[...] + lnb_ref[...]
            out = (yn + val(chunk, "bonus") * val(chunk, "v")) * val(chunk, "gate")
            write_y(out.astype(BF16).reshape(nc, CHUNK, RWKV_DIM))

        return [apply_state, outputs, update_state, centre, variance, finish]

    def init_state():
        for ri in range(nc):
            s_scr[ri] = s0_ref[...]

    if pipelined:
        slot = c % 2

        @pl.when(c == 0)
        def _():
            init_state()
            _emit_interleaved(front_segments(p0_ref, {}, 0))

        def write_half(half):
            def write(y):
                y_ref[:, half * CHUNK:(half + 1) * CHUNK, :] = y
            return write

        first, first_chunk = carried(slot)
        second = {}
        _emit_interleaved(intra_segments(first), front_segments(p1_ref, second, None))
        _emit_interleaved(intra_segments(second["pairs"]),
                          state_back_segments(first, first_chunk, write_half(0)))
        _emit_interleaved(state_back_segments(second["pairs"], second["chunk"], write_half(1)),
                          front_segments(p2_ref, {}, 1 - slot))
    else:
        pl.when(c == 0)(init_state)
        only = {}

        def write_all(y):
            y_ref[...] = y

        _emit_interleaved(front_segments(p0_ref, only, None))
        _emit_interleaved(intra_segments(only["pairs"]))
        _emit_interleaved(state_back_segments(only["pairs"], only["chunk"], write_all))

    @pl.when(c == pl.num_programs(1) - 1)
    def _():
        sfin_ref[...] = s_scr[...]


def _rwkv(p, s0, vecs, w2a, g2p, consts, batch, n_chunks, nc, pipelined):
    w0, a0, k_k, k_a, r_k, ln_w, ln_b = vecs
    tri, bd, strict, incl, eye = consts
    tok = nc * CHUNK
    n_pairs = nc * N_GROUPS
    vec = _const_spec((1, RWKV_DIM))
    chunk_spec = lambda index: pl.BlockSpec((nc, CHUNK, RW_W), index)
    if pipelined:
        steps = n_chunks // 2
        p_specs = [chunk_spec(lambda b, c: (b, 0, 0)),
                   chunk_spec(lambda b, c: (b, 2 * c + 1, 0)),
                   chunk_spec(lambda b, c: (b, jnp.minimum(2 * c + 2, n_chunks - 1), 0))]
        p_args = [p, p, p]
        y_rows = 2 * CHUNK
        carry = [pltpu.VMEM((2, n_pairs, 2, CHUNK, GROUP), F32),
                 pltpu.VMEM((2, n_pairs, 8, GROUP), F32),
                 pltpu.VMEM((2, n_pairs, 3, GROUP, GROUP), BF16),
                 pltpu.VMEM((2, n_pairs, 2, 2 * CHUNK, GROUP), BF16),
                 pltpu.VMEM((2, 3, tok, RWKV_DIM), F32)]
    else:
        steps = n_chunks
        p_specs = [chunk_spec(lambda b, c: (b, c, 0))]
        p_args = [p]
        y_rows = CHUNK
        carry = []
    return pl.pallas_call(
        functools.partial(_rwkv_kernel, nc=nc, pipelined=pipelined),
        grid=(batch // nc, steps),
        in_specs=p_specs + [
            _const_spec((N_GROUPS, GROUP, GROUP)),
            vec, vec,
            _const_spec((LANES, 2 * RWKV_DIM)),
            _const_spec((GATE_PAD, RWKV_DIM)),
            vec, vec, vec, vec, vec,
            _const_spec((tok, tok)),
            _const_spec((GROUP, GROUP)),
            _const_spec((CHUNK, GROUP)),
            _const_spec((CHUNK, GROUP)),
            _const_spec((CHUNK, GROUP)),
        ],
        out_specs=[
            pl.BlockSpec((nc, y_rows, RWKV_DIM), lambda b, c: (b, c, 0)),
            pl.BlockSpec((nc, N_GROUPS, GROUP, GROUP), lambda b, c: (b, 0, 0, 0)),
        ],
        out_shape=[
            jax.ShapeDtypeStruct((batch, n_chunks * CHUNK, RWKV_DIM), BF16),
            jax.ShapeDtypeStruct((batch, N_GROUPS, GROUP, GROUP), F32),
        ],
        scratch_shapes=[pltpu.VMEM((nc, N_GROUPS, GROUP, GROUP), F32)] + carry,
        compiler_params=_params(2),
        name="rwkv",
    )(*p_args, s0, w0, a0, w2a, g2p, k_k, k_a, r_k, ln_w, ln_b,
      tri, bd, strict, incl, eye)


def _post_kernel(ya_ref, yr_ref, gate_ref, x_ref, wba_ref, wbr_ref, wo_ref, gf_ref,
                 wg_ref, wu_ref, wd_ref, gn_ref, o_ref):
    gates = gate_ref[...].astype(F32)
    merged = (gates[:, :D_MODEL] * _dot(ya_ref[...], wba_ref[...])
              + gates[:, D_MODEL:] * _dot(yr_ref[...], wbr_ref[...]))
    h = x_ref[...] + _dot(merged.astype(BF16), wo_ref[...])
    ms = jnp.mean(h * h, axis=-1, keepdims=True)
    f = (h * lax.rsqrt(ms + RMS_EPS) * gf_ref[...]).astype(BF16)
    gt = _dot(f, wg_ref[...])
    up = _dot(f, wu_ref[...])
    act = (gt * _sigmoid(gt) * up).astype(BF16)
    h = h + _dot(act, wd_ref[...])
    ms = jnp.mean(h * h, axis=-1, keepdims=True)
    o_ref[...] = h * lax.rsqrt(ms + RMS_EPS) * gn_ref[...]


def _post(ya, yr, gates, x2, wba, wbr, wo, gf, wg, wu, wd, gn, tm):
    n = x2.shape[0]
    row = lambda i: (i, 0)
    return pl.pallas_call(
        _post_kernel,
        grid=(n // tm,),
        in_specs=[
            pl.BlockSpec((tm, Q_W), row),
            pl.BlockSpec((tm, RWKV_DIM), row),
            pl.BlockSpec((tm, 2 * D_MODEL), row),
            pl.BlockSpec((tm, D_MODEL), row),
            _const_spec((Q_W, D_MODEL)),
            _const_spec((RWKV_DIM, D_MODEL)),
            _const_spec((D_MODEL, D_MODEL)),
            _const_spec((1, D_MODEL)),
            _const_spec((D_MODEL, D_FF)),
            _const_spec((D_MODEL, D_FF)),
            _const_spec((D_FF, D_MODEL)),
            _const_spec((1, D_MODEL)),
        ],
        out_specs=pl.BlockSpec((tm, D_MODEL), row),
        out_shape=jax.ShapeDtypeStruct((n, D_MODEL), F32),
        compiler_params=_params(1),
        name="post",
    )(ya, yr, gates, x2, wba, wbr, wo, gf, wg, wu, wd, gn)


def _rope_tables(pos):
    half = ROPE_DIM // 2
    inv_freq = jnp.power(jnp.float32(ROPE_THETA),
                         -jnp.arange(half, dtype=F32) * (2.0 / ROPE_DIM))
    ang = pos.astype(F32)[:, None] * inv_freq[None, :]
    cos, sin = jnp.cos(ang), jnp.sin(ang)
    n = pos.shape[0]
    one = jnp.ones((n, HEAD_DIM - ROPE_DIM), F32)
    zero8 = jnp.zeros((n, half), F32)
    zero48 = jnp.zeros((n, HEAD_DIM - ROPE_DIM), F32)
    c_head = jnp.concatenate([cos, cos, one], axis=1)
    s1_head = jnp.concatenate([zero8, sin, zero48], axis=1)
    s2_head = jnp.concatenate([-sin, zero8, zero48], axis=1)
    dup = lambda t: jnp.concatenate([t, t], axis=1)
    return dup(c_head), dup(s1_head), dup(s2_head)


def _rwkv_constants(nc):
    tt = np.arange(nc * CHUNK)
    tri = ((tt[None, :] <= tt[:, None])
           & (tt[None, :] // CHUNK == tt[:, None] // CHUNK)).astype(np.float32)
    t = np.arange(CHUNK)
    hh = np.arange(GROUP) // HEAD_DIM
    bd = (hh[:, None] == hh[None, :]).astype(np.float32)
    s = np.arange(GROUP) % CHUNK
    strict = (s[None, :] < t[:, None]).astype(np.float32)
    incl = (s[None, :] <= t[:, None]).astype(np.float32)
    eye = (s[None, :] == t[:, None]).astype(np.float32)
    return (jnp.asarray(tri, BF16), jnp.asarray(bd, BF16), jnp.asarray(strict),
            jnp.asarray(incl), jnp.asarray(eye))


def kernel(x, meta_tokens, norm_mix_g, w_in, b_in, attn_sinks, rwkv_mix, rwkv_w0, rwkv_w2, rwkv_a0, rwkv_a2, rwkv_g2, rwkv_k_k, rwkv_k_a, rwkv_r_k, rwkv_ln_w, rwkv_ln_b, w_br_attn, w_br_rwkv, w_o, norm_ffn_g, w_ffn_gate, w_ffn_up, w_ffn_down, norm_final_g):
    batch, seq, _ = x.shape
    layer = 0
    x2 = x.reshape(batch * seq, D_MODEL)

    w = w_in[layer]
    b = b_in[layer]
    scale = HEAD_DIM ** -0.5
    h64 = HEAD_DIM
    kq, kk0, kk1 = Q_W, Q_W + h64, Q_W + 2 * h64
    vq = Q_W + KV_W

    def qkv_cols(t):
        return jnp.concatenate(
            [t[..., :Q_W] * scale,
             t[..., kq:kq + KV_W],
             t[..., kk0:kk1], t[..., kq:kk0],
             t[..., vq:vq + KV_W],
             t[..., vq + h64:vq + 2 * h64], t[..., vq:vq + h64]], axis=-1)

    def rw_cols(t, n_lead):
        pad = jnp.zeros(t.shape[:n_lead] + (GATE_PAD - GATE_LORA,), t.dtype)
        return jnp.concatenate([t, pad], axis=-1)

    wq = qkv_cols(w).astype(BF16)
    bq = qkv_cols(b)[None]
    wr = rw_cols(w[:, ATTN_PROJ:ATTN_PROJ + RWKV_PROJ], 1).astype(BF16)
    br = rw_cols(b[ATTN_PROJ:ATTN_PROJ + RWKV_PROJ], 0)[None]
    wg = w[:, ATTN_PROJ + RWKV_PROJ:].astype(BF16)
    bg = b[ATTN_PROJ + RWKV_PROJ:][None]
    g_mix = norm_mix_g[layer][None]

    mix = rw_cols(rwkv_mix[layer], 0)[None]
    zl = jnp.zeros((DECAY_LORA, RWKV_DIM), F32)
    w2a = jnp.concatenate(
        [jnp.concatenate([rwkv_w2[layer], zl], axis=1),
         jnp.concatenate([zl, rwkv_a2[layer]], axis=1)], axis=0).astype(BF16)
    g2p = jnp.concatenate(
        [rwkv_g2[layer], jnp.zeros((GATE_PAD - GATE_LORA, RWKV_DIM), F32)], axis=0).astype(BF16)
    vecs = (rwkv_w0[layer][None], rwkv_a0[layer][None], rwkv_k_k[layer][None],
            rwkv_k_a[layer][None], rwkv_r_k[layer].reshape(1, RWKV_DIM),
            rwkv_ln_w[layer][None], rwkv_ln_b[layer][None])

    cos_m, s1_m, s2_m = _rope_tables(jnp.arange(N_META, dtype=jnp.int32))
    qkv_m, p_m, _, p_m_tail = _inproj(meta_tokens, g_mix, wq, wr, wg, bq, br, bg,
                                      cos_m, s1_m, s2_m, jnp.zeros((8, RW_W), F32), mix, N_META)
    p_m_pad = jnp.concatenate([jnp.zeros((CHUNK - N_META, RW_W), F32), p_m], axis=0)
    zero_state = jnp.zeros((N_GROUPS, GROUP, GROUP), F32)
    _, s_meta = _rwkv(p_m_pad[None], zero_state, vecs, w2a, g2p, _rwkv_constants(1), 1, 1, 1, False)

    cos, s1, s2 = _rope_tables(N_META + jnp.arange(seq, dtype=jnp.int32))
    tm_in = min(512, seq)
    qkv, p, gates, _ = _inproj(x2, g_mix, wq, wr, wg, bq, br, bg, cos, s1, s2,
                               p_m_tail, mix, tm_in)
    y_attn = _attn(attn_sinks[layer], qkv, qkv_m, batch, seq, ATTN_BLOCKS_PER_STEP)
    nc = min(RWKV_ROWS_PER_STEP, batch)
    y_rwkv, _ = _rwkv(p.reshape(batch, seq, RW_W), s_meta[0], vecs, w2a, g2p,
                      _rwkv_constants(nc), batch, seq // CHUNK, nc, True)
    out = _post(y_attn, y_rwkv.reshape(batch * seq, RWKV_DIM), gates, x2,
                w_br_attn[layer].astype(BF16), w_br_rwkv[layer].astype(BF16),
                w_o[layer].astype(BF16), norm_ffn_g[layer][None],
                w_ffn_gate[layer].astype(BF16), w_ffn_up[layer].astype(BF16),
                w_ffn_down[layer].astype(BF16), norm_final_g[None], min(256, seq))
    return out.reshape(batch, seq, D_MODEL)
```

```python
import functools

import jax
import jax.numpy as jnp
import numpy as np
from jax import lax
from jax.experimental import pallas as pl
from jax.experimental.pallas import tpu as pltpu

F32 = jnp.float32
BF16 = jnp.bfloat16

D_MODEL = 1024
N_META = 16
HEAD_DIM = 64
Q_HEADS = 8
KV_HEADS = 2
WINDOW = 128
ROPE_THETA = 500000.0
ROPE_DIM = HEAD_DIM // 4
RWKV_HEADS = 8
RWKV_DIM = RWKV_HEADS * HEAD_DIM
DECAY_LORA = 64
AAA_LORA = 64
GATE_LORA = 160
RWKV_LN_EPS = 64e-5
D_FF = 2816
Q_W = Q_HEADS * HEAD_DIM
KV_W = KV_HEADS * HEAD_DIM
ATTN_PROJ = Q_W + 2 * KV_W
RWKV_PROJ = 3 * RWKV_DIM + DECAY_LORA + AAA_LORA + GATE_LORA
RMS_EPS = 1e-6
NEG_INF = -1e30

LANES = 128
MXU_COLS = 256
QKV_W = Q_W + 4 * KV_W
GATE_PAD = 256
RW_W = 3 * RWKV_DIM + LANES + GATE_PAD
CHUNK = 64
GROUP = 4 * HEAD_DIM
N_GROUPS = RWKV_DIM // GROUP
ATTN_BLOCKS_PER_STEP = 2
RWKV_ROWS_PER_STEP = 4
VMEM_LIMIT = 56 * 1024 * 1024


def _dot(a, b):
    return jnp.dot(a, b, preferred_element_type=F32)


def _dot_nt(a, b):
    return lax.dot_general(a, b, (((1,), (1,)), ((), ())), preferred_element_type=F32)


def _dot_tn(a, b):
    return lax.dot_general(a, b, (((0,), (0,)), ((), ())), preferred_element_type=F32)


def _sigmoid(x):
    return 1.0 / (1.0 + jnp.exp(-x))


def _params(n_grid_dims):
    return pltpu.CompilerParams(
        dimension_semantics=("arbitrary",) * n_grid_dims,
        vmem_limit_bytes=VMEM_LIMIT)


def _const_spec(shape):
    nd = len(shape)
    return pl.BlockSpec(shape, lambda *_: (0,) * nd, pipeline_mode=pl.Buffered(1))


def _inproj_kernel(x_ref, g_ref, wq_ref, wr_ref, wg_ref, bq_ref, br_ref, bg_ref,
                   cos_ref, s1_ref, s2_ref, pprev_ref, mix_ref,
                   qkv_ref, p_ref, gate_ref, plast_ref, pbuf, *, tm, seq_tiles):
    @pl.when(pl.program_id(0) % seq_tiles == 0)
    def _():
        pbuf[0:8, :] = pprev_ref[...]

    x = x_ref[...]
    ms = jnp.mean(x * x, axis=-1, keepdims=True)
    u = (x * lax.rsqrt(ms + RMS_EPS) * g_ref[...]).astype(BF16)

    cos = cos_ref[...]
    s1 = s1_ref[...]
    s2 = s2_ref[...]
    n_rope = (Q_W + 2 * KV_W) // LANES
    for j in range(QKV_W // MXU_COLS):
        sl = slice(j * MXU_COLS, (j + 1) * MXU_COLS)
        t2 = _dot(u, wq_ref[:, sl]) + bq_ref[:, sl]
        for h in range(MXU_COLS // LANES):
            slab = j * (MXU_COLS // LANES) + h
            t = t2[:, h * LANES:(h + 1) * LANES]
            if slab < n_rope:
                t = t * cos + pltpu.roll(t, 8, 1) * s1 + pltpu.roll(t, LANES - 8, 1) * s2
            qkv_ref[:, slab * LANES:(slab + 1) * LANES] = t.astype(BF16)

    for lo in range(0, RW_W, MXU_COLS):
        sl = slice(lo, min(lo + MXU_COLS, RW_W))
        pc = _dot(u, wr_ref[:, sl]) + br_ref[:, sl]
        pbuf[8:8 + tm, sl] = pc
        psh = pbuf[7:7 + tm, sl]
        p_ref[:, sl] = pc + (psh - pc) * mix_ref[:, sl]
        pbuf[0:8, sl] = pc[tm - 8:tm, :]
        plast_ref[:, sl] = pc[tm - 8:tm, :]

    for lo in range(0, 2 * D_MODEL, MXU_COLS):
        sl = slice(lo, lo + MXU_COLS)
        gate_ref[:, sl] = _sigmoid(_dot(u, wg_ref[:, sl]) + bg_ref[:, sl]).astype(BF16)


def _inproj(x2, g, wq, wr, wg, bq, br, bg, cos, s1, s2, pprev, mix, tm):
    n = x2.shape[0]
    seq_tiles = cos.shape[0] // tm
    row = lambda i: (i, 0)
    pos = lambda i: (i % seq_tiles, 0)
    return pl.pallas_call(
        functools.partial(_inproj_kernel, tm=tm, seq_tiles=seq_tiles),
        grid=(n // tm,),
        in_specs=[
            pl.BlockSpec((tm, D_MODEL), row),
            _const_spec((1, D_MODEL)),
            _const_spec((D_MODEL, QKV_W)),
            _const_spec((D_MODEL, RW_W)),
            _const_spec((D_MODEL, 2 * D_MODEL)),
            _const_spec((1, QKV_W)),
            _const_spec((1, RW_W)),
            _const_spec((1, 2 * D_MODEL)),
            pl.BlockSpec((tm, LANES), pos),
            pl.BlockSpec((tm, LANES), pos),
            pl.BlockSpec((tm, LANES), pos),
            _const_spec((8, RW_W)),
            _const_spec((1, RW_W)),
        ],
        out_specs=[
            pl.BlockSpec((tm, QKV_W), row),
            pl.BlockSpec((tm, RW_W), row),
            pl.BlockSpec((tm, 2 * D_MODEL), row),
            pl.BlockSpec((8, RW_W), lambda i: (0, 0)),
        ],
        out_shape=[
            jax.ShapeDtypeStruct((n, QKV_W), BF16),
            jax.ShapeDtypeStruct((n, RW_W), F32),
            jax.ShapeDtypeStruct((n, 2 * D_MODEL), BF16),
            jax.ShapeDtypeStruct((8, RW_W), F32),
        ],
        scratch_shapes=[pltpu.VMEM((8 + tm, RW_W), F32)],
        compiler_params=_params(1),
        name="inproj",
    )(x2, g, wq, wr, wg, bq, br, bg, cos, s1, s2, pprev, mix)


def _attn_kernel(sink_ref, q_ref, kvc_ref, kvp_ref, kvm_ref, o_ref, *, qb):
    n = pl.program_id(1)
    blk = WINDOW
    lane = lax.broadcasted_iota(jnp.int32, (1, LANES), 1)
    m_lo = (lane < HEAD_DIM).astype(BF16)
    m_hi = (lane >= HEAD_DIM).astype(BF16)

    kv_all = jnp.concatenate([kvp_ref[...], kvc_ref[...]], axis=0)
    kv_meta = kvm_ref[...]

    qi = lax.broadcasted_iota(jnp.int32, (2 * blk, 2 * blk), 0) % blk
    c = lax.broadcasted_iota(jnp.int32, (2 * blk, 2 * blk), 1)
    cur_ok = jnp.where(c - blk <= qi, 1, 0)
    has_prev = jnp.minimum(n, 1)
    ok_first = jnp.where(c < blk, jnp.where(c > qi, has_prev, 0), cur_ok) > 0
    ok_rest = jnp.where(c < blk, jnp.where(c > qi, 1, 0), cur_ok) > 0
    first = lax.broadcasted_iota(jnp.int32, (2 * blk, 2 * N_META), 1) < N_META
    top = lax.broadcasted_iota(jnp.int32, (2 * blk, 1), 0) < blk

    def halves(kv, g):
        k_plain, k_swap = kv[:, 0:LANES], kv[:, LANES:2 * LANES]
        v_plain, v_swap = kv[:, 2 * LANES:3 * LANES], kv[:, 3 * LANES:4 * LANES]
        if g == 0:
            k_lo, k_hi, v_lo, v_hi = k_plain, k_swap, v_plain, v_swap
        else:
            k_lo, k_hi, v_lo, v_hi = k_swap, k_plain, v_swap, v_plain
        return k_lo * m_lo, k_hi * m_hi, v_lo * m_lo, v_hi * m_hi

    band = [halves(kv_all, g) for g in range(KV_HEADS)]
    meta = [tuple(jnp.concatenate(pair, axis=0) for pair in
                  ((h[0], h[1]), (h[2], h[3])))
            for h in (halves(kv_meta, g) for g in range(KV_HEADS))]

    units = [(j, g) for j in range(qb) for g in range(KV_HEADS)]
    st = {}
    for key in units:
        j, g = key
        rows = slice(j * blk, (j + 2) * blk)
        k_lo, k_hi, v_lo, v_hi = band[g]
        kb = jnp.concatenate([k_lo[rows], k_hi[rows]], axis=0)
        vb = jnp.concatenate([v_lo[rows], v_hi[rows]], axis=0)
        q = jnp.concatenate(
            [q_ref[j * blk:(j + 1) * blk, s * LANES:(s + 1) * LANES] for s in (2 * g, 2 * g + 1)],
            axis=0)
        st[key] = dict(vb=vb,
                       sb=_dot_nt(q, kb),
                       sm=_dot_nt(q, meta[g][0]))

    for key in units:
        j, g = key
        u = st[key]
        ok = ok_first if j == 0 else ok_rest
        sink_a = jnp.where(top, sink_ref[4 * g], sink_ref[4 * g + 2])
        sink_b = jnp.where(top, sink_ref[4 * g + 1], sink_ref[4 * g + 3])
        sb, sm = u["sb"], u["sm"]
        sa = jnp.where(ok, sb[:, :2 * blk], NEG_INF)
        sbb = jnp.where(ok, sb[:, 2 * blk:], NEG_INF)
        mx_a = jnp.maximum(
            jnp.maximum(jnp.max(sa, axis=-1, keepdims=True),
                        jnp.max(jnp.where(first, sm, NEG_INF), axis=-1, keepdims=True)),
            sink_a)
        mx_b = jnp.maximum(
            jnp.maximum(jnp.max(sbb, axis=-1, keepdims=True),
                        jnp.max(jnp.where(first, NEG_INF, sm), axis=-1, keepdims=True)),
            sink_b)
        pa = jnp.exp(sa - mx_a)
        pb = jnp.exp(sbb - mx_b)
        pm = jnp.exp(sm - jnp.where(first, mx_a, mx_b))
        den_a = (jnp.sum(pa, axis=-1, keepdims=True)
                 + jnp.sum(jnp.where(first, pm, 0.0), axis=-1, keepdims=True)
                 + jnp.exp(sink_a - mx_a))
        den_b = (jnp.sum(pb, axis=-1, keepdims=True)
                 + jnp.sum(jnp.where(first, 0.0, pm), axis=-1, keepdims=True)
                 + jnp.exp(sink_b - mx_b))
        inv_a = 1.0 / den_a
        inv_b = 1.0 / den_b
        u["p_band"] = jnp.concatenate([pa * inv_a, pb * inv_b], axis=1).astype(BF16)
        u["p_meta"] = (pm * jnp.where(first, inv_a, inv_b)).astype(BF16)

    for key in units:
        j, g = key
        u = st[key]
        o = (_dot(u["p_band"], u["vb"]) + _dot(u["p_meta"], meta[g][1])).astype(BF16)
        for i, s in enumerate((2 * g, 2 * g + 1)):
            o_ref[j * blk:(j + 1) * blk, s * LANES:(s + 1) * LANES] = o[i * blk:(i + 1) * blk]


def _attn(sinks, qkv, qkv_meta, batch, seq, qb):
    nb = seq // WINDOW
    steps = nb // qb
    n = batch * seq
    kvw = 4 * KV_W
    rows = qb * WINDOW
    return pl.pallas_call(
        functools.partial(_attn_kernel, qb=qb),
        grid=(batch, steps),
        in_specs=[
            pl.BlockSpec(memory_space=pltpu.SMEM),
            pl.BlockSpec((rows, Q_W), lambda b, i: (b * steps + i, 0)),
            pl.BlockSpec((rows, kvw), lambda b, i: (b * steps + i, 1)),
            pl.BlockSpec((WINDOW, kvw), lambda b, i: (jnp.maximum((b * steps + i) * qb - 1, 0), 1)),
            pl.BlockSpec((N_META, kvw), lambda b, i: (0, 1)),
        ],
        out_specs=pl.BlockSpec((rows, Q_W), lambda b, i: (b * steps + i, 0)),
        out_shape=jax.ShapeDtypeStruct((n, Q_W), BF16),
        compiler_params=_params(2),
        name="attn",
    )(sinks, qkv, qkv, qkv, qkv_meta)


def _emit_interleaved(*segment_lists):
    keyed = []
    for li, segs in enumerate(segment_lists):
        for j, seg in enumerate(segs):
            keyed.append(((j + 0.5) / len(segs), li, j, seg))
    for _, _, _, seg in sorted(keyed, key=lambda t: t[:3]):
        seg()


def _rwkv_kernel(*refs, nc, pipelined):
    if pipelined:
        (p0_ref, p1_ref, p2_ref, s0_ref, w0_ref, a0_ref, w2a_ref, g2_ref, kk_ref, ka_ref, rk_ref,
         lnw_ref, lnb_ref, bd_ref, tri_ref, eye_ref,
         y_ref, sfin_ref, s_scr, c_f32, c_pc, c_blk, c_rows, c_chunk) = refs
    else:
        (p0_ref, s0_ref, w0_ref, a0_ref, w2a_ref, g2_ref, kk_ref, ka_ref, rk_ref,
         lnw_ref, lnb_ref, bd_ref, tri_ref, eye_ref,
         y_ref, sfin_ref, s_scr) = refs
    c = pl.program_id(1)
    tok = nc * CHUNK
    o1, o2, o3 = RWKV_DIM, 2 * RWKV_DIM, 3 * RWKV_DIM
    pairs = [(gi, ri) for ri in range(nc) for gi in range(N_GROUPS)]

    bd = bd_ref[...]
    bd_f = bd.astype(F32)
    tri_mask = tri_ref[...]
    eye = eye_ref[...]
    lane = lax.broadcasted_iota(jnp.int32, (tok, LANES), 1)
    lane1 = lax.broadcasted_iota(jnp.int32, (1, LANES), 1)
    head_lo = (lane1 < HEAD_DIM).astype(BF16)
    head_hi = (lane1 >= HEAD_DIM).astype(BF16)
    zero_slab = jnp.zeros((CHUNK, LANES), BF16)

    def blk(x):
        xb = x.astype(BF16)
        left, right = xb[:, :LANES], xb[:, LANES:]
        return jnp.concatenate(
            [jnp.concatenate([left * head_lo, zero_slab], axis=1),
             jnp.concatenate([left * head_hi, zero_slab], axis=1),
             jnp.concatenate([zero_slab, right * head_lo], axis=1),
             jnp.concatenate([zero_slab, right * head_hi], axis=1)], axis=0)

    pos_in_chunk = lax.broadcasted_iota(jnp.int32, (tok, 1), 0) % CHUNK

    def seg_sums(xs):
        out = _dot(jnp.concatenate([x.astype(BF16) for x in xs], axis=0), bd)
        return [out[i * tok:(i + 1) * tok] for i in range(len(xs))]

    def grp(x, gi):
        return x[:, gi * GROUP:(gi + 1) * GROUP]

    def val(q, name):
        item = q[name]
        return item() if callable(item) else item

    def front_segments(p_ref, out, to_carry):
        f = {}

        def lora():
            pf = p_ref[...].reshape(tok, RW_W)
            f["r"], f["k"], f["v"] = pf[:, :o1], pf[:, o1:o2], pf[:, o2:o3]
            dwa = pf[:, o3:o3 + LANES]
            f["dg"] = pf[:, o3 + LANES:]
            z = jnp.where(lane < DECAY_LORA, jnp.tanh(dwa), dwa).astype(BF16)
            f["wa"] = _dot(z, w2a_ref[...])

        def decay_gate():
            zw = -(w0_ref[...] + f["wa"][:, :o1])
            softplus = jnp.maximum(zw, 0.0) + jnp.log(1.0 + jnp.exp(-jnp.abs(zw)))
            f["logw"] = -jnp.exp(-softplus - 0.5)
            f["a"] = _sigmoid(a0_ref[...] + f["wa"][:, o1:])
            f["gate"] = _dot(_sigmoid(f["dg"]).astype(BF16), g2_ref[...])

        def norms():
            f["kkv"] = f["k"] * kk_ref[...]
            f["kp"] = f["k"] * (1.0 + (f["a"] - 1.0) * ka_ref[...])
            rkk = f["r"] * f["kp"] * rk_ref[...]
            sq = f["kkv"] * f["kkv"]
            f["sums"] = seg_sums([grp(sq, 0), grp(sq, 1), grp(rkk, 0), grp(rkk, 1)])

        def cumulative():
            ss = jnp.concatenate(f["sums"][0:2], axis=1)
            f["bonus"] = jnp.concatenate(f["sums"][2:4], axis=1)
            kkn = f["kkv"] / jnp.maximum(jnp.sqrt(ss), 1e-12)
            f["aa"] = -kkn
            f["bb"] = kkn * f["a"]
            lp = f["logw"]
            step = 1
            while step < CHUNK:
                lp = lp + jnp.where(pos_in_chunk >= step, pltpu.roll(lp, step, 0), 0.0)
                step *= 2
            f["lps"] = [grp(lp, gi) for gi in range(N_GROUPS)]
            chunk = dict(gate=f["gate"], bonus=f["bonus"], v=f["v"])
            if to_carry is not None:
                c_chunk[to_carry, 0] = chunk["gate"]
                c_chunk[to_carry, 1] = chunk["bonus"]
                c_chunk[to_carry, 2] = chunk["v"]
            out["chunk"] = chunk
            out["pairs"] = {}

        def pair_prep(i, key):
            def run():
                gi, ri = key
                rows = slice(ri * CHUNK, (ri + 1) * CHUNK)
                cut = lambda x: x[rows, gi * GROUP:(gi + 1) * GROUP]
                lp = f["lps"][gi][rows]
                lw = cut(f["logw"])
                lpc = lp[CHUNK - 1:CHUNK, :]
                e_neg = jnp.exp(-lp)
                e_end = jnp.exp(lpc - lp)
                rg, kg, vg = cut(f["r"]), cut(f["kp"]), cut(f["v"])
                ag, bg = cut(f["aa"]), cut(f["bb"])
                at = ag * jnp.exp(lp - lw)
                rt = rg * jnp.exp(lp)
                q = dict(
                    at=at, vg=vg, pc=jnp.exp(lpc), vblk=blk(vg),
                    bt=blk(bg * e_neg), kt=blk(kg * e_neg),
                    ar=jnp.concatenate([at, rt], axis=0).astype(BF16),
                    bk_end=jnp.concatenate([bg * e_end, kg * e_end], axis=0).astype(BF16))
                if to_carry is not None:
                    c_f32[to_carry, i, 0], c_f32[to_carry, i, 1] = q["at"], q["vg"]
                    c_pc[to_carry, i, 0:1, :] = q["pc"]
                    c_blk[to_carry, i, 0], c_blk[to_carry, i, 1] = q["bt"], q["kt"]
                    c_blk[to_carry, i, 2] = q["vblk"]
                    c_rows[to_carry, i, 0], c_rows[to_carry, i, 1] = q["ar"], q["bk_end"]
                out["pairs"][key] = q
            return run

        return [lora, decay_gate, norms, cumulative] + [pair_prep(i, k) for i, k in enumerate(pairs)]

    def carried(slot):
        def pair(i):
            return dict(
                at=lambda: c_f32[slot, i, 0], vg=lambda: c_f32[slot, i, 1],
                pc=lambda: c_pc[slot, i, 0:1, :],
                bt=lambda: c_blk[slot, i, 0], kt=lambda: c_blk[slot, i, 1],
                vblk=lambda: c_blk[slot, i, 2],
                ar=lambda: c_rows[slot, i, 0], bk_end=lambda: c_rows[slot, i, 1])
        chunk = dict(gate=lambda: c_chunk[slot, 0], bonus=lambda: c_chunk[slot, 1],
                     v=lambda: c_chunk[slot, 2])
        return {key: pair(i) for i, key in enumerate(pairs)}, chunk

    def intra_segments(st):
        def scores():
            for key in pairs:
                q = st[key]
                ar = val(q, "ar")
                sb = _dot_nt(ar, val(q, "bt")) * tri_mask
                sk = _dot_nt(ar, val(q, "kt")) * tri_mask
                q["a_ab"] = sb[:CHUNK]
                q["a_rb"] = sb[CHUNK:].astype(BF16)
                q["a_k"] = sk.astype(BF16)

        def square():
            for key in pairs:
                q = st[key]
                q["pw"] = _dot(q["a_ab"].astype(BF16), blk(q["a_ab"]))
                q["minv"] = eye + q["a_ab"]

        def neumann(j):
            def run():
                for key in pairs:
                    q = st[key]
                    if j < 5:
                        out = _dot(q["pw"].astype(BF16),
                                   jnp.concatenate([blk(q["pw"]), blk(q["minv"])], axis=1))
                        q["pw"] = out[:, :GROUP]
                        q["minv"] = q["minv"] + out[:, GROUP:]
                    else:
                        q["minv"] = q["minv"] + _dot(q["pw"].astype(BF16), blk(q["minv"]))
            return run

        def values():
            for key in pairs:
                q = st[key]
                kv = _dot(q["a_k"], val(q, "vblk"))
                q["av"] = kv[:CHUNK]
                q["y_rk"] = kv[CHUNK:]

        def solve():
            for key in pairs:
                q = st[key]
                wu = _dot(q["minv"].astype(BF16),
                          jnp.concatenate([blk(val(q, "at")), blk(q["av"])], axis=1))
                q["w"] = wu[:, :GROUP].astype(BF16)
                q["u0"] = wu[:, GROUP:]

        return [scores, square] + [neumann(j) for j in range(1, 6)] + [values, solve]

    def state_back_segments(st, chunk, write_y):
        def apply_state():
            for key in pairs:
                gi, ri = key
                q = st[key]
                q["s"] = s_scr[ri, gi]
                rt = val(q, "ar")[CHUNK:]
                ws = _dot_nt(jnp.concatenate([q["w"], rt], axis=0), q["s"].astype(BF16))
                q["u"] = ws[:CHUNK] + q["u0"]
                q["y_rs"] = ws[CHUNK:]

        def outputs():
            for key in pairs:
                q = st[key]
                q["y"] = q["y_rs"] + _dot(q["a_rb"], blk(q["u"])) + q["y_rk"]

        def update_state():
            for key in pairs:
                gi, ri = key
                q = st[key]
                upd = _dot_tn(jnp.concatenate([q["u"], val(q, "vg")], axis=0).astype(BF16),
                              val(q, "bk_end"))
                s_scr[ri, gi] = q["s"] * val(q, "pc") + upd * bd_f

        g = {}

        def centre():
            g["ys"] = [jnp.concatenate([st[gi, ri]["y"] for ri in range(nc)], axis=0)
                       if nc > 1 else st[gi, 0]["y"] for gi in range(N_GROUPS)]
            means = seg_sums(g["ys"])
            g["ds"] = [g["ys"][i] - means[i] * (1.0 / HEAD_DIM) for i in range(N_GROUPS)]

        def variance():
            g["vars"] = seg_sums([d * d for d in g["ds"]])

        def finish():
            yn = jnp.concatenate(
                [g["ds"][i] * lax.rsqrt(g["vars"][i] * (1.0 / HEAD_DIM) + RWKV_LN_EPS)
                 for i in range(N_GROUPS)], axis=1)
            yn = yn * lnw_ref[...] + lnb_ref[...]
            out = (yn + val(chunk, "bonus") * val(chunk, "v")) * val(chunk, "gate")
            write_y(out.astype(BF16).reshape(nc, CHUNK, RWKV_DIM))

        return [apply_state, outputs, update_state, centre, variance, finish]

    def init_state():
        for ri in range(nc):
            s_scr[ri] = s0_ref[...]

    if pipelined:
        slot = c % 2

        @pl.when(c == 0)
        def _():
            init_state()
            _emit_interleaved(front_segments(p0_ref, {}, 0))

        def write_half(half):
            def write(y):
                y_ref[:, half * CHUNK:(half + 1) * CHUNK, :] = y
            return write

        first, first_chunk = carried(slot)
        second = {}
        _emit_interleaved(intra_segments(first), front_segments(p1_ref, second, None))
        ahead = front_segments(p2_ref, {}, 1 - slot)
        cut = len(ahead) // 2
        _emit_interleaved(intra_segments(second["pairs"]),
                          state_back_segments(first, first_chunk, write_half(0)), ahead[:cut])
        _emit_interleaved(state_back_segments(second["pairs"], second["chunk"], write_half(1)),
                          ahead[cut:])
    else:
        pl.when(c == 0)(init_state)
        only = {}

        def write_all(y):
            y_ref[...] = y

        _emit_interleaved(front_segments(p0_ref, only, None))
        _emit_interleaved(intra_segments(only["pairs"]))
        _emit_interleaved(state_back_segments(only["pairs"], only["chunk"], write_all))

    @pl.when(c == pl.num_programs(1) - 1)
    def _():
        sfin_ref[...] = s_scr[...]


def _rwkv(p, s0, vecs, w2a, g2p, consts, batch, n_chunks, nc, pipelined):
    w0, a0, k_k, k_a, r_k, ln_w, ln_b = vecs
    bd, tri_mask, eye = consts
    tok = nc * CHUNK
    n_pairs = nc * N_GROUPS
    vec = _const_spec((1, RWKV_DIM))
    chunk_spec = lambda index: pl.BlockSpec((nc, CHUNK, RW_W), index)
    if pipelined:
        steps = n_chunks // 2
        p_specs = [chunk_spec(lambda b, c: (b, 0, 0)),
                   chunk_spec(lambda b, c: (b, 2 * c + 1, 0)),
                   chunk_spec(lambda b, c: (b, jnp.minimum(2 * c + 2, n_chunks - 1), 0))]
        p_args = [p, p, p]
        y_rows = 2 * CHUNK
        carry = [pltpu.VMEM((2, n_pairs, 2, CHUNK, GROUP), F32),
                 pltpu.VMEM((2, n_pairs, 8, GROUP), F32),
                 pltpu.VMEM((2, n_pairs, 3, GROUP, GROUP), BF16),
                 pltpu.VMEM((2, n_pairs, 2, 2 * CHUNK, GROUP), BF16),
                 pltpu.VMEM((2, 3, tok, RWKV_DIM), F32)]
    else:
        steps = n_chunks
        p_specs = [chunk_spec(lambda b, c: (b, c, 0))]
        p_args = [p]
        y_rows = CHUNK
        carry = []
    return pl.pallas_call(
        functools.partial(_rwkv_kernel, nc=nc, pipelined=pipelined),
        grid=(batch // nc, steps),
        in_specs=p_specs + [
            _const_spec((N_GROUPS, GROUP, GROUP)),
            vec, vec,
            _const_spec((LANES, 2 * RWKV_DIM)),
            _const_spec((GATE_PAD, RWKV_DIM)),
            vec, vec, vec, vec, vec,
            _const_spec((GROUP, GROUP)),
            _const_spec((2 * CHUNK, GROUP)),
            _const_spec((CHUNK, GROUP)),
        ],
        out_specs=[
            pl.BlockSpec((nc, y_rows, RWKV_DIM), lambda b, c: (b, c, 0)),
            pl.BlockSpec((nc, N_GROUPS, GROUP, GROUP), lambda b, c: (b, 0, 0, 0)),
        ],
        out_shape=[
            jax.ShapeDtypeStruct((batch, n_chunks * CHUNK, RWKV_DIM), BF16),
            jax.ShapeDtypeStruct((batch, N_GROUPS, GROUP, GROUP), F32),
        ],
        scratch_shapes=[pltpu.VMEM((nc, N_GROUPS, GROUP, GROUP), F32)] + carry,
        compiler_params=_params(2),
        name="rwkv",
    )(*p_args, s0, w0, a0, w2a, g2p, k_k, k_a, r_k, ln_w, ln_b, bd, tri_mask, eye)


def _post_kernel(ya_ref, yr_ref, gate_ref, x_ref, wba_ref, wbr_ref, wo_ref, gf_ref,
                 wg_ref, wu_ref, wd_ref, gn_ref, o_ref):
    gates = gate_ref[...].astype(F32)
    merged = (gates[:, :D_MODEL] * _dot(ya_ref[...], wba_ref[...])
              + gates[:, D_MODEL:] * _dot(yr_ref[...], wbr_ref[...]))
    h = x_ref[...] + _dot(merged.astype(BF16), wo_ref[...])
    ms = jnp.mean(h * h, axis=-1, keepdims=True)
    f = (h * lax.rsqrt(ms + RMS_EPS) * gf_ref[...]).astype(BF16)
    gt = _dot(f, wg_ref[...])
    up = _dot(f, wu_ref[...])
    act = (gt * _sigmoid(gt) * up).astype(BF16)
    h = h + _dot(act, wd_ref[...])
    ms = jnp.mean(h * h, axis=-1, keepdims=True)
    o_ref[...] = h * lax.rsqrt(ms + RMS_EPS) * gn_ref[...]


def _post(ya, yr, gates, x2, wba, wbr, wo, gf, wg, wu, wd, gn, tm):
    n = x2.shape[0]
    row = lambda i: (i, 0)
    return pl.pallas_call(
        _post_kernel,
        grid=(n // tm,),
        in_specs=[
            pl.BlockSpec((tm, Q_W), row),
            pl.BlockSpec((tm, RWKV_DIM), row),
            pl.BlockSpec((tm, 2 * D_MODEL), row),
            pl.BlockSpec((tm, D_MODEL), row),
            _const_spec((Q_W, D_MODEL)),
            _const_spec((RWKV_DIM, D_MODEL)),
            _const_spec((D_MODEL, D_MODEL)),
            _const_spec((1, D_MODEL)),
            _const_spec((D_MODEL, D_FF)),
            _const_spec((D_MODEL, D_FF)),
            _const_spec((D_FF, D_MODEL)),
            _const_spec((1, D_MODEL)),
        ],
        out_specs=pl.BlockSpec((tm, D_MODEL), row),
        out_shape=jax.ShapeDtypeStruct((n, D_MODEL), F32),
        compiler_params=_params(1),
        name="post",
    )(ya, yr, gates, x2, wba, wbr, wo, gf, wg, wu, wd, gn)


def _rope_tables(pos):
    half = ROPE_DIM // 2
    inv_freq = jnp.power(jnp.float32(ROPE_THETA),
                         -jnp.arange(half, dtype=F32) * (2.0 / ROPE_DIM))
    ang = pos.astype(F32)[:, None] * inv_freq[None, :]
    cos, sin = jnp.cos(ang), jnp.sin(ang)
    n = pos.shape[0]
    one = jnp.ones((n, HEAD_DIM - ROPE_DIM), F32)
    zero8 = jnp.zeros((n, half), F32)
    zero48 = jnp.zeros((n, HEAD_DIM - ROPE_DIM), F32)
    c_head = jnp.concatenate([cos, cos, one], axis=1)
    s1_head = jnp.concatenate([zero8, sin, zero48], axis=1)
    s2_head = jnp.concatenate([-sin, zero8, zero48], axis=1)
    dup = lambda t: jnp.concatenate([t, t], axis=1)
    return dup(c_head), dup(s1_head), dup(s2_head)


def _rwkv_constants():
    t = np.arange(CHUNK)
    hh = np.arange(GROUP) // HEAD_DIM
    bd = (hh[:, None] == hh[None, :]).astype(np.float32)
    s = np.arange(GROUP) % CHUNK
    strict = (s[None, :] < t[:, None]).astype(np.float32)
    incl = (s[None, :] <= t[:, None]).astype(np.float32)
    eye = (s[None, :] == t[:, None]).astype(np.float32)
    return (jnp.asarray(bd, BF16), jnp.asarray(np.concatenate([strict, incl], axis=0)),
            jnp.asarray(eye))


def kernel(x, meta_tokens, norm_mix_g, w_in, b_in, attn_sinks, rwkv_mix, rwkv_w0, rwkv_w2, rwkv_a0, rwkv_a2, rwkv_g2, rwkv_k_k, rwkv_k_a, rwkv_r_k, rwkv_ln_w, rwkv_ln_b, w_br_attn, w_br_rwkv, w_o, norm_ffn_g, w_ffn_gate, w_ffn_up, w_ffn_down, norm_final_g):
    batch, seq, _ = x.shape
    layer = 0
    x2 = x.reshape(batch * seq, D_MODEL)

    w = w_in[layer]
    b = b_in[layer]
    scale = HEAD_DIM ** -0.5
    h64 = HEAD_DIM
    kq, kk0, kk1 = Q_W, Q_W + h64, Q_W + 2 * h64
    vq = Q_W + KV_W

    def qkv_cols(t):
        return jnp.concatenate(
            [t[..., :Q_W] * scale,
             t[..., kq:kq + KV_W],
             t[..., kk0:kk1], t[..., kq:kk0],
             t[..., vq:vq + KV_W],
             t[..., vq + h64:vq + 2 * h64], t[..., vq:vq + h64]], axis=-1)

    def rw_cols(t, n_lead):
        pad = jnp.zeros(t.shape[:n_lead] + (GATE_PAD - GATE_LORA,), t.dtype)
        return jnp.concatenate([t, pad], axis=-1)

    wq = qkv_cols(w).astype(BF16)
    bq = qkv_cols(b)[None]
    wr = rw_cols(w[:, ATTN_PROJ:ATTN_PROJ + RWKV_PROJ], 1).astype(BF16)
    br = rw_cols(b[ATTN_PROJ:ATTN_PROJ + RWKV_PROJ], 0)[None]
    wg = w[:, ATTN_PROJ + RWKV_PROJ:].astype(BF16)
    bg = b[ATTN_PROJ + RWKV_PROJ:][None]
    g_mix = norm_mix_g[layer][None]

    mix = rw_cols(rwkv_mix[layer], 0)[None]
    zl = jnp.zeros((DECAY_LORA, RWKV_DIM), F32)
    w2a = jnp.concatenate(
        [jnp.concatenate([rwkv_w2[layer], zl], axis=1),
         jnp.concatenate([zl, rwkv_a2[layer]], axis=1)], axis=0).astype(BF16)
    g2p = jnp.concatenate(
        [rwkv_g2[layer], jnp.zeros((GATE_PAD - GATE_LORA, RWKV_DIM), F32)], axis=0).astype(BF16)
    vecs = (rwkv_w0[layer][None], rwkv_a0[layer][None], rwkv_k_k[layer][None],
            rwkv_k_a[layer][None], rwkv_r_k[layer].reshape(1, RWKV_DIM),
            rwkv_ln_w[layer][None], rwkv_ln_b[layer][None])

    cos_m, s1_m, s2_m = _rope_tables(jnp.arange(N_META, dtype=jnp.int32))
    qkv_m, p_m, _, p_m_tail = _inproj(meta_tokens, g_mix, wq, wr, wg, bq, br, bg,
                                      cos_m, s1_m, s2_m, jnp.zeros((8, RW_W), F32), mix, N_META)
    p_m_pad = jnp.concatenate([jnp.zeros((CHUNK - N_META, RW_W), F32), p_m], axis=0)
    zero_state = jnp.zeros((N_GROUPS, GROUP, GROUP), F32)
    _, s_meta = _rwkv(p_m_pad[None], zero_state, vecs, w2a, g2p, _rwkv_constants(), 1, 1, 1, False)

    cos, s1, s2 = _rope_tables(N_META + jnp.arange(seq, dtype=jnp.int32))
    tm_in = min(512, seq)
    qkv, p, gates, _ = _inproj(x2, g_mix, wq, wr, wg, bq, br, bg, cos, s1, s2,
                               p_m_tail, mix, tm_in)
    y_attn = _attn(attn_sinks[layer], qkv, qkv_m, batch, seq, ATTN_BLOCKS_PER_STEP)
    nc = min(RWKV_ROWS_PER_STEP, batch)
    y_rwkv, _ = _rwkv(p.reshape(batch, seq, RW_W), s_meta[0], vecs, w2a, g2p,
                      _rwkv_constants(), batch, seq // CHUNK, nc, True)
    out = _post(y_attn, y_rwkv.reshape(batch * seq, RWKV_DIM), gates, x2,
                w_br_attn[layer].astype(BF16), w_br_rwkv[layer].astype(BF16),
                w_o[layer].astype(BF16), norm_ffn_g[layer][None],
                w_ffn_gate[layer].astype(BF16), w_ffn_up[layer].astype(BF16),
                w_ffn_down[layer].astype(BF16), norm_final_g[None], min(256, seq))
    return out.reshape(batch, seq, D_MODEL)
```

```python
import functools

import jax
import jax.numpy as jnp
import numpy as np
from jax import lax
from jax.experimental import pallas as pl
from jax.experimental.pallas import tpu as pltpu

F32 = jnp.float32
BF16 = jnp.bfloat16

D_MODEL = 1024
N_META = 16
HEAD_DIM = 64
Q_HEADS = 8
KV_HEADS = 2
WINDOW = 128
ROPE_THETA = 500000.0
ROPE_DIM = HEAD_DIM // 4
RWKV_HEADS = 8
RWKV_DIM = RWKV_HEADS * HEAD_DIM
DECAY_LORA = 64
AAA_LORA = 64
GATE_LORA = 160
RWKV_LN_EPS = 64e-5
D_FF = 2816
Q_W = Q_HEADS * HEAD_DIM
KV_W = KV_HEADS * HEAD_DIM
ATTN_PROJ = Q_W + 2 * KV_W
RWKV_PROJ = 3 * RWKV_DIM + DECAY_LORA + AAA_LORA + GATE_LORA
RMS_EPS = 1e-6
NEG_INF = -1e30

LANES = 128
MXU_COLS = 256
QKV_W = Q_W + 4 * KV_W
GATE_PAD = 256
RW_W = 3 * RWKV_DIM + LANES + GATE_PAD
CHUNK = 64
GROUP = 4 * HEAD_DIM
N_GROUPS = RWKV_DIM // GROUP
ATTN_BLOCKS_PER_STEP = 2
RWKV_ROWS_PER_STEP = 4
POST_ROWS = 512
POST_PARTS = 2
VMEM_LIMIT = 56 * 1024 * 1024


def _dot(a, b):
    return jnp.dot(a, b, preferred_element_type=F32)


def _dot_nt(a, b):
    return lax.dot_general(a, b, (((1,), (1,)), ((), ())), preferred_element_type=F32)


def _dot_tn(a, b):
    return lax.dot_general(a, b, (((0,), (0,)), ((), ())), preferred_element_type=F32)


def _sigmoid(x):
    return 1.0 / (1.0 + jnp.exp(-x))


def _params(n_grid_dims):
    return pltpu.CompilerParams(
        dimension_semantics=("arbitrary",) * n_grid_dims,
        vmem_limit_bytes=VMEM_LIMIT)


def _const_spec(shape):
    nd = len(shape)
    return pl.BlockSpec(shape, lambda *_: (0,) * nd, pipeline_mode=pl.Buffered(1))


def _inproj_kernel(x_ref, g_ref, wq_ref, wr_ref, wg_ref, bq_ref, br_ref, bg_ref,
                   cos_ref, s1_ref, s2_ref, pprev_ref, mix_ref,
                   qkv_ref, p_ref, gate_ref, plast_ref, pbuf, *, tm, seq_tiles):
    @pl.when(pl.program_id(0) % seq_tiles == 0)
    def _():
        pbuf[0:8, :] = pprev_ref[...]

    x = x_ref[...]
    ms = jnp.mean(x * x, axis=-1, keepdims=True)
    u = (x * lax.rsqrt(ms + RMS_EPS) * g_ref[...]).astype(BF16)

    cos = cos_ref[...]
    s1 = s1_ref[...]
    s2 = s2_ref[...]
    n_rope = (Q_W + 2 * KV_W) // LANES
    for j in range(QKV_W // MXU_COLS):
        sl = slice(j * MXU_COLS, (j + 1) * MXU_COLS)
        t2 = _dot(u, wq_ref[:, sl]) + bq_ref[:, sl]
        for h in range(MXU_COLS // LANES):
            slab = j * (MXU_COLS // LANES) + h
            t = t2[:, h * LANES:(h + 1) * LANES]
            if slab < n_rope:
                t = t * cos + pltpu.roll(t, 8, 1) * s1 + pltpu.roll(t, LANES - 8, 1) * s2
            qkv_ref[:, slab * LANES:(slab + 1) * LANES] = t.astype(BF16)

    for lo in range(0, RW_W, MXU_COLS):
        sl = slice(lo, min(lo + MXU_COLS, RW_W))
        pc = _dot(u, wr_ref[:, sl]) + br_ref[:, sl]
        pbuf[8:8 + tm, sl] = pc
        psh = pbuf[7:7 + tm, sl]
        p_ref[:, sl] = pc + (psh - pc) * mix_ref[:, sl]
        pbuf[0:8, sl] = pc[tm - 8:tm, :]
        plast_ref[:, sl] = pc[tm - 8:tm, :]

    for lo in range(0, 2 * D_MODEL, MXU_COLS):
        sl = slice(lo, lo + MXU_COLS)
        gate_ref[:, sl] = _sigmoid(_dot(u, wg_ref[:, sl]) + bg_ref[:, sl]).astype(BF16)


def _inproj(x2, g, wq, wr, wg, bq, br, bg, cos, s1, s2, pprev, mix, tm):
    n = x2.shape[0]
    seq_tiles = cos.shape[0] // tm
    row = lambda i: (i, 0)
    pos = lambda i: (i % seq_tiles, 0)
    return pl.pallas_call(
        functools.partial(_inproj_kernel, tm=tm, seq_tiles=seq_tiles),
        grid=(n // tm,),
        in_specs=[
            pl.BlockSpec((tm, D_MODEL), row),
            _const_spec((1, D_MODEL)),
            _const_spec((D_MODEL, QKV_W)),
            _const_spec((D_MODEL, RW_W)),
            _const_spec((D_MODEL, 2 * D_MODEL)),
            _const_spec((1, QKV_W)),
            _const_spec((1, RW_W)),
            _const_spec((1, 2 * D_MODEL)),
            pl.BlockSpec((tm, LANES), pos),
            pl.BlockSpec((tm, LANES), pos),
            pl.BlockSpec((tm, LANES), pos),
            _const_spec((8, RW_W)),
            _const_spec((1, RW_W)),
        ],
        out_specs=[
            pl.BlockSpec((tm, QKV_W), row),
            pl.BlockSpec((tm, RW_W), row),
            pl.BlockSpec((tm, 2 * D_MODEL), row),
            pl.BlockSpec((8, RW_W), lambda i: (0, 0)),
        ],
        out_shape=[
            jax.ShapeDtypeStruct((n, QKV_W), BF16),
            jax.ShapeDtypeStruct((n, RW_W), F32),
            jax.ShapeDtypeStruct((n, 2 * D_MODEL), BF16),
            jax.ShapeDtypeStruct((8, RW_W), F32),
        ],
        scratch_shapes=[pltpu.VMEM((8 + tm, RW_W), F32)],
        compiler_params=_params(1),
        name="inproj",
    )(x2, g, wq, wr, wg, bq, br, bg, cos, s1, s2, pprev, mix)


def _attn_kernel(sink_ref, q_ref, kvc_ref, kvp_ref, kvm_ref, o_ref, *, qb):
    n = pl.program_id(1)
    blk = WINDOW
    lane = lax.broadcasted_iota(jnp.int32, (1, LANES), 1)
    m_lo = (lane < HEAD_DIM).astype(BF16)
    m_hi = (lane >= HEAD_DIM).astype(BF16)

    kv_all = jnp.concatenate([kvp_ref[...], kvc_ref[...]], axis=0)
    kv_meta = kvm_ref[...]

    qi = lax.broadcasted_iota(jnp.int32, (2 * blk, 2 * blk), 0) % blk
    c = lax.broadcasted_iota(jnp.int32, (2 * blk, 2 * blk), 1)
    cur_ok = jnp.where(c - blk <= qi, 1, 0)
    has_prev = jnp.minimum(n, 1)
    ok_first = jnp.where(c < blk, jnp.where(c > qi, has_prev, 0), cur_ok) > 0
    ok_rest = jnp.where(c < blk, jnp.where(c > qi, 1, 0), cur_ok) > 0
    first = lax.broadcasted_iota(jnp.int32, (2 * blk, 2 * N_META), 1) < N_META
    top = lax.broadcasted_iota(jnp.int32, (2 * blk, 1), 0) < blk

    def halves(kv, g):
        k_plain, k_swap = kv[:, 0:LANES], kv[:, LANES:2 * LANES]
        v_plain, v_swap = kv[:, 2 * LANES:3 * LANES], kv[:, 3 * LANES:4 * LANES]
        if g == 0:
            k_lo, k_hi, v_lo, v_hi = k_plain, k_swap, v_plain, v_swap
        else:
            k_lo, k_hi, v_lo, v_hi = k_swap, k_plain, v_swap, v_plain
        return k_lo * m_lo, k_hi * m_hi, v_lo * m_lo, v_hi * m_hi

    band = [halves(kv_all, g) for g in range(KV_HEADS)]
    meta = [tuple(jnp.concatenate(pair, axis=0) for pair in
                  ((h[0], h[1]), (h[2], h[3])))
            for h in (halves(kv_meta, g) for g in range(KV_HEADS))]

    units = [(j, g) for j in range(qb) for g in range(KV_HEADS)]
    st = {}
    for key in units:
        j, g = key
        rows = slice(j * blk, (j + 2) * blk)
        k_lo, k_hi, v_lo, v_hi = band[g]
        kb = jnp.concatenate([k_lo[rows], k_hi[rows]], axis=0)
        vb = jnp.concatenate([v_lo[rows], v_hi[rows]], axis=0)
        q = jnp.concatenate(
            [q_ref[j * blk:(j + 1) * blk, s * LANES:(s + 1) * LANES] for s in (2 * g, 2 * g + 1)],
            axis=0)
        st[key] = dict(vb=vb,
                       sb=_dot_nt(q, kb),
                       sm=_dot_nt(q, meta[g][0]))

    for key in units:
        j, g = key
        u = st[key]
        ok = ok_first if j == 0 else ok_rest
        sink_a = jnp.where(top, sink_ref[4 * g], sink_ref[4 * g + 2])
        sink_b = jnp.where(top, sink_ref[4 * g + 1], sink_ref[4 * g + 3])
        sb, sm = u["sb"], u["sm"]
        sa = jnp.where(ok, sb[:, :2 * blk], NEG_INF)
        sbb = jnp.where(ok, sb[:, 2 * blk:], NEG_INF)
        mx_a = jnp.maximum(
            jnp.maximum(jnp.max(sa, axis=-1, keepdims=True),
                        jnp.max(jnp.where(first, sm, NEG_INF), axis=-1, keepdims=True)),
            sink_a)
        mx_b = jnp.maximum(
            jnp.maximum(jnp.max(sbb, axis=-1, keepdims=True),
                        jnp.max(jnp.where(first, NEG_INF, sm), axis=-1, keepdims=True)),
            sink_b)
        pa = jnp.exp(sa - mx_a)
        pb = jnp.exp(sbb - mx_b)
        pm = jnp.exp(sm - jnp.where(first, mx_a, mx_b))
        den_a = (jnp.sum(pa, axis=-1, keepdims=True)
                 + jnp.sum(jnp.where(first, pm, 0.0), axis=-1, keepdims=True)
                 + jnp.exp(sink_a - mx_a))
        den_b = (jnp.sum(pb, axis=-1, keepdims=True)
                 + jnp.sum(jnp.where(first, 0.0, pm), axis=-1, keepdims=True)
                 + jnp.exp(sink_b - mx_b))
        inv_a = 1.0 / den_a
        inv_b = 1.0 / den_b
        u["p_band"] = jnp.concatenate([pa * inv_a, pb * inv_b], axis=1).astype(BF16)
        u["p_meta"] = (pm * jnp.where(first, inv_a, inv_b)).astype(BF16)

    for key in units:
        j, g = key
        u = st[key]
        o = (_dot(u["p_band"], u["vb"]) + _dot(u["p_meta"], meta[g][1])).astype(BF16)
        for i, s in enumerate((2 * g, 2 * g + 1)):
            o_ref[j * blk:(j + 1) * blk, s * LANES:(s + 1) * LANES] = o[i * blk:(i + 1) * blk]


def _attn(sinks, qkv, qkv_meta, batch, seq, qb):
    nb = seq // WINDOW
    steps = nb // qb
    n = batch * seq
    kvw = 4 * KV_W
    rows = qb * WINDOW
    return pl.pallas_call(
        functools.partial(_attn_kernel, qb=qb),
        grid=(batch, steps),
        in_specs=[
            pl.BlockSpec(memory_space=pltpu.SMEM),
            pl.BlockSpec((rows, Q_W), lambda b, i: (b * steps + i, 0)),
            pl.BlockSpec((rows, kvw), lambda b, i: (b * steps + i, 1)),
            pl.BlockSpec((WINDOW, kvw), lambda b, i: (jnp.maximum((b * steps + i) * qb - 1, 0), 1)),
            pl.BlockSpec((N_META, kvw), lambda b, i: (0, 1)),
        ],
        out_specs=pl.BlockSpec((rows, Q_W), lambda b, i: (b * steps + i, 0)),
        out_shape=jax.ShapeDtypeStruct((n, Q_W), BF16),
        compiler_params=_params(2),
        name="attn",
    )(sinks, qkv, qkv, qkv, qkv_meta)


def _emit_interleaved(*segment_lists):
    keyed = []
    for li, segs in enumerate(segment_lists):
        for j, seg in enumerate(segs):
            keyed.append(((j + 0.5) / len(segs), li, j, seg))
    for _, _, _, seg in sorted(keyed, key=lambda t: t[:3]):
        seg()


def _rwkv_kernel(*refs, nc, pipelined):
    if pipelined:
        (p0_ref, p1_ref, p2_ref, s0_ref, w0_ref, a0_ref, w2a_ref, g2_ref, kk_ref, ka_ref, rk_ref,
         lnw_ref, lnb_ref, bd_ref, tri_ref, eye_ref,
         y_ref, sfin_ref, s_scr, c_f32, c_pc, c_blk, c_rows, c_chunk) = refs
    else:
        (p0_ref, s0_ref, w0_ref, a0_ref, w2a_ref, g2_ref, kk_ref, ka_ref, rk_ref,
         lnw_ref, lnb_ref, bd_ref, tri_ref, eye_ref,
         y_ref, sfin_ref, s_scr) = refs
    c = pl.program_id(1)
    tok = nc * CHUNK
    o1, o2, o3 = RWKV_DIM, 2 * RWKV_DIM, 3 * RWKV_DIM
    pairs = [(gi, ri) for ri in range(nc) for gi in range(N_GROUPS)]

    bd = bd_ref[...]
    bd_f = bd.astype(F32)
    tri_mask = tri_ref[...]
    eye = eye_ref[...]
    lane = lax.broadcasted_iota(jnp.int32, (tok, LANES), 1)
    lane1 = lax.broadcasted_iota(jnp.int32, (1, LANES), 1)
    head_lo = (lane1 < HEAD_DIM).astype(BF16)
    head_hi = (lane1 >= HEAD_DIM).astype(BF16)
    zero_slab = jnp.zeros((CHUNK, LANES), BF16)

    def blk(x):
        xb = x.astype(BF16)
        left, right = xb[:, :LANES], xb[:, LANES:]
        return jnp.concatenate(
            [jnp.concatenate([left * head_lo, zero_slab], axis=1),
             jnp.concatenate([left * head_hi, zero_slab], axis=1),
             jnp.concatenate([zero_slab, right * head_lo], axis=1),
             jnp.concatenate([zero_slab, right * head_hi], axis=1)], axis=0)

    pos_in_chunk = lax.broadcasted_iota(jnp.int32, (tok, 1), 0) % CHUNK

    def seg_sums(xs):
        out = _dot(jnp.concatenate([x.astype(BF16) for x in xs], axis=0), bd)
        return [out[i * tok:(i + 1) * tok] for i in range(len(xs))]

    def grp(x, gi):
        return x[:, gi * GROUP:(gi + 1) * GROUP]

    def val(q, name):
        item = q[name]
        return item() if callable(item) else item

    def front_segments(p_ref, out, to_carry):
        f = {}

        def lora():
            pf = p_ref[...].reshape(tok, RW_W)
            f["r"], f["k"], f["v"] = pf[:, :o1], pf[:, o1:o2], pf[:, o2:o3]
            dwa = pf[:, o3:o3 + LANES]
            f["dg"] = pf[:, o3 + LANES:]
            z = jnp.where(lane < DECAY_LORA, jnp.tanh(dwa), dwa).astype(BF16)
            f["wa"] = _dot(z, w2a_ref[...])

        def decay_gate():
            zw = -(w0_ref[...] + f["wa"][:, :o1])
            softplus = jnp.maximum(zw, 0.0) + jnp.log(1.0 + jnp.exp(-jnp.abs(zw)))
            f["logw"] = -jnp.exp(-softplus - 0.5)
            f["a"] = _sigmoid(a0_ref[...] + f["wa"][:, o1:])
            f["gate"] = _dot(_sigmoid(f["dg"]).astype(BF16), g2_ref[...])

        def norms():
            f["kkv"] = f["k"] * kk_ref[...]
            f["kp"] = f["k"] * (1.0 + (f["a"] - 1.0) * ka_ref[...])
            rkk = f["r"] * f["kp"] * rk_ref[...]
            sq = f["kkv"] * f["kkv"]
            f["sums"] = seg_sums([grp(sq, 0), grp(sq, 1), grp(rkk, 0), grp(rkk, 1)])

        def cumulative():
            ss = jnp.concatenate(f["sums"][0:2], axis=1)
            f["bonus"] = jnp.concatenate(f["sums"][2:4], axis=1)
            kkn = f["kkv"] / jnp.maximum(jnp.sqrt(ss), 1e-12)
            f["aa"] = -kkn
            f["bb"] = kkn * f["a"]
            lp = f["logw"]
            step = 1
            while step < CHUNK:
                lp = lp + jnp.where(pos_in_chunk >= step, pltpu.roll(lp, step, 0), 0.0)
                step *= 2
            f["lps"] = [grp(lp, gi) for gi in range(N_GROUPS)]
            chunk = dict(gate=f["gate"], bonus=f["bonus"], v=f["v"])
            if to_carry is not None:
                c_chunk[to_carry, 0] = chunk["gate"]
                c_chunk[to_carry, 1] = chunk["bonus"]
                c_chunk[to_carry, 2] = chunk["v"]
            out["chunk"] = chunk
            out["pairs"] = {}

        def pair_prep(i, key):
            def run():
                gi, ri = key
                rows = slice(ri * CHUNK, (ri + 1) * CHUNK)
                cut = lambda x: x[rows, gi * GROUP:(gi + 1) * GROUP]
                lp = f["lps"][gi][rows]
                lw = cut(f["logw"])
                lpc = lp[CHUNK - 1:CHUNK, :]
                e_neg = jnp.exp(-lp)
                e_end = jnp.exp(lpc - lp)
                rg, kg, vg = cut(f["r"]), cut(f["kp"]), cut(f["v"])
                ag, bg = cut(f["aa"]), cut(f["bb"])
                at = ag * jnp.exp(lp - lw)
                rt = rg * jnp.exp(lp)
                q = dict(
                    at=at, vg=vg, pc=jnp.exp(lpc), vblk=blk(vg),
                    bt=blk(bg * e_neg), kt=blk(kg * e_neg),
                    ar=jnp.concatenate([at, rt], axis=0).astype(BF16),
                    bk_end=jnp.concatenate([bg * e_end, kg * e_end], axis=0).astype(BF16))
                if to_carry is not None:
                    c_f32[to_carry, i, 0], c_f32[to_carry, i, 1] = q["at"], q["vg"]
                    c_pc[to_carry, i, 0:1, :] = q["pc"]
                    c_blk[to_carry, i, 0], c_blk[to_carry, i, 1] = q["bt"], q["kt"]
                    c_blk[to_carry, i, 2] = q["vblk"]
                    c_rows[to_carry, i, 0], c_rows[to_carry, i, 1] = q["ar"], q["bk_end"]
                out["pairs"][key] = q
            return run

        return [lora, decay_gate, norms, cumulative] + [pair_prep(i, k) for i, k in enumerate(pairs)]

    def carried(slot):
        def pair(i):
            return dict(
                at=lambda: c_f32[slot, i, 0], vg=lambda: c_f32[slot, i, 1],
                pc=lambda: c_pc[slot, i, 0:1, :],
                bt=lambda: c_blk[slot, i, 0], kt=lambda: c_blk[slot, i, 1],
                vblk=lambda: c_blk[slot, i, 2],
                ar=lambda: c_rows[slot, i, 0], bk_end=lambda: c_rows[slot, i, 1])
        chunk = dict(gate=lambda: c_chunk[slot, 0], bonus=lambda: c_chunk[slot, 1],
                     v=lambda: c_chunk[slot, 2])
        return {key: pair(i) for i, key in enumerate(pairs)}, chunk

    def intra_segments(st):
        def scores():
            for key in pairs:
                q = st[key]
                ar = val(q, "ar")
                sb = _dot_nt(ar, val(q, "bt")) * tri_mask
                sk = _dot_nt(ar, val(q, "kt")) * tri_mask
                q["a_ab"] = sb[:CHUNK]
                q["a_rb"] = sb[CHUNK:].astype(BF16)
                q["a_k"] = sk.astype(BF16)

        def square():
            for key in pairs:
                q = st[key]
                q["pw"] = _dot(q["a_ab"].astype(BF16), blk(q["a_ab"]))
                q["minv"] = eye + q["a_ab"]

        def neumann(j):
            def run():
                for key in pairs:
                    q = st[key]
                    if j < 5:
                        out = _dot(q["pw"].astype(BF16),
                                   jnp.concatenate([blk(q["pw"]), blk(q["minv"])], axis=1))
                        q["pw"] = out[:, :GROUP]
                        q["minv"] = q["minv"] + out[:, GROUP:]
                    else:
                        q["minv"] = q["minv"] + _dot(q["pw"].astype(BF16), blk(q["minv"]))
            return run

        def values():
            for key in pairs:
                q = st[key]
                kv = _dot(q["a_k"], val(q, "vblk"))
                q["av"] = kv[:CHUNK]
                q["y_rk"] = kv[CHUNK:]

        def solve():
            for key in pairs:
                q = st[key]
                wu = _dot(q["minv"].astype(BF16),
                          jnp.concatenate([blk(val(q, "at")), blk(q["av"])], axis=1))
                q["w"] = wu[:, :GROUP].astype(BF16)
                q["u0"] = wu[:, GROUP:]

        return [scores, square] + [neumann(j) for j in range(1, 6)] + [values, solve]

    def state_back_segments(st, chunk, write_y):
        def apply_state():
            for key in pairs:
                gi, ri = key
                q = st[key]
                q["s"] = s_scr[ri, gi]
                rt = val(q, "ar")[CHUNK:]
                ws = _dot_nt(jnp.concatenate([q["w"], rt], axis=0), q["s"].astype(BF16))
                q["u"] = ws[:CHUNK] + q["u0"]
                q["y_rs"] = ws[CHUNK:]

        def outputs():
            for key in pairs:
                q = st[key]
                q["y"] = q["y_rs"] + _dot(q["a_rb"], blk(q["u"])) + q["y_rk"]

        def update_state():
            for key in pairs:
                gi, ri = key
                q = st[key]
                upd = _dot_tn(jnp.concatenate([q["u"], val(q, "vg")], axis=0).astype(BF16),
                              val(q, "bk_end"))
                s_scr[ri, gi] = q["s"] * val(q, "pc") + upd * bd_f

        g = {}

        def centre():
            g["ys"] = [jnp.concatenate([st[gi, ri]["y"] for ri in range(nc)], axis=0)
                       if nc > 1 else st[gi, 0]["y"] for gi in range(N_GROUPS)]
            means = seg_sums(g["ys"])
            g["ds"] = [g["ys"][i] - means[i] * (1.0 / HEAD_DIM) for i in range(N_GROUPS)]

        def variance():
            g["vars"] = seg_sums([d * d for d in g["ds"]])

        def finish():
            yn = jnp.concatenate(
                [g["ds"][i] * lax.rsqrt(g["vars"][i] * (1.0 / HEAD_DIM) + RWKV_LN_EPS)
                 for i in range(N_GROUPS)], axis=1)
            yn = yn * lnw_ref[...] + lnb_ref[...]
            out = (yn + val(chunk, "bonus") * val(chunk, "v")) * val(chunk, "gate")
            write_y(out.astype(BF16).reshape(nc, CHUNK, RWKV_DIM))

        return [apply_state, outputs, update_state, centre, variance, finish]

    def init_state():
        for ri in range(nc):
            s_scr[ri] = s0_ref[...]

    if pipelined:
        slot = c % 2

        @pl.when(c == 0)
        def _():
            init_state()
            _emit_interleaved(front_segments(p0_ref, {}, 0))

        def write_half(half):
            def write(y):
                y_ref[:, half * CHUNK:(half + 1) * CHUNK, :] = y
            return write

        first, first_chunk = carried(slot)
        second = {}
        _emit_interleaved(intra_segments(first), front_segments(p1_ref, second, None))
        ahead = front_segments(p2_ref, {}, 1 - slot)
        cut = len(ahead) // 2
        _emit_interleaved(intra_segments(second["pairs"]),
                          state_back_segments(first, first_chunk, write_half(0)), ahead[:cut])
        _emit_interleaved(state_back_segments(second["pairs"], second["chunk"], write_half(1)),
                          ahead[cut:])
    else:
        pl.when(c == 0)(init_state)
        only = {}

        def write_all(y):
            y_ref[...] = y

        _emit_interleaved(front_segments(p0_ref, only, None))
        _emit_interleaved(intra_segments(only["pairs"]))
        _emit_interleaved(state_back_segments(only["pairs"], only["chunk"], write_all))

    @pl.when(c == pl.num_programs(1) - 1)
    def _():
        sfin_ref[...] = s_scr[...]


def _rwkv(p, s0, vecs, w2a, g2p, consts, batch, n_chunks, nc, pipelined):
    w0, a0, k_k, k_a, r_k, ln_w, ln_b = vecs
    bd, tri_mask, eye = consts
    tok = nc * CHUNK
    n_pairs = nc * N_GROUPS
    vec = _const_spec((1, RWKV_DIM))
    chunk_spec = lambda index: pl.BlockSpec((nc, CHUNK, RW_W), index)
    if pipelined:
        steps = n_chunks // 2
        p_specs = [chunk_spec(lambda b, c: (b, 0, 0)),
                   chunk_spec(lambda b, c: (b, 2 * c + 1, 0)),
                   chunk_spec(lambda b, c: (b, jnp.minimum(2 * c + 2, n_chunks - 1), 0))]
        p_args = [p, p, p]
        y_rows = 2 * CHUNK
        carry = [pltpu.VMEM((2, n_pairs, 2, CHUNK, GROUP), F32),
                 pltpu.VMEM((2, n_pairs, 8, GROUP), F32),
                 pltpu.VMEM((2, n_pairs, 3, GROUP, GROUP), BF16),
                 pltpu.VMEM((2, n_pairs, 2, 2 * CHUNK, GROUP), BF16),
                 pltpu.VMEM((2, 3, tok, RWKV_DIM), F32)]
    else:
        steps = n_chunks
        p_specs = [chunk_spec(lambda b, c: (b, c, 0))]
        p_args = [p]
        y_rows = CHUNK
        carry = []
    return pl.pallas_call(
        functools.partial(_rwkv_kernel, nc=nc, pipelined=pipelined),
        grid=(batch // nc, steps),
        in_specs=p_specs + [
            _const_spec((N_GROUPS, GROUP, GROUP)),
            vec, vec,
            _const_spec((LANES, 2 * RWKV_DIM)),
            _const_spec((GATE_PAD, RWKV_DIM)),
            vec, vec, vec, vec, vec,
            _const_spec((GROUP, GROUP)),
            _const_spec((2 * CHUNK, GROUP)),
            _const_spec((CHUNK, GROUP)),
        ],
        out_specs=[
            pl.BlockSpec((nc, y_rows, RWKV_DIM), lambda b, c: (b, c, 0)),
            pl.BlockSpec((nc, N_GROUPS, GROUP, GROUP), lambda b, c: (b, 0, 0, 0)),
        ],
        out_shape=[
            jax.ShapeDtypeStruct((batch, n_chunks * CHUNK, RWKV_DIM), BF16),
            jax.ShapeDtypeStruct((batch, N_GROUPS, GROUP, GROUP), F32),
        ],
        scratch_shapes=[pltpu.VMEM((nc, N_GROUPS, GROUP, GROUP), F32)] + carry,
        compiler_params=_params(2),
        name="rwkv",
    )(*p_args, s0, w0, a0, w2a, g2p, k_k, k_a, r_k, ln_w, ln_b, bd, tri_mask, eye)


def _post_kernel(ya_ref, yr_ref, gate_ref, x_ref, wba_ref, wbr_ref, wo_ref, gf_ref,
                 wg_ref, wu_ref, wd_ref, gn_ref, o_ref, *, parts):
    pm = o_ref.shape[0] // parts
    rows = [slice(i * pm, (i + 1) * pm) for i in range(parts)]

    def rms(h, g_ref):
        ms = jnp.mean(h * h, axis=-1, keepdims=True)
        return h * lax.rsqrt(ms + RMS_EPS) * g_ref[...]

    merged = []
    for r in rows:
        gates = gate_ref[r, :].astype(F32)
        merged.append((gates[:, :D_MODEL] * _dot(ya_ref[r, :], wba_ref[...])
                       + gates[:, D_MODEL:] * _dot(yr_ref[r, :], wbr_ref[...])).astype(BF16))
    h1 = [x_ref[r, :] + _dot(m, wo_ref[...]) for r, m in zip(rows, merged)]
    acts = []
    for h in h1:
        f = rms(h, gf_ref).astype(BF16)
        gt = _dot(f, wg_ref[...])
        up = _dot(f, wu_ref[...])
        acts.append((gt * _sigmoid(gt) * up).astype(BF16))
    h2 = [h + _dot(a, wd_ref[...]) for h, a in zip(h1, acts)]
    for r, h in zip(rows, h2):
        o_ref[r, :] = rms(h, gn_ref)


def _post(ya, yr, gates, x2, wba, wbr, wo, gf, wg, wu, wd, gn, tm):
    n = x2.shape[0]
    row = lambda i: (i, 0)
    return pl.pallas_call(
        functools.partial(_post_kernel, parts=POST_PARTS),
        grid=(n // tm,),
        in_specs=[
            pl.BlockSpec((tm, Q_W), row),
            pl.BlockSpec((tm, RWKV_DIM), row),
            pl.BlockSpec((tm, 2 * D_MODEL), row),
            pl.BlockSpec((tm, D_MODEL), row),
            _const_spec((Q_W, D_MODEL)),
            _const_spec((RWKV_DIM, D_MODEL)),
            _const_spec((D_MODEL, D_MODEL)),
            _const_spec((1, D_MODEL)),
            _const_spec((D_MODEL, D_FF)),
            _const_spec((D_MODEL, D_FF)),
            _const_spec((D_FF, D_MODEL)),
            _const_spec((1, D_MODEL)),
        ],
        out_specs=pl.BlockSpec((tm, D_MODEL), row),
        out_shape=jax.ShapeDtypeStruct((n, D_MODEL), F32),
        compiler_params=_params(1),
        name="post",
    )(ya, yr, gates, x2, wba, wbr, wo, gf, wg, wu, wd, gn)


def _rope_tables(first_pos, n):
    half = ROPE_DIM // 2
    f32 = np.float32
    inv_freq = np.power(f32(ROPE_THETA), -np.arange(half, dtype=f32) * f32(2.0 / ROPE_DIM))
    pos = (first_pos + np.arange(n)).astype(f32)
    ang = (pos[:, None] * inv_freq[None, :]).astype(f32)
    cos, sin = np.cos(ang).astype(f32), np.sin(ang).astype(f32)
    one = np.ones((n, HEAD_DIM - ROPE_DIM), f32)
    zero8 = np.zeros((n, half), f32)
    zero48 = np.zeros((n, HEAD_DIM - ROPE_DIM), f32)
    c_head = np.concatenate([cos, cos, one], axis=1)
    s1_head = np.concatenate([zero8, sin, zero48], axis=1)
    s2_head = np.concatenate([-sin, zero8, zero48], axis=1)
    dup = lambda t: jnp.asarray(np.concatenate([t, t], axis=1))
    return dup(c_head), dup(s1_head), dup(s2_head)


def _rwkv_constants():
    t = np.arange(CHUNK)
    hh = np.arange(GROUP) // HEAD_DIM
    bd = (hh[:, None] == hh[None, :]).astype(np.float32)
    s = np.arange(GROUP) % CHUNK
    strict = (s[None, :] < t[:, None]).astype(np.float32)
    incl = (s[None, :] <= t[:, None]).astype(np.float32)
    eye = (s[None, :] == t[:, None]).astype(np.float32)
    return (jnp.asarray(bd, BF16), jnp.asarray(np.concatenate([strict, incl], axis=0)),
            jnp.asarray(eye))


def kernel(x, meta_tokens, norm_mix_g, w_in, b_in, attn_sinks, rwkv_mix, rwkv_w0, rwkv_w2, rwkv_a0, rwkv_a2, rwkv_g2, rwkv_k_k, rwkv_k_a, rwkv_r_k, rwkv_ln_w, rwkv_ln_b, w_br_attn, w_br_rwkv, w_o, norm_ffn_g, w_ffn_gate, w_ffn_up, w_ffn_down, norm_final_g):
    batch, seq, _ = x.shape
    layer = 0
    x2 = x.reshape(batch * seq, D_MODEL)

    w = w_in[layer]
    b = b_in[layer]
    scale = HEAD_DIM ** -0.5
    h64 = HEAD_DIM
    kq, kk0, kk1 = Q_W, Q_W + h64, Q_W + 2 * h64
    vq = Q_W + KV_W

    def qkv_cols(t):
        return jnp.concatenate(
            [t[..., :Q_W] * scale,
             t[..., kq:kq + KV_W],
             t[..., kk0:kk1], t[..., kq:kk0],
             t[..., vq:vq + KV_W],
             t[..., vq + h64:vq + 2 * h64], t[..., vq:vq + h64]], axis=-1)

    def rw_cols(t, n_lead):
        pad = jnp.zeros(t.shape[:n_lead] + (GATE_PAD - GATE_LORA,), t.dtype)
        return jnp.concatenate([t, pad], axis=-1)

    wq = qkv_cols(w).astype(BF16)
    bq = qkv_cols(b)[None]
    wr = rw_cols(w[:, ATTN_PROJ:ATTN_PROJ + RWKV_PROJ], 1).astype(BF16)
    br = rw_cols(b[ATTN_PROJ:ATTN_PROJ + RWKV_PROJ], 0)[None]
    wg = w[:, ATTN_PROJ + RWKV_PROJ:].astype(BF16)
    bg = b[ATTN_PROJ + RWKV_PROJ:][None]
    g_mix = norm_mix_g[layer][None]

    mix = rw_cols(rwkv_mix[layer], 0)[None]
    zl = jnp.zeros((DECAY_LORA, RWKV_DIM), F32)
    w2a = jnp.concatenate(
        [jnp.concatenate([rwkv_w2[layer], zl], axis=1),
         jnp.concatenate([zl, rwkv_a2[layer]], axis=1)], axis=0).astype(BF16)
    g2p = jnp.concatenate(
        [rwkv_g2[layer], jnp.zeros((GATE_PAD - GATE_LORA, RWKV_DIM), F32)], axis=0).astype(BF16)
    vecs = (rwkv_w0[layer][None], rwkv_a0[layer][None], rwkv_k_k[layer][None],
            rwkv_k_a[layer][None], rwkv_r_k[layer].reshape(1, RWKV_DIM),
            rwkv_ln_w[layer][None], rwkv_ln_b[layer][None])

    cos_m, s1_m, s2_m = _rope_tables(0, N_META)
    qkv_m, p_m, _, p_m_tail = _inproj(meta_tokens, g_mix, wq, wr, wg, bq, br, bg,
                                      cos_m, s1_m, s2_m, jnp.zeros((8, RW_W), F32), mix, N_META)
    p_m_pad = jnp.concatenate([jnp.zeros((CHUNK - N_META, RW_W), F32), p_m], axis=0)
    zero_state = jnp.zeros((N_GROUPS, GROUP, GROUP), F32)
    _, s_meta = _rwkv(p_m_pad[None], zero_state, vecs, w2a, g2p, _rwkv_constants(), 1, 1, 1, False)

    cos, s1, s2 = _rope_tables(N_META, seq)
    tm_in = min(512, seq)
    qkv, p, gates, _ = _inproj(x2, g_mix, wq, wr, wg, bq, br, bg, cos, s1, s2,
                               p_m_tail, mix, tm_in)
    y_attn = _attn(attn_sinks[layer], qkv, qkv_m, batch, seq, ATTN_BLOCKS_PER_STEP)
    nc = min(RWKV_ROWS_PER_STEP, batch)
    y_rwkv, _ = _rwkv(p.reshape(batch, seq, RW_W), s_meta[0], vecs, w2a, g2p,
                      _rwkv_constants(), batch, seq // CHUNK, nc, True)
    out = _post(y_attn, y_rwkv.reshape(batch * seq, RWKV_DIM), gates, x2,
                w_br_attn[layer].astype(BF16), w_br_rwkv[layer].astype(BF16),
                w_o[layer].astype(BF16), norm_ffn_g[layer][None],
                w_ffn_gate[layer].astype(BF16), w_ffn_up[layer].astype(BF16),
                w_ffn_down[layer].astype(BF16), norm_final_g[None], min(POST_ROWS, seq))
    return out.reshape(batch, seq, D_MODEL)
```

```python
import functools

import jax
import jax.numpy as jnp
import numpy as np
from jax import lax
from jax.experimental import pallas as pl
from jax.experimental.pallas import tpu as pltpu

F32 = jnp.float32
BF16 = jnp.bfloat16

D_MODEL = 1024
N_META = 16
HEAD_DIM = 64
Q_HEADS = 8
KV_HEADS = 2
WINDOW = 128
ROPE_THETA = 500000.0
ROPE_DIM = HEAD_DIM // 4
RWKV_HEADS = 8
RWKV_DIM = RWKV_HEADS * HEAD_DIM
DECAY_LORA = 64
AAA_LORA = 64
GATE_LORA = 160
RWKV_LN_EPS = 64e-5
D_FF = 2816
Q_W = Q_HEADS * HEAD_DIM
KV_W = KV_HEADS * HEAD_DIM
ATTN_PROJ = Q_W + 2 * KV_W
RWKV_PROJ = 3 * RWKV_DIM + DECAY_LORA + AAA_LORA + GATE_LORA
RMS_EPS = 1e-6
NEG_INF = -1e30
LOG2_E = 1.4426950408889634

LANES = 128
MXU_COLS = 256
QKV_W = Q_W + 4 * KV_W
GATE_PAD = 256
RW_W = 3 * RWKV_DIM + LANES + GATE_PAD
CHUNK = 64
GROUP = 4 * HEAD_DIM
N_GROUPS = RWKV_DIM // GROUP
ATTN_BLOCKS_PER_STEP = 2
RWKV_ROWS_PER_STEP = 4
RWKV_CHAINS = 8
POST_ROWS = 512
POST_PARTS = 2
VMEM_LIMIT = 56 * 1024 * 1024


def _dot(a, b):
    return jnp.dot(a, b, preferred_element_type=F32)


def _dot_nt(a, b):
    return lax.dot_general(a, b, (((1,), (1,)), ((), ())), preferred_element_type=F32)


def _dot_tn(a, b):
    return lax.dot_general(a, b, (((0,), (0,)), ((), ())), preferred_element_type=F32)


def _sigmoid(x):
    return 1.0 / (1.0 + jnp.exp(-x))


def _params(n_grid_dims):
    return pltpu.CompilerParams(
        dimension_semantics=("arbitrary",) * n_grid_dims,
        vmem_limit_bytes=VMEM_LIMIT)


def _const_spec(shape):
    nd = len(shape)
    return pl.BlockSpec(shape, lambda *_: (0,) * nd, pipeline_mode=pl.Buffered(1))


def _inproj_kernel(x_ref, g_ref, wq_ref, wr_ref, wg_ref, bq_ref, br_ref, bg_ref,
                   cos_ref, s1_ref, s2_ref, pprev_ref, mix_ref,
                   qkv_ref, p_ref, gate_ref, plast_ref, pbuf, *, tm, seq_tiles):
    @pl.when(pl.program_id(0) % seq_tiles == 0)
    def _():
        pbuf[0:8, :] = pprev_ref[...]

    x = x_ref[...]
    ms = jnp.mean(x * x, axis=-1, keepdims=True)
    u = (x * lax.rsqrt(ms + RMS_EPS) * g_ref[...]).astype(BF16)

    cos = cos_ref[...]
    s1 = s1_ref[...]
    s2 = s2_ref[...]
    n_rope = (Q_W + 2 * KV_W) // LANES
    for j in range(QKV_W // MXU_COLS):
        sl = slice(j * MXU_COLS, (j + 1) * MXU_COLS)
        t2 = _dot(u, wq_ref[:, sl]) + bq_ref[:, sl]
        for h in range(MXU_COLS // LANES):
            slab = j * (MXU_COLS // LANES) + h
            t = t2[:, h * LANES:(h + 1) * LANES]
            if slab < n_rope:
                t = t * cos + pltpu.roll(t, 8, 1) * s1 + pltpu.roll(t, LANES - 8, 1) * s2
            qkv_ref[:, slab * LANES:(slab + 1) * LANES] = t.astype(BF16)

    for lo in range(0, RW_W, MXU_COLS):
        sl = slice(lo, min(lo + MXU_COLS, RW_W))
        pc = _dot(u, wr_ref[:, sl]) + br_ref[:, sl]
        pbuf[8:8 + tm, sl] = pc
        psh = pbuf[7:7 + tm, sl]
        p_ref[:, sl] = pc + (psh - pc) * mix_ref[:, sl]
        pbuf[0:8, sl] = pc[tm - 8:tm, :]
        plast_ref[:, sl] = pc[tm - 8:tm, :]

    for lo in range(0, 2 * D_MODEL, MXU_COLS):
        sl = slice(lo, lo + MXU_COLS)
        gate_ref[:, sl] = _sigmoid(_dot(u, wg_ref[:, sl]) + bg_ref[:, sl]).astype(BF16)


def _inproj(x2, g, wq, wr, wg, bq, br, bg, cos, s1, s2, pprev, mix, tm):
    n = x2.shape[0]
    seq_tiles = cos.shape[0] // tm
    row = lambda i: (i, 0)
    pos = lambda i: (i % seq_tiles, 0)
    return pl.pallas_call(
        functools.partial(_inproj_kernel, tm=tm, seq_tiles=seq_tiles),
        grid=(n // tm,),
        in_specs=[
            pl.BlockSpec((tm, D_MODEL), row),
            _const_spec((1, D_MODEL)),
            _const_spec((D_MODEL, QKV_W)),
            _const_spec((D_MODEL, RW_W)),
            _const_spec((D_MODEL, 2 * D_MODEL)),
            _const_spec((1, QKV_W)),
            _const_spec((1, RW_W)),
            _const_spec((1, 2 * D_MODEL)),
            pl.BlockSpec((tm, LANES), pos),
            pl.BlockSpec((tm, LANES), pos),
            pl.BlockSpec((tm, LANES), pos),
            _const_spec((8, RW_W)),
            _const_spec((1, RW_W)),
        ],
        out_specs=[
            pl.BlockSpec((tm, QKV_W), row),
            pl.BlockSpec((tm, RW_W), row),
            pl.BlockSpec((tm, 2 * D_MODEL), row),
            pl.BlockSpec((8, RW_W), lambda i: (0, 0)),
        ],
        out_shape=[
            jax.ShapeDtypeStruct((n, QKV_W), BF16),
            jax.ShapeDtypeStruct((n, RW_W), F32),
            jax.ShapeDtypeStruct((n, 2 * D_MODEL), BF16),
            jax.ShapeDtypeStruct((8, RW_W), F32),
        ],
        scratch_shapes=[pltpu.VMEM((8 + tm, RW_W), F32)],
        compiler_params=_params(1),
        name="inproj",
    )(x2, g, wq, wr, wg, bq, br, bg, cos, s1, s2, pprev, mix)


def _attn_kernel(sink_ref, q_ref, kvc_ref, kvp_ref, kvm_ref, band_ref, o_ref, *, qb):
    n = pl.program_id(1)
    blk = WINDOW
    lane = lax.broadcasted_iota(jnp.int32, (1, LANES), 1)
    m_lo = (lane < HEAD_DIM).astype(BF16)
    m_hi = (lane >= HEAD_DIM).astype(BF16)
    lo_half = lane < HEAD_DIM

    kv_all = jnp.concatenate([kvp_ref[...], kvc_ref[...]], axis=0)
    kv_meta = kvm_ref[...]

    band_mask = band_ref[...]
    col = lax.broadcasted_iota(jnp.int32, (1, 2 * blk), 1)
    no_prev = (1 - jnp.minimum(n, 1)).astype(F32)
    first_mask = band_mask + jnp.where(col < blk, NEG_INF, 0.0) * no_prev
    first = lax.broadcasted_iota(jnp.int32, (2 * blk, 2 * N_META), 1) < N_META
    top = lax.broadcasted_iota(jnp.int32, (2 * blk, 1), 0) < blk

    def halves(kv, g):
        k_plain, k_swap = kv[:, 0:LANES], kv[:, LANES:2 * LANES]
        v_plain, v_swap = kv[:, 2 * LANES:3 * LANES], kv[:, 3 * LANES:4 * LANES]
        if g == 0:
            k_lo, k_hi, v_lo, v_hi = k_plain, k_swap, v_plain, v_swap
        else:
            k_lo, k_hi, v_lo, v_hi = k_swap, k_plain, v_swap, v_plain
        return k_lo * m_lo, k_hi * m_hi, v_lo * m_lo, v_hi * m_hi

    band = [halves(kv_all, g) for g in range(KV_HEADS)]
    meta = [tuple(jnp.concatenate(pair, axis=0) for pair in
                  ((h[0], h[1]), (h[2], h[3])))
            for h in (halves(kv_meta, g) for g in range(KV_HEADS))]

    units = [(j, g) for j in range(qb) for g in range(KV_HEADS)]
    st = {}
    for key in units:
        j, g = key
        rows = slice(j * blk, (j + 2) * blk)
        k_lo, k_hi, v_lo, v_hi = band[g]
        kb = jnp.concatenate([k_lo[rows], k_hi[rows]], axis=0)
        vb = jnp.concatenate([v_lo[rows], v_hi[rows]], axis=0)
        q = jnp.concatenate(
            [q_ref[j * blk:(j + 1) * blk, s * LANES:(s + 1) * LANES] for s in (2 * g, 2 * g + 1)],
            axis=0)
        st[key] = dict(vb=vb,
                       sb=_dot_nt(q, kb),
                       sm=_dot_nt(q, meta[g][0]))

    for key in units:
        j, g = key
        u = st[key]
        mask = first_mask if j == 0 else band_mask
        sink_a = jnp.where(top, sink_ref[4 * g], sink_ref[4 * g + 2])
        sink_b = jnp.where(top, sink_ref[4 * g + 1], sink_ref[4 * g + 3])
        sb, sm = u["sb"], u["sm"]
        sa = sb[:, :2 * blk] + mask
        sbb = sb[:, 2 * blk:] + mask
        mx_a = jnp.maximum(
            jnp.maximum(jnp.max(sa, axis=-1, keepdims=True),
                        jnp.max(jnp.where(first, sm, NEG_INF), axis=-1, keepdims=True)),
            sink_a)
        mx_b = jnp.maximum(
            jnp.maximum(jnp.max(sbb, axis=-1, keepdims=True),
                        jnp.max(jnp.where(first, NEG_INF, sm), axis=-1, keepdims=True)),
            sink_b)
        pa = jnp.exp2(sa - mx_a)
        pb = jnp.exp2(sbb - mx_b)
        pm = jnp.exp2(sm - jnp.where(first, mx_a, mx_b))
        den_a = (jnp.sum(pa, axis=-1, keepdims=True)
                 + jnp.sum(jnp.where(first, pm, 0.0), axis=-1, keepdims=True)
                 + jnp.exp2(sink_a - mx_a))
        den_b = (jnp.sum(pb, axis=-1, keepdims=True)
                 + jnp.sum(jnp.where(first, 0.0, pm), axis=-1, keepdims=True)
                 + jnp.exp2(sink_b - mx_b))
        u["inv"] = jnp.where(lo_half, 1.0 / den_a, 1.0 / den_b)
        u["p_band"] = jnp.concatenate([pa, pb], axis=1).astype(BF16)
        u["p_meta"] = pm.astype(BF16)

    for key in units:
        j, g = key
        u = st[key]
        o = ((_dot(u["p_band"], u["vb"]) + _dot(u["p_meta"], meta[g][1])) * u["inv"]).astype(BF16)
        for i, s in enumerate((2 * g, 2 * g + 1)):
            o_ref[j * blk:(j + 1) * blk, s * LANES:(s + 1) * LANES] = o[i * blk:(i + 1) * blk]


def _attn(sinks, qkv, qkv_meta, batch, seq, qb):
    nb = seq // WINDOW
    steps = nb // qb
    n = batch * seq
    kvw = 4 * KV_W
    rows = qb * WINDOW
    qi = np.arange(2 * WINDOW)[:, None] % WINDOW
    c = np.arange(2 * WINDOW)[None, :]
    band_ok = np.where(c < WINDOW, c > qi, c - WINDOW <= qi)
    band_mask = jnp.asarray(np.where(band_ok, 0.0, NEG_INF).astype(np.float32))
    return pl.pallas_call(
        functools.partial(_attn_kernel, qb=qb),
        grid=(batch, steps),
        in_specs=[
            pl.BlockSpec(memory_space=pltpu.SMEM),
            pl.BlockSpec((rows, Q_W), lambda b, i: (b * steps + i, 0)),
            pl.BlockSpec((rows, kvw), lambda b, i: (b * steps + i, 1)),
            pl.BlockSpec((WINDOW, kvw), lambda b, i: (jnp.maximum((b * steps + i) * qb - 1, 0), 1)),
            pl.BlockSpec((N_META, kvw), lambda b, i: (0, 1)),
            _const_spec((2 * WINDOW, 2 * WINDOW)),
        ],
        out_specs=pl.BlockSpec((rows, Q_W), lambda b, i: (b * steps + i, 0)),
        out_shape=jax.ShapeDtypeStruct((n, Q_W), BF16),
        compiler_params=_params(2),
        name="attn",
    )(sinks, qkv, qkv, qkv, qkv_meta, band_mask)


def _emit_interleaved(*segment_lists):
    keyed = []
    for li, segs in enumerate(segment_lists):
        for j, seg in enumerate(segs):
            keyed.append(((j + 0.5) / len(segs), li, j, seg))
    for _, _, _, seg in sorted(keyed, key=lambda t: t[:3]):
        seg()


def _rwkv_kernel(*refs, nc, pipelined):
    if pipelined:
        (p0_ref, p1_ref, p2_ref, s0_ref, w0_ref, a0_ref, w2a_ref, g2_ref, kk_ref, ka_ref, rk_ref,
         lnw_ref, lnb_ref, bd_ref, tri_ref, eye_ref,
         y_ref, sfin_ref, s_scr, c_f32, c_pc, c_blk, c_rows, c_chunk) = refs
    else:
        (p0_ref, s0_ref, w0_ref, a0_ref, w2a_ref, g2_ref, kk_ref, ka_ref, rk_ref,
         lnw_ref, lnb_ref, bd_ref, tri_ref, eye_ref,
         y_ref, sfin_ref, s_scr) = refs
    c = pl.program_id(1)
    tok = nc * CHUNK
    o1, o2, o3 = RWKV_DIM, 2 * RWKV_DIM, 3 * RWKV_DIM
    pairs = [(gi, ri) for ri in range(nc) for gi in range(N_GROUPS)]

    bd = bd_ref[...]
    bd_f = bd.astype(F32)
    tri_mask = tri_ref[...]
    eye = eye_ref[...]
    lane = lax.broadcasted_iota(jnp.int32, (tok, LANES), 1)
    lane1 = lax.broadcasted_iota(jnp.int32, (1, LANES), 1)
    head_lo = (lane1 < HEAD_DIM).astype(BF16)
    head_hi = (lane1 >= HEAD_DIM).astype(BF16)
    zero_slab = jnp.zeros((CHUNK, LANES), BF16)

    def blk(x):
        xb = x.astype(BF16)
        left, right = xb[:, :LANES], xb[:, LANES:]
        return jnp.concatenate(
            [jnp.concatenate([left * head_lo, zero_slab], axis=1),
             jnp.concatenate([left * head_hi, zero_slab], axis=1),
             jnp.concatenate([zero_slab, right * head_lo], axis=1),
             jnp.concatenate([zero_slab, right * head_hi], axis=1)], axis=0)

    pos_in_chunk = lax.broadcasted_iota(jnp.int32, (tok, 1), 0) % CHUNK

    def seg_sums(xs):
        out = _dot(jnp.concatenate([x.astype(BF16) for x in xs], axis=0), bd)
        return [out[i * tok:(i + 1) * tok] for i in range(len(xs))]

    def grp(x, gi):
        return x[:, gi * GROUP:(gi + 1) * GROUP]

    def val(q, name):
        item = q[name]
        return item() if callable(item) else item

    def front_segments(p_ref, out, to_carry):
        f = {}

        def lora():
            pf = p_ref[...].reshape(tok, RW_W)
            f["r"], f["k"], f["v"] = pf[:, :o1], pf[:, o1:o2], pf[:, o2:o3]
            dwa = pf[:, o3:o3 + LANES]
            f["dg"] = pf[:, o3 + LANES:]
            z = jnp.where(lane < DECAY_LORA, jnp.tanh(dwa), dwa).astype(BF16)
            f["wa"] = _dot(z, w2a_ref[...])

        def decay_gate():
            zw = -(w0_ref[...] + f["wa"][:, :o1])
            softplus = jnp.maximum(zw, 0.0) + jnp.log(1.0 + jnp.exp(-jnp.abs(zw)))
            f["logw"] = -jnp.exp(-softplus - 0.5)
            f["a"] = _sigmoid(a0_ref[...] + f["wa"][:, o1:])
            f["gate"] = _dot(_sigmoid(f["dg"]).astype(BF16), g2_ref[...])

        def norms():
            f["kkv"] = f["k"] * kk_ref[...]
            f["kp"] = f["k"] * (1.0 + (f["a"] - 1.0) * ka_ref[...])
            rkk = f["r"] * f["kp"] * rk_ref[...]
            sq = f["kkv"] * f["kkv"]
            f["sums"] = seg_sums([grp(sq, 0), grp(sq, 1), grp(rkk, 0), grp(rkk, 1)])

        def cumulative():
            ss = jnp.concatenate(f["sums"][0:2], axis=1)
            f["bonus"] = jnp.concatenate(f["sums"][2:4], axis=1)
            kkn = f["kkv"] / jnp.maximum(jnp.sqrt(ss), 1e-12)
            f["aa"] = -kkn
            f["bb"] = kkn * f["a"]
            lp = f["logw"]
            step = 1
            while step < CHUNK:
                lp = lp + jnp.where(pos_in_chunk >= step, pltpu.roll(lp, step, 0), 0.0)
                step *= 2
            f["lps"] = [grp(lp, gi) for gi in range(N_GROUPS)]
            chunk = dict(gate=f["gate"], bonus=f["bonus"], v=f["v"])
            if to_carry is not None:
                c_chunk[to_carry, 0] = chunk["gate"]
                c_chunk[to_carry, 1] = chunk["bonus"]
                c_chunk[to_carry, 2] = chunk["v"]
            out["chunk"] = chunk
            out["pairs"] = {}

        def pair_prep(i, key):
            def run():
                gi, ri = key
                rows = slice(ri * CHUNK, (ri + 1) * CHUNK)
                cut = lambda x: x[rows, gi * GROUP:(gi + 1) * GROUP]
                lp = f["lps"][gi][rows]
                lw = cut(f["logw"])
                lpc = lp[CHUNK - 1:CHUNK, :]
                e_neg = jnp.exp(-lp)
                e_end = jnp.exp(lpc - lp)
                rg, kg, vg = cut(f["r"]), cut(f["kp"]), cut(f["v"])
                ag, bg = cut(f["aa"]), cut(f["bb"])
                at = ag * jnp.exp(lp - lw)
                rt = rg * jnp.exp(lp)
                q = dict(
                    at=at, vg=vg, pc=jnp.exp(lpc), vblk=blk(vg),
                    bt=blk(bg * e_neg), kt=blk(kg * e_neg),
                    ar=jnp.concatenate([at, rt], axis=0).astype(BF16),
                    bk_end=jnp.concatenate([bg * e_end, kg * e_end], axis=0).astype(BF16))
                if to_carry is not None:
                    c_f32[to_carry, i, 0], c_f32[to_carry, i, 1] = q["at"], q["vg"]
                    c_pc[to_carry, i, 0:1, :] = q["pc"]
                    c_blk[to_carry, i, 0], c_blk[to_carry, i, 1] = q["bt"], q["kt"]
                    c_blk[to_carry, i, 2] = q["vblk"]
                    c_rows[to_carry, i, 0], c_rows[to_carry, i, 1] = q["ar"], q["bk_end"]
                out["pairs"][key] = q
            return run

        return [lora, decay_gate, norms, cumulative] + [pair_prep(i, k) for i, k in enumerate(pairs)]

    def carried(slot):
        def pair(i):
            return dict(
                at=lambda: c_f32[slot, i, 0], vg=lambda: c_f32[slot, i, 1],
                pc=lambda: c_pc[slot, i, 0:1, :],
                bt=lambda: c_blk[slot, i, 0], kt=lambda: c_blk[slot, i, 1],
                vblk=lambda: c_blk[slot, i, 2],
                ar=lambda: c_rows[slot, i, 0], bk_end=lambda: c_rows[slot, i, 1])
        chunk = dict(gate=lambda: c_chunk[slot, 0], bonus=lambda: c_chunk[slot, 1],
                     v=lambda: c_chunk[slot, 2])
        return {key: pair(i) for i, key in enumerate(pairs)}, chunk

    def intra_segments(st):
        def stages(batch):
            def scores():
                for key in batch:
                    q = st[key]
                    ar = val(q, "ar")
                    sb = _dot_nt(ar, val(q, "bt")) * tri_mask
                    sk = _dot_nt(ar, val(q, "kt")) * tri_mask
                    q["a_ab"] = sb[:CHUNK]
                    q["a_rb"] = sb[CHUNK:].astype(BF16)
                    q["a_k"] = sk.astype(BF16)

            def square():
                for key in batch:
                    q = st[key]
                    q["pw"] = _dot(q["a_ab"].astype(BF16), blk(q["a_ab"]))
                    q["minv"] = eye + q["a_ab"]

            def neumann(j):
                def run():
                    for key in batch:
                        q = st[key]
                        if j < 5:
                            out = _dot(q["pw"].astype(BF16),
                                       jnp.concatenate([blk(q["pw"]), blk(q["minv"])], axis=1))
                            q["pw"] = out[:, :GROUP]
                            q["minv"] = q["minv"] + out[:, GROUP:]
                        else:
                            q["minv"] = q["minv"] + _dot(q["pw"].astype(BF16), blk(q["minv"]))
                return run

            def values():
                for key in batch:
                    q = st[key]
                    kv = _dot(q["a_k"], val(q, "vblk"))
                    q["av"] = kv[:CHUNK]
                    q["y_rk"] = kv[CHUNK:]

            def solve():
                for key in batch:
                    q = st[key]
                    wu = _dot(q["minv"].astype(BF16),
                              jnp.concatenate([blk(val(q, "at")), blk(q["av"])], axis=1))
                    q["w"] = wu[:, :GROUP].astype(BF16)
                    q["u0"] = wu[:, GROUP:]

            return [scores, square] + [neumann(j) for j in range(1, 6)] + [values, solve]

        segs = []
        for lo in range(0, len(pairs), RWKV_CHAINS):
            segs.extend(stages(pairs[lo:lo + RWKV_CHAINS]))
        return segs

    def state_back_segments(st, chunk, write_y):
        def apply_state():
            for key in pairs:
                gi, ri = key
                q = st[key]
                q["s"] = s_scr[ri, gi]
                rt = val(q, "ar")[CHUNK:]
                ws = _dot_nt(jnp.concatenate([q["w"], rt], axis=0), q["s"].astype(BF16))
                q["u"] = ws[:CHUNK] + q["u0"]
                q["y_rs"] = ws[CHUNK:]

        def outputs():
            for key in pairs:
                q = st[key]
                q["y"] = q["y_rs"] + _dot(q["a_rb"], blk(q["u"])) + q["y_rk"]

        def update_state():
            for key in pairs:
                gi, ri = key
                q = st[key]
                upd = _dot_tn(jnp.concatenate([q["u"], val(q, "vg")], axis=0).astype(BF16),
                              val(q, "bk_end"))
                s_scr[ri, gi] = q["s"] * val(q, "pc") + upd * bd_f

        g = {}

        def centre():
            g["ys"] = [jnp.concatenate([st[gi, ri]["y"] for ri in range(nc)], axis=0)
                       if nc > 1 else st[gi, 0]["y"] for gi in range(N_GROUPS)]
            means = seg_sums(g["ys"])
            g["ds"] = [g["ys"][i] - means[i] * (1.0 / HEAD_DIM) for i in range(N_GROUPS)]

        def variance():
            g["vars"] = seg_sums([d * d for d in g["ds"]])

        def finish():
            yn = jnp.concatenate(
                [g["ds"][i] * lax.rsqrt(g["vars"][i] * (1.0 / HEAD_DIM) + RWKV_LN_EPS)
                 for i in range(N_GROUPS)], axis=1)
            yn = yn * lnw_ref[...] + lnb_ref[...]
            out = (yn + val(chunk, "bonus") * val(chunk, "v")) * val(chunk, "gate")
            write_y(out.astype(BF16).reshape(nc, CHUNK, RWKV_DIM))

        return [apply_state, outputs, update_state, centre, variance, finish]

    def init_state():
        for ri in range(nc):
            s_scr[ri] = s0_ref[...]

    if pipelined:
        slot = c % 2

        @pl.when(c == 0)
        def _():
            init_state()
            _emit_interleaved(front_segments(p0_ref, {}, 0))

        def write_half(half):
            def write(y):
                y_ref[:, half * CHUNK:(half + 1) * CHUNK, :] = y
            return write

        first, first_chunk = carried(slot)
        second = {}
        _emit_interleaved(intra_segments(first), front_segments(p1_ref, second, None))
        ahead = front_segments(p2_ref, {}, 1 - slot)
        cut = len(ahead) // 2
        _emit_interleaved(intra_segments(second["pairs"]),
                          state_back_segments(first, first_chunk, write_half(0)), ahead[:cut])
        _emit_interleaved(state_back_segments(second["pairs"], second["chunk"], write_half(1)),
                          ahead[cut:])
    else:
        pl.when(c == 0)(init_state)
        only = {}

        def write_all(y):
            y_ref[...] = y

        _emit_interleaved(front_segments(p0_ref, only, None))
        _emit_interleaved(intra_segments(only["pairs"]))
        _emit_interleaved(state_back_segments(only["pairs"], only["chunk"], write_all))

    @pl.when(c == pl.num_programs(1) - 1)
    def _():
        sfin_ref[...] = s_scr[...]


def _rwkv(p, s0, vecs, w2a, g2p, consts, batch, n_chunks, nc, pipelined):
    w0, a0, k_k, k_a, r_k, ln_w, ln_b = vecs
    bd, tri_mask, eye = consts
    tok = nc * CHUNK
    n_pairs = nc * N_GROUPS
    vec = _const_spec((1, RWKV_DIM))
    chunk_spec = lambda index: pl.BlockSpec((nc, CHUNK, RW_W), index)
    if pipelined:
        steps = n_chunks // 2
        p_specs = [chunk_spec(lambda b, c: (b, 0, 0)),
                   chunk_spec(lambda b, c: (b, 2 * c + 1, 0)),
                   chunk_spec(lambda b, c: (b, jnp.minimum(2 * c + 2, n_chunks - 1), 0))]
        p_args = [p, p, p]
        y_rows = 2 * CHUNK
        carry = [pltpu.VMEM((2, n_pairs, 2, CHUNK, GROUP), F32),
                 pltpu.VMEM((2, n_pairs, 8, GROUP), F32),
                 pltpu.VMEM((2, n_pairs, 3, GROUP, GROUP), BF16),
                 pltpu.VMEM((2, n_pairs, 2, 2 * CHUNK, GROUP), BF16),
                 pltpu.VMEM((2, 3, tok, RWKV_DIM), F32)]
    else:
        steps = n_chunks
        p_specs = [chunk_spec(lambda b, c: (b, c, 0))]
        p_args = [p]
        y_rows = CHUNK
        carry = []
    return pl.pallas_call(
        functools.partial(_rwkv_kernel, nc=nc, pipelined=pipelined),
        grid=(batch // nc, steps),
        in_specs=p_specs + [
            _const_spec((N_GROUPS, GROUP, GROUP)),
            vec, vec,
            _const_spec((LANES, 2 * RWKV_DIM)),
            _const_spec((GATE_PAD, RWKV_DIM)),
            vec, vec, vec, vec, vec,
            _const_spec((GROUP, GROUP)),
            _const_spec((2 * CHUNK, GROUP)),
            _const_spec((CHUNK, GROUP)),
        ],
        out_specs=[
            pl.BlockSpec((nc, y_rows, RWKV_DIM), lambda b, c: (b, c, 0)),
            pl.BlockSpec((nc, N_GROUPS, GROUP, GROUP), lambda b, c: (b, 0, 0, 0)),
        ],
        out_shape=[
            jax.ShapeDtypeStruct((batch, n_chunks * CHUNK, RWKV_DIM), BF16),
            jax.ShapeDtypeStruct((batch, N_GROUPS, GROUP, GROUP), F32),
        ],
        scratch_shapes=[pltpu.VMEM((nc, N_GROUPS, GROUP, GROUP), F32)] + carry,
        compiler_params=_params(2),
        name="rwkv",
    )(*p_args, s0, w0, a0, w2a, g2p, k_k, k_a, r_k, ln_w, ln_b, bd, tri_mask, eye)


def _post_kernel(ya_ref, yr_ref, gate_ref, x_ref, wba_ref, wbr_ref, wo_ref, gf_ref,
                 wg_ref, wu_ref, wd_ref, gn_ref, o_ref, *, parts):
    pm = o_ref.shape[0] // parts
    rows = [slice(i * pm, (i + 1) * pm) for i in range(parts)]

    def rms(h, g_ref):
        ms = jnp.mean(h * h, axis=-1, keepdims=True)
        return h * lax.rsqrt(ms + RMS_EPS) * g_ref[...]

    merged = []
    for r in rows:
        gates = gate_ref[r, :].astype(F32)
        merged.append((gates[:, :D_MODEL] * _dot(ya_ref[r, :], wba_ref[...])
                       + gates[:, D_MODEL:] * _dot(yr_ref[r, :], wbr_ref[...])).astype(BF16))
    h1 = [x_ref[r, :] + _dot(m, wo_ref[...]) for r, m in zip(rows, merged)]
    acts = []
    for h in h1:
        f = rms(h, gf_ref).astype(BF16)
        gt = _dot(f, wg_ref[...])
        up = _dot(f, wu_ref[...])
        acts.append((gt * _sigmoid(gt) * up).astype(BF16))
    h2 = [h + _dot(a, wd_ref[...]) for h, a in zip(h1, acts)]
    for r, h in zip(rows, h2):
        o_ref[r, :] = rms(h, gn_ref)


def _post(ya, yr, gates, x2, wba, wbr, wo, gf, wg, wu, wd, gn, tm):
    n = x2.shape[0]
    row = lambda i: (i, 0)
    return pl.pallas_call(
        functools.partial(_post_kernel, parts=POST_PARTS),
        grid=(n // tm,),
        in_specs=[
            pl.BlockSpec((tm, Q_W), row),
            pl.BlockSpec((tm, RWKV_DIM), row),
            pl.BlockSpec((tm, 2 * D_MODEL), row),
            pl.BlockSpec((tm, D_MODEL), row),
            _const_spec((Q_W, D_MODEL)),
            _const_spec((RWKV_DIM, D_MODEL)),
            _const_spec((D_MODEL, D_MODEL)),
            _const_spec((1, D_MODEL)),
            _const_spec((D_MODEL, D_FF)),
            _const_spec((D_MODEL, D_FF)),
            _const_spec((D_FF, D_MODEL)),
            _const_spec((1, D_MODEL)),
        ],
        out_specs=pl.BlockSpec((tm, D_MODEL), row),
        out_shape=jax.ShapeDtypeStruct((n, D_MODEL), F32),
        compiler_params=_params(1),
        name="post",
    )(ya, yr, gates, x2, wba, wbr, wo, gf, wg, wu, wd, gn)


def _rope_tables(first_pos, n):
    half = ROPE_DIM // 2
    f32 = np.float32
    inv_freq = np.power(f32(ROPE_THETA), -np.arange(half, dtype=f32) * f32(2.0 / ROPE_DIM))
    pos = (first_pos + np.arange(n)).astype(f32)
    ang = (pos[:, None] * inv_freq[None, :]).astype(f32)
    cos, sin = np.cos(ang).astype(f32), np.sin(ang).astype(f32)
    one = np.ones((n, HEAD_DIM - ROPE_DIM), f32)
    zero8 = np.zeros((n, half), f32)
    zero48 = np.zeros((n, HEAD_DIM - ROPE_DIM), f32)
    c_head = np.concatenate([cos, cos, one], axis=1)
    s1_head = np.concatenate([zero8, sin, zero48], axis=1)
    s2_head = np.concatenate([-sin, zero8, zero48], axis=1)
    dup = lambda t: jnp.asarray(np.concatenate([t, t], axis=1))
    return dup(c_head), dup(s1_head), dup(s2_head)


def _rwkv_constants():
    t = np.arange(CHUNK)
    hh = np.arange(GROUP) // HEAD_DIM
    bd = (hh[:, None] == hh[None, :]).astype(np.float32)
    s = np.arange(GROUP) % CHUNK
    strict = (s[None, :] < t[:, None]).astype(np.float32)
    incl = (s[None, :] <= t[:, None]).astype(np.float32)
    eye = (s[None, :] == t[:, None]).astype(np.float32)
    return (jnp.asarray(bd, BF16), jnp.asarray(np.concatenate([strict, incl], axis=0)),
            jnp.asarray(eye))


def kernel(x, meta_tokens, norm_mix_g, w_in, b_in, attn_sinks, rwkv_mix, rwkv_w0, rwkv_w2, rwkv_a0, rwkv_a2, rwkv_g2, rwkv_k_k, rwkv_k_a, rwkv_r_k, rwkv_ln_w, rwkv_ln_b, w_br_attn, w_br_rwkv, w_o, norm_ffn_g, w_ffn_gate, w_ffn_up, w_ffn_down, norm_final_g):
    batch, seq, _ = x.shape
    layer = 0
    x2 = x.reshape(batch * seq, D_MODEL)

    w = w_in[layer]
    b = b_in[layer]
    scale = LOG2_E * HEAD_DIM ** -0.5
    h64 = HEAD_DIM
    kq, kk0, kk1 = Q_W, Q_W + h64, Q_W + 2 * h64
    vq = Q_W + KV_W

    def qkv_cols(t):
        return jnp.concatenate(
            [t[..., :Q_W] * scale,
             t[..., kq:kq + KV_W],
             t[..., kk0:kk1], t[..., kq:kk0],
             t[..., vq:vq + KV_W],
             t[..., vq + h64:vq + 2 * h64], t[..., vq:vq + h64]], axis=-1)

    def rw_cols(t, n_lead):
        pad = jnp.zeros(t.shape[:n_lead] + (GATE_PAD - GATE_LORA,), t.dtype)
        return jnp.concatenate([t, pad], axis=-1)

    wq = qkv_cols(w).astype(BF16)
    bq = qkv_cols(b)[None]
    wr = rw_cols(w[:, ATTN_PROJ:ATTN_PROJ + RWKV_PROJ], 1).astype(BF16)
    br = rw_cols(b[ATTN_PROJ:ATTN_PROJ + RWKV_PROJ], 0)[None]
    wg = w[:, ATTN_PROJ + RWKV_PROJ:].astype(BF16)
    bg = b[ATTN_PROJ + RWKV_PROJ:][None]
    g_mix = norm_mix_g[layer][None]

    mix = rw_cols(rwkv_mix[layer], 0)[None]
    zl = jnp.zeros((DECAY_LORA, RWKV_DIM), F32)
    w2a = jnp.concatenate(
        [jnp.concatenate([rwkv_w2[layer], zl], axis=1),
         jnp.concatenate([zl, rwkv_a2[layer]], axis=1)], axis=0).astype(BF16)
    g2p = jnp.concatenate(
        [rwkv_g2[layer], jnp.zeros((GATE_PAD - GATE_LORA, RWKV_DIM), F32)], axis=0).astype(BF16)
    vecs = (rwkv_w0[layer][None], rwkv_a0[layer][None], rwkv_k_k[layer][None],
            rwkv_k_a[layer][None], rwkv_r_k[layer].reshape(1, RWKV_DIM),
            rwkv_ln_w[layer][None], rwkv_ln_b[layer][None])

    cos_m, s1_m, s2_m = _rope_tables(0, N_META)
    qkv_m, p_m, _, p_m_tail = _inproj(meta_tokens, g_mix, wq, wr, wg, bq, br, bg,
                                      cos_m, s1_m, s2_m, jnp.zeros((8, RW_W), F32), mix, N_META)
    p_m_pad = jnp.concatenate([jnp.zeros((CHUNK - N_META, RW_W), F32), p_m], axis=0)
    zero_state = jnp.zeros((N_GROUPS, GROUP, GROUP), F32)
    _, s_meta = _rwkv(p_m_pad[None], zero_state, vecs, w2a, g2p, _rwkv_constants(), 1, 1, 1, False)

    cos, s1, s2 = _rope_tables(N_META, seq)
    tm_in = min(512, seq)
    qkv, p, gates, _ = _inproj(x2, g_mix, wq, wr, wg, bq, br, bg, cos, s1, s2,
                               p_m_tail, mix, tm_in)
    y_attn = _attn(attn_sinks[layer] * LOG2_E, qkv, qkv_m, batch, seq, ATTN_BLOCKS_PER_STEP)
    nc = min(RWKV_ROWS_PER_STEP, batch)
    y_rwkv, _ = _rwkv(p.reshape(batch, seq, RW_W), s_meta[0], vecs, w2a, g2p,
                      _rwkv_constants(), batch, seq // CHUNK, nc, True)
    out = _post(y_attn, y_rwkv.reshape(batch * seq, RWKV_DIM), gates, x2,
                w_br_attn[layer].astype(BF16), w_br_rwkv[layer].astype(BF16),
                w_o[layer].astype(BF16), norm_ffn_g[layer][None],
                w_ffn_gate[layer].astype(BF16), w_ffn_up[layer].astype(BF16),
                w_ffn_down[layer].astype(BF16), norm_final_g[None], min(POST_ROWS, seq))
    return out.reshape(batch, seq, D_MODEL)
```

```python
import functools

import jax
import jax.numpy as jnp
import numpy as np
from jax import lax
from jax.experimental import pallas as pl
from jax.experimental.pallas import tpu as pltpu

F32 = jnp.float32
BF16 = jnp.bfloat16

D_MODEL = 1024
N_META = 16
HEAD_DIM = 64
Q_HEADS = 8
KV_HEADS = 2
WINDOW = 128
ROPE_THETA = 500000.0
ROPE_DIM = HEAD_DIM // 4
RWKV_HEADS = 8
RWKV_DIM = RWKV_HEADS * HEAD_DIM
DECAY_LORA = 64
AAA_LORA = 64
GATE_LORA = 160
RWKV_LN_EPS = 64e-5
D_FF = 2816
Q_W = Q_HEADS * HEAD_DIM
KV_W = KV_HEADS * HEAD_DIM
ATTN_PROJ = Q_W + 2 * KV_W
RWKV_PROJ = 3 * RWKV_DIM + DECAY_LORA + AAA_LORA + GATE_LORA
RMS_EPS = 1e-6
NEG_INF = -1e30
LOG2_E = 1.4426950408889634

LANES = 128
MXU_COLS = 256
QKV_W = Q_W + 4 * KV_W
GATE_PAD = 256
RW_W = 3 * RWKV_DIM + LANES + GATE_PAD
CHUNK = 64
GROUP = 4 * HEAD_DIM
N_GROUPS = RWKV_DIM // GROUP
ATTN_BLOCKS_PER_STEP = 2
RWKV_ROWS_PER_STEP = 4
RWKV_CHAINS = 8
INPROJ_PARTS = 2
POST_ROWS = 512
POST_PARTS = 2
VMEM_LIMIT = 56 * 1024 * 1024


def _dot(a, b):
    return jnp.dot(a, b, preferred_element_type=F32)


def _dot_nt(a, b):
    return lax.dot_general(a, b, (((1,), (1,)), ((), ())), preferred_element_type=F32)


def _dot_tn(a, b):
    return lax.dot_general(a, b, (((0,), (0,)), ((), ())), preferred_element_type=F32)


def _sigmoid(x):
    return 1.0 / (1.0 + jnp.exp(-x))


def _params(n_grid_dims):
    return pltpu.CompilerParams(
        dimension_semantics=("arbitrary",) * n_grid_dims,
        vmem_limit_bytes=VMEM_LIMIT)


def _const_spec(shape):
    nd = len(shape)
    return pl.BlockSpec(shape, lambda *_: (0,) * nd, pipeline_mode=pl.Buffered(1))


def _inproj_kernel(x_ref, g_ref, wq_ref, wr_ref, wg_ref, bq_ref, br_ref, bg_ref,
                   cos_ref, s1_ref, s2_ref, pprev_ref, mix_ref,
                   qkv_ref, p_ref, gate_ref, plast_ref, pbuf, *, tm, seq_tiles, parts):
    @pl.when(pl.program_id(0) % seq_tiles == 0)
    def _():
        pbuf[0:8, :] = pprev_ref[...]

    pm = tm // parts
    rows = [slice(i * pm, (i + 1) * pm) for i in range(parts)]
    normed = {}

    def u(i):
        if i not in normed:
            x = x_ref[rows[i], :]
            ms = jnp.mean(x * x, axis=-1, keepdims=True)
            normed[i] = (x * lax.rsqrt(ms + RMS_EPS) * g_ref[...]).astype(BF16)
        return normed[i]

    def rope(t, r):
        return (t * cos_ref[r, :] + pltpu.roll(t, 8, 1) * s1_ref[r, :]
                + pltpu.roll(t, LANES - 8, 1) * s2_ref[r, :])

    def put(slab, r, t):
        qkv_ref[r, slab * LANES:(slab + 1) * LANES] = t.astype(BF16)

    q_slabs = Q_W // LANES
    for j in range(Q_W // MXU_COLS):
        sl = slice(j * MXU_COLS, (j + 1) * MXU_COLS)
        for i, r in enumerate(rows):
            t2 = _dot(u(i), wq_ref[:, sl]) + bq_ref[:, sl]
            for h in range(MXU_COLS // LANES):
                put(j * (MXU_COLS // LANES) + h, r, rope(t2[:, h * LANES:(h + 1) * LANES], r))
    for i, r in enumerate(rows):
        kv = _dot(u(i), wq_ref[:, Q_W:]) + bq_ref[:, Q_W:]
        k = rope(kv[:, :LANES], r)
        v = kv[:, LANES:]
        put(q_slabs, r, k)
        put(q_slabs + 1, r, pltpu.roll(k, HEAD_DIM, 1))
        put(q_slabs + 2, r, v)
        put(q_slabs + 3, r, pltpu.roll(v, HEAD_DIM, 1))

    for lo in range(0, RW_W, MXU_COLS):
        sl = slice(lo, min(lo + MXU_COLS, RW_W))
        pcs = [_dot(u(i), wr_ref[:, sl]) + br_ref[:, sl] for i in range(parts)]
        for r, pc in zip(rows, pcs):
            pbuf[8 + r.start:8 + r.stop, sl] = pc
        for r, pc in zip(rows, pcs):
            psh = pbuf[7 + r.start:7 + r.stop, sl]
            p_ref[r, sl] = pc + (psh - pc) * mix_ref[:, sl]
        tail = pcs[-1][pm - 8:pm, :]
        pbuf[0:8, sl] = tail
        plast_ref[:, sl] = tail

    for lo in range(0, 2 * D_MODEL, MXU_COLS):
        sl = slice(lo, lo + MXU_COLS)
        for i, r in enumerate(rows):
            gate_ref[r, sl] = _sigmoid(_dot(u(i), wg_ref[:, sl]) + bg_ref[:, sl]).astype(BF16)


def _inproj(x2, g, wq, wr, wg, bq, br, bg, cos, s1, s2, pprev, mix, tm):
    n = x2.shape[0]
    seq_tiles = cos.shape[0] // tm
    row = lambda i: (i, 0)
    pos = lambda i: (i % seq_tiles, 0)
    return pl.pallas_call(
        functools.partial(_inproj_kernel, tm=tm, seq_tiles=seq_tiles,
                          parts=INPROJ_PARTS if tm % (INPROJ_PARTS * 128) == 0 else 1),
        grid=(n // tm,),
        in_specs=[
            pl.BlockSpec((tm, D_MODEL), row),
            _const_spec((1, D_MODEL)),
            _const_spec((D_MODEL, ATTN_PROJ)),
            _const_spec((D_MODEL, RW_W)),
            _const_spec((D_MODEL, 2 * D_MODEL)),
            _const_spec((1, ATTN_PROJ)),
            _const_spec((1, RW_W)),
            _const_spec((1, 2 * D_MODEL)),
            pl.BlockSpec((tm, LANES), pos),
            pl.BlockSpec((tm, LANES), pos),
            pl.BlockSpec((tm, LANES), pos),
            _const_spec((8, RW_W)),
            _const_spec((1, RW_W)),
        ],
        out_specs=[
            pl.BlockSpec((tm, QKV_W), row),
            pl.BlockSpec((tm, RW_W), row),
            pl.BlockSpec((tm, 2 * D_MODEL), row),
            pl.BlockSpec((8, RW_W), lambda i: (0, 0)),
        ],
        out_shape=[
            jax.ShapeDtypeStruct((n, QKV_W), BF16),
            jax.ShapeDtypeStruct((n, RW_W), F32),
            jax.ShapeDtypeStruct((n, 2 * D_MODEL), BF16),
            jax.ShapeDtypeStruct((8, RW_W), F32),
        ],
        scratch_shapes=[pltpu.VMEM((8 + tm, RW_W), F32)],
        compiler_params=_params(1),
        name="inproj",
    )(x2, g, wq, wr, wg, bq, br, bg, cos, s1, s2, pprev, mix)


def _attn_kernel(sink_ref, q_ref, kvc_ref, kvp_ref, kvm_ref, band_ref, o_ref, *, qb):
    n = pl.program_id(1)
    blk = WINDOW
    lane = lax.broadcasted_iota(jnp.int32, (1, LANES), 1)
    m_lo = (lane < HEAD_DIM).astype(BF16)
    m_hi = (lane >= HEAD_DIM).astype(BF16)
    lo_half = lane < HEAD_DIM

    kv_all = jnp.concatenate([kvp_ref[...], kvc_ref[...]], axis=0)
    kv_meta = kvm_ref[...]

    band_mask = band_ref[...]
    col = lax.broadcasted_iota(jnp.int32, (1, 2 * blk), 1)
    no_prev = (1 - jnp.minimum(n, 1)).astype(F32)
    first_mask = band_mask + jnp.where(col < blk, NEG_INF, 0.0) * no_prev
    first = lax.broadcasted_iota(jnp.int32, (2 * blk, 2 * N_META), 1) < N_META
    top = lax.broadcasted_iota(jnp.int32, (2 * blk, 1), 0) < blk

    def halves(kv, g):
        k_plain, k_swap = kv[:, 0:LANES], kv[:, LANES:2 * LANES]
        v_plain, v_swap = kv[:, 2 * LANES:3 * LANES], kv[:, 3 * LANES:4 * LANES]
        if g == 0:
            k_lo, k_hi, v_lo, v_hi = k_plain, k_swap, v_plain, v_swap
        else:
            k_lo, k_hi, v_lo, v_hi = k_swap, k_plain, v_swap, v_plain
        return k_lo * m_lo, k_hi * m_hi, v_lo * m_lo, v_hi * m_hi

    band = [halves(kv_all, g) for g in range(KV_HEADS)]
    meta = [tuple(jnp.concatenate(pair, axis=0) for pair in
                  ((h[0], h[1]), (h[2], h[3])))
            for h in (halves(kv_meta, g) for g in range(KV_HEADS))]

    units = [(j, g) for j in range(qb) for g in range(KV_HEADS)]
    st = {}
    for key in units:
        j, g = key
        rows = slice(j * blk, (j + 2) * blk)
        k_lo, k_hi, v_lo, v_hi = band[g]
        kb = jnp.concatenate([k_lo[rows], k_hi[rows]], axis=0)
        vb = jnp.concatenate([v_lo[rows], v_hi[rows]], axis=0)
        q = jnp.concatenate(
            [q_ref[j * blk:(j + 1) * blk, s * LANES:(s + 1) * LANES] for s in (2 * g, 2 * g + 1)],
            axis=0)
        st[key] = dict(vb=vb,
                       sb=_dot_nt(q, kb),
                       sm=_dot_nt(q, meta[g][0]))

    for key in units:
        j, g = key
        u = st[key]
        mask = first_mask if j == 0 else band_mask
        sink_a = jnp.where(top, sink_ref[4 * g], sink_ref[4 * g + 2])
        sink_b = jnp.where(top, sink_ref[4 * g + 1], sink_ref[4 * g + 3])
        sb, sm = u["sb"], u["sm"]
        sa = sb[:, :2 * blk] + mask
        sbb = sb[:, 2 * blk:] + mask
        mx_a = jnp.maximum(
            jnp.maximum(jnp.max(sa, axis=-1, keepdims=True),
                        jnp.max(jnp.where(first, sm, NEG_INF), axis=-1, keepdims=True)),
            sink_a)
        mx_b = jnp.maximum(
            jnp.maximum(jnp.max(sbb, axis=-1, keepdims=True),
                        jnp.max(jnp.where(first, NEG_INF, sm), axis=-1, keepdims=True)),
            sink_b)
        pa = jnp.exp2(sa - mx_a)
        pb = jnp.exp2(sbb - mx_b)
        pm = jnp.exp2(sm - jnp.where(first, mx_a, mx_b))
        den_a = (jnp.sum(pa, axis=-1, keepdims=True)
                 + jnp.sum(jnp.where(first, pm, 0.0), axis=-1, keepdims=True)
                 + jnp.exp2(sink_a - mx_a))
        den_b = (jnp.sum(pb, axis=-1, keepdims=True)
                 + jnp.sum(jnp.where(first, 0.0, pm), axis=-1, keepdims=True)
                 + jnp.exp2(sink_b - mx_b))
        u["inv"] = jnp.where(lo_half, 1.0 / den_a, 1.0 / den_b)
        u["p_band"] = jnp.concatenate([pa, pb], axis=1).astype(BF16)
        u["p_meta"] = pm.astype(BF16)

    for key in units:
        j, g = key
        u = st[key]
        o = ((_dot(u["p_band"], u["vb"]) + _dot(u["p_meta"], meta[g][1])) * u["inv"]).astype(BF16)
        for i, s in enumerate((2 * g, 2 * g + 1)):
            o_ref[j * blk:(j + 1) * blk, s * LANES:(s + 1) * LANES] = o[i * blk:(i + 1) * blk]


def _attn(sinks, qkv, qkv_meta, batch, seq, qb):
    nb = seq // WINDOW
    steps = nb // qb
    n = batch * seq
    kvw = 4 * KV_W
    rows = qb * WINDOW
    qi = np.arange(2 * WINDOW)[:, None] % WINDOW
    c = np.arange(2 * WINDOW)[None, :]
    band_ok = np.where(c < WINDOW, c > qi, c - WINDOW <= qi)
    band_mask = jnp.asarray(np.where(band_ok, 0.0, NEG_INF).astype(np.float32))
    return pl.pallas_call(
        functools.partial(_attn_kernel, qb=qb),
        grid=(batch, steps),
        in_specs=[
            pl.BlockSpec(memory_space=pltpu.SMEM),
            pl.BlockSpec((rows, Q_W), lambda b, i: (b * steps + i, 0)),
            pl.BlockSpec((rows, kvw), lambda b, i: (b * steps + i, 1)),
            pl.BlockSpec((WINDOW, kvw), lambda b, i: (jnp.maximum((b * steps + i) * qb - 1, 0), 1)),
            pl.BlockSpec((N_META, kvw), lambda b, i: (0, 1)),
            _const_spec((2 * WINDOW, 2 * WINDOW)),
        ],
        out_specs=pl.BlockSpec((rows, Q_W), lambda b, i: (b * steps + i, 0)),
        out_shape=jax.ShapeDtypeStruct((n, Q_W), BF16),
        compiler_params=_params(2),
        name="attn",
    )(sinks, qkv, qkv, qkv, qkv_meta, band_mask)


def _emit_interleaved(*segment_lists):
    keyed = []
    for li, segs in enumerate(segment_lists):
        for j, seg in enumerate(segs):
            keyed.append(((j + 0.5) / len(segs), li, j, seg))
    for _, _, _, seg in sorted(keyed, key=lambda t: t[:3]):
        seg()


def _rwkv_kernel(*refs, nc, pipelined):
    if pipelined:
        (p0_ref, p1_ref, p2_ref, s0_ref, w0_ref, a0_ref, w2a_ref, g2_ref, kk_ref, ka_ref, rk_ref,
         lnw_ref, lnb_ref, bd_ref, tri_ref, eye_ref,
         y_ref, sfin_ref, s_scr, c_f32, c_pc, c_blk, c_rows, c_chunk) = refs
    else:
        (p0_ref, s0_ref, w0_ref, a0_ref, w2a_ref, g2_ref, kk_ref, ka_ref, rk_ref,
         lnw_ref, lnb_ref, bd_ref, tri_ref, eye_ref,
         y_ref, sfin_ref, s_scr) = refs
    c = pl.program_id(1)
    tok = nc * CHUNK
    o1, o2, o3 = RWKV_DIM, 2 * RWKV_DIM, 3 * RWKV_DIM
    pairs = [(gi, ri) for ri in range(nc) for gi in range(N_GROUPS)]

    bd = bd_ref[...]
    bd_f = bd.astype(F32)
    tri_mask = tri_ref[...]
    eye = eye_ref[...]
    lane = lax.broadcasted_iota(jnp.int32, (tok, LANES), 1)
    lane1 = lax.broadcasted_iota(jnp.int32, (1, LANES), 1)
    head_lo = (lane1 < HEAD_DIM).astype(BF16)
    head_hi = (lane1 >= HEAD_DIM).astype(BF16)
    zero_slab = jnp.zeros((CHUNK, LANES), BF16)

    def blk(x):
        xb = x.astype(BF16)
        left, right = xb[:, :LANES], xb[:, LANES:]
        return jnp.concatenate(
            [jnp.concatenate([left * head_lo, zero_slab], axis=1),
             jnp.concatenate([left * head_hi, zero_slab], axis=1),
             jnp.concatenate([zero_slab, right * head_lo], axis=1),
             jnp.concatenate([zero_slab, right * head_hi], axis=1)], axis=0)

    pos_in_chunk = lax.broadcasted_iota(jnp.int32, (tok, 1), 0) % CHUNK

    def seg_sums(xs):
        out = _dot(jnp.concatenate([x.astype(BF16) for x in xs], axis=0), bd)
        return [out[i * tok:(i + 1) * tok] for i in range(len(xs))]

    def grp(x, gi):
        return x[:, gi * GROUP:(gi + 1) * GROUP]

    def val(q, name):
        item = q[name]
        return item() if callable(item) else item

    def front_segments(p_ref, out, to_carry):
        f = {}

        def lora():
            pf = p_ref[...].reshape(tok, RW_W)
            f["r"], f["k"], f["v"] = pf[:, :o1], pf[:, o1:o2], pf[:, o2:o3]
            dwa = pf[:, o3:o3 + LANES]
            f["dg"] = pf[:, o3 + LANES:]
            z = jnp.where(lane < DECAY_LORA, jnp.tanh(dwa), dwa).astype(BF16)
            f["wa"] = _dot(z, w2a_ref[...])

        def decay_gate():
            zw = -(w0_ref[...] + f["wa"][:, :o1])
            softplus = jnp.maximum(zw, 0.0) + jnp.log(1.0 + jnp.exp(-jnp.abs(zw)))
            f["logw"] = -jnp.exp(-softplus - 0.5)
            f["a"] = _sigmoid(a0_ref[...] + f["wa"][:, o1:])
            f["gate"] = _dot(_sigmoid(f["dg"]).astype(BF16), g2_ref[...])

        def norms():
            f["kkv"] = f["k"] * kk_ref[...]
            f["kp"] = f["k"] * (1.0 + (f["a"] - 1.0) * ka_ref[...])
            rkk = f["r"] * f["kp"] * rk_ref[...]
            sq = f["kkv"] * f["kkv"]
            f["sums"] = seg_sums([grp(sq, 0), grp(sq, 1), grp(rkk, 0), grp(rkk, 1)])

        def cumulative():
            ss = jnp.concatenate(f["sums"][0:2], axis=1)
            f["bonus"] = jnp.concatenate(f["sums"][2:4], axis=1)
            kkn = f["kkv"] / jnp.maximum(jnp.sqrt(ss), 1e-12)
            f["aa"] = -kkn
            f["bb"] = kkn * f["a"]
            lp = f["logw"]
            step = 1
            while step < CHUNK:
                lp = lp + jnp.where(pos_in_chunk >= step, pltpu.roll(lp, step, 0), 0.0)
                step *= 2
            f["lps"] = [grp(lp, gi) for gi in range(N_GROUPS)]
            chunk = dict(gate=f["gate"], bonus=f["bonus"], v=f["v"])
            if to_carry is not None:
                c_chunk[to_carry, 0] = chunk["gate"]
                c_chunk[to_carry, 1] = chunk["bonus"]
                c_chunk[to_carry, 2] = chunk["v"]
            out["chunk"] = chunk
            out["pairs"] = {}

        def pair_prep(i, key):
            def run():
                gi, ri = key
                rows = slice(ri * CHUNK, (ri + 1) * CHUNK)
                cut = lambda x: x[rows, gi * GROUP:(gi + 1) * GROUP]
                lp = f["lps"][gi][rows]
                lw = cut(f["logw"])
                lpc = lp[CHUNK - 1:CHUNK, :]
                e_neg = jnp.exp(-lp)
                e_end = jnp.exp(lpc - lp)
                rg, kg, vg = cut(f["r"]), cut(f["kp"]), cut(f["v"])
                ag, bg = cut(f["aa"]), cut(f["bb"])
                at = ag * jnp.exp(lp - lw)
                rt = rg * jnp.exp(lp)
                q = dict(
                    at=at, vg=vg, pc=jnp.exp(lpc), vblk=blk(vg),
                    bt=blk(bg * e_neg), kt=blk(kg * e_neg),
                    ar=jnp.concatenate([at, rt], axis=0).astype(BF16),
                    bk_end=jnp.concatenate([bg * e_end, kg * e_end], axis=0).astype(BF16))
                if to_carry is not None:
                    c_f32[to_carry, i, 0], c_f32[to_carry, i, 1] = q["at"], q["vg"]
                    c_pc[to_carry, i, 0:1, :] = q["pc"]
                    c_blk[to_carry, i, 0], c_blk[to_carry, i, 1] = q["bt"], q["kt"]
                    c_blk[to_carry, i, 2] = q["vblk"]
                    c_rows[to_carry, i, 0], c_rows[to_carry, i, 1] = q["ar"], q["bk_end"]
                out["pairs"][key] = q
            return run

        return [lora, decay_gate, norms, cumulative] + [pair_prep(i, k) for i, k in enumerate(pairs)]

    def carried(slot):
        def pair(i):
            return dict(
                at=lambda: c_f32[slot, i, 0], vg=lambda: c_f32[slot, i, 1],
                pc=lambda: c_pc[slot, i, 0:1, :],
                bt=lambda: c_blk[slot, i, 0], kt=lambda: c_blk[slot, i, 1],
                vblk=lambda: c_blk[slot, i, 2],
                ar=lambda: c_rows[slot, i, 0], bk_end=lambda: c_rows[slot, i, 1])
        chunk = dict(gate=lambda: c_chunk[slot, 0], bonus=lambda: c_chunk[slot, 1],
                     v=lambda: c_chunk[slot, 2])
        return {key: pair(i) for i, key in enumerate(pairs)}, chunk

    def intra_segments(st):
        def stages(batch):
            def scores():
                for key in batch:
                    q = st[key]
                    ar = val(q, "ar")
                    sb = _dot_nt(ar, val(q, "bt")) * tri_mask
                    sk = _dot_nt(ar, val(q, "kt")) * tri_mask
                    q["a_ab"] = sb[:CHUNK]
                    q["a_rb"] = sb[CHUNK:].astype(BF16)
                    q["a_k"] = sk.astype(BF16)

            def square():
                for key in batch:
                    q = st[key]
                    q["pw"] = _dot(q["a_ab"].astype(BF16), blk(q["a_ab"]))
                    q["minv"] = eye + q["a_ab"]

            def neumann(j):
                def run():
                    for key in batch:
                        q = st[key]
                        if j < 5:
                            out = _dot(q["pw"].astype(BF16),
                                       jnp.concatenate([blk(q["pw"]), blk(q["minv"])], axis=1))
                            q["pw"] = out[:, :GROUP]
                            q["minv"] = q["minv"] + out[:, GROUP:]
                        else:
                            q["minv"] = q["minv"] + _dot(q["pw"].astype(BF16), blk(q["minv"]))
                return run

            def values():
                for key in batch:
                    q = st[key]
                    kv = _dot(q["a_k"], val(q, "vblk"))
                    q["av"] = kv[:CHUNK]
                    q["y_rk"] = kv[CHUNK:]

            def solve():
                for key in batch:
                    q = st[key]
                    wu = _dot(q["minv"].astype(BF16),
                              jnp.concatenate([blk(val(q, "at")), blk(q["av"])], axis=1))
                    q["w"] = wu[:, :GROUP].astype(BF16)
                    q["u0"] = wu[:, GROUP:]

            return [scores, square] + [neumann(j) for j in range(1, 6)] + [values, solve]

        segs = []
        for lo in range(0, len(pairs), RWKV_CHAINS):
            segs.extend(stages(pairs[lo:lo + RWKV_CHAINS]))
        return segs

    def state_back_segments(st, chunk, write_y):
        def apply_state():
            for key in pairs:
                gi, ri = key
                q = st[key]
                q["s"] = s_scr[ri, gi]
                rt = val(q, "ar")[CHUNK:]
                ws = _dot_nt(jnp.concatenate([q["w"], rt], axis=0), q["s"].astype(BF16))
                q["u"] = ws[:CHUNK] + q["u0"]
                q["y_rs"] = ws[CHUNK:]

        def outputs():
            for key in pairs:
                q = st[key]
                q["y"] = q["y_rs"] + _dot(q["a_rb"], blk(q["u"])) + q["y_rk"]

        def update_state():
            for key in pairs:
                gi, ri = key
                q = st[key]
                upd = _dot_tn(jnp.concatenate([q["u"], val(q, "vg")], axis=0).astype(BF16),
                              val(q, "bk_end"))
                s_scr[ri, gi] = q["s"] * val(q, "pc") + upd * bd_f

        g = {}

        def centre():
            g["ys"] = [jnp.concatenate([st[gi, ri]["y"] for ri in range(nc)], axis=0)
                       if nc > 1 else st[gi, 0]["y"] for gi in range(N_GROUPS)]
            means = seg_sums(g["ys"])
            g["ds"] = [g["ys"][i] - means[i] * (1.0 / HEAD_DIM) for i in range(N_GROUPS)]

        def variance():
            g["vars"] = seg_sums([d * d for d in g["ds"]])

        def finish():
            yn = jnp.concatenate(
                [g["ds"][i] * lax.rsqrt(g["vars"][i] * (1.0 / HEAD_DIM) + RWKV_LN_EPS)
                 for i in range(N_GROUPS)], axis=1)
            yn = yn * lnw_ref[...] + lnb_ref[...]
            out = (yn + val(chunk, "bonus") * val(chunk, "v")) * val(chunk, "gate")
            write_y(out.astype(BF16).reshape(nc, CHUNK, RWKV_DIM))

        return [apply_state, outputs, update_state, centre, variance, finish]

    def init_state():
        for ri in range(nc):
            s_scr[ri] = s0_ref[...]

    if pipelined:
        slot = c % 2

        @pl.when(c == 0)
        def _():
            init_state()
            _emit_interleaved(front_segments(p0_ref, {}, 0))

        def write_half(half):
            def write(y):
                y_ref[:, half * CHUNK:(half + 1) * CHUNK, :] = y
            return write

        first, first_chunk = carried(slot)
        second = {}
        _emit_interleaved(intra_segments(first), front_segments(p1_ref, second, None))
        ahead = front_segments(p2_ref, {}, 1 - slot)
        cut = len(ahead) // 2
        _emit_interleaved(intra_segments(second["pairs"]),
                          state_back_segments(first, first_chunk, write_half(0)), ahead[:cut])
        _emit_interleaved(state_back_segments(second["pairs"], second["chunk"], write_half(1)),
                          ahead[cut:])
    else:
        pl.when(c == 0)(init_state)
        only = {}

        def write_all(y):
            y_ref[...] = y

        _emit_interleaved(front_segments(p0_ref, only, None))
        _emit_interleaved(intra_segments(only["pairs"]))
        _emit_interleaved(state_back_segments(only["pairs"], only["chunk"], write_all))

    @pl.when(c == pl.num_programs(1) - 1)
    def _():
        sfin_ref[...] = s_scr[...]


def _rwkv(p, s0, vecs, w2a, g2p, consts, batch, n_chunks, nc, pipelined):
    w0, a0, k_k, k_a, r_k, ln_w, ln_b = vecs
    bd, tri_mask, eye = consts
    tok = nc * CHUNK
    n_pairs = nc * N_GROUPS
    vec = _const_spec((1, RWKV_DIM))
    chunk_spec = lambda index: pl.BlockSpec((nc, CHUNK, RW_W), index)
    if pipelined:
        steps = n_chunks // 2
        p_specs = [chunk_spec(lambda b, c: (b, 0, 0)),
                   chunk_spec(lambda b, c: (b, 2 * c + 1, 0)),
                   chunk_spec(lambda b, c: (b, jnp.minimum(2 * c + 2, n_chunks - 1), 0))]
        p_args = [p, p, p]
        y_rows = 2 * CHUNK
        carry = [pltpu.VMEM((2, n_pairs, 2, CHUNK, GROUP), F32),
                 pltpu.VMEM((2, n_pairs, 8, GROUP), F32),
                 pltpu.VMEM((2, n_pairs, 3, GROUP, GROUP), BF16),
                 pltpu.VMEM((2, n_pairs, 2, 2 * CHUNK, GROUP), BF16),
                 pltpu.VMEM((2, 3, tok, RWKV_DIM), F32)]
    else:
        steps = n_chunks
        p_specs = [chunk_spec(lambda b, c: (b, c, 0))]
        p_args = [p]
        y_rows = CHUNK
        carry = []
    return pl.pallas_call(
        functools.partial(_rwkv_kernel, nc=nc, pipelined=pipelined),
        grid=(batch // nc, steps),
        in_specs=p_specs + [
            _const_spec((N_GROUPS, GROUP, GROUP)),
            vec, vec,
            _const_spec((LANES, 2 * RWKV_DIM)),
            _const_spec((GATE_PAD, RWKV_DIM)),
            vec, vec, vec, vec, vec,
            _const_spec((GROUP, GROUP)),
            _const_spec((2 * CHUNK, GROUP)),
            _const_spec((CHUNK, GROUP)),
        ],
        out_specs=[
            pl.BlockSpec((nc, y_rows, RWKV_DIM), lambda b, c: (b, c, 0)),
            pl.BlockSpec((nc, N_GROUPS, GROUP, GROUP), lambda b, c: (b, 0, 0, 0)),
        ],
        out_shape=[
            jax.ShapeDtypeStruct((batch, n_chunks * CHUNK, RWKV_DIM), BF16),
            jax.ShapeDtypeStruct((batch, N_GROUPS, GROUP, GROUP), F32),
        ],
        scratch_shapes=[pltpu.VMEM((nc, N_GROUPS, GROUP, GROUP), F32)] + carry,
        compiler_params=_params(2),
        name="rwkv",
    )(*p_args, s0, w0, a0, w2a, g2p, k_k, k_a, r_k, ln_w, ln_b, bd, tri_mask, eye)


def _post_kernel(ya_ref, yr_ref, gate_ref, x_ref, wba_ref, wbr_ref, wo_ref, gf_ref,
                 wg_ref, wu_ref, wd_ref, gn_ref, o_ref, *, parts):
    pm = o_ref.shape[0] // parts
    rows = [slice(i * pm, (i + 1) * pm) for i in range(parts)]

    def rms(h, g_ref):
        ms = jnp.mean(h * h, axis=-1, keepdims=True)
        return h * lax.rsqrt(ms + RMS_EPS) * g_ref[...]

    merged = []
    for r in rows:
        gates = gate_ref[r, :].astype(F32)
        merged.append((gates[:, :D_MODEL] * _dot(ya_ref[r, :], wba_ref[...])
                       + gates[:, D_MODEL:] * _dot(yr_ref[r, :], wbr_ref[...])).astype(BF16))
    h1 = [x_ref[r, :] + _dot(m, wo_ref[...]) for r, m in zip(rows, merged)]
    acts = []
    for h in h1:
        f = rms(h, gf_ref).astype(BF16)
        gt = _dot(f, wg_ref[...])
        up = _dot(f, wu_ref[...])
        acts.append((gt * _sigmoid(gt) * up).astype(BF16))
    h2 = [h + _dot(a, wd_ref[...]) for h, a in zip(h1, acts)]
    for r, h in zip(rows, h2):
        o_ref[r, :] = rms(h, gn_ref)


def _post(ya, yr, gates, x2, wba, wbr, wo, gf, wg, wu, wd, gn, tm):
    n = x2.shape[0]
    row = lambda i: (i, 0)
    return pl.pallas_call(
        functools.partial(_post_kernel, parts=POST_PARTS),
        grid=(n // tm,),
        in_specs=[
            pl.BlockSpec((tm, Q_W), row),
            pl.BlockSpec((tm, RWKV_DIM), row),
            pl.BlockSpec((tm, 2 * D_MODEL), row),
            pl.BlockSpec((tm, D_MODEL), row),
            _const_spec((Q_W, D_MODEL)),
            _const_spec((RWKV_DIM, D_MODEL)),
            _const_spec((D_MODEL, D_MODEL)),
            _const_spec((1, D_MODEL)),
            _const_spec((D_MODEL, D_FF)),
            _const_spec((D_MODEL, D_FF)),
            _const_spec((D_FF, D_MODEL)),
            _const_spec((1, D_MODEL)),
        ],
        out_specs=pl.BlockSpec((tm, D_MODEL), row),
        out_shape=jax.ShapeDtypeStruct((n, D_MODEL), F32),
        compiler_params=_params(1),
        name="post",
    )(ya, yr, gates, x2, wba, wbr, wo, gf, wg, wu, wd, gn)


def _rope_tables(first_pos, n):
    half = ROPE_DIM // 2
    f32 = np.float32
    inv_freq = np.power(f32(ROPE_THETA), -np.arange(half, dtype=f32) * f32(2.0 / ROPE_DIM))
    pos = (first_pos + np.arange(n)).astype(f32)
    ang = (pos[:, None] * inv_freq[None, :]).astype(f32)
    cos, sin = np.cos(ang).astype(f32), np.sin(ang).astype(f32)
    one = np.ones((n, HEAD_DIM - ROPE_DIM), f32)
    zero8 = np.zeros((n, half), f32)
    zero48 = np.zeros((n, HEAD_DIM - ROPE_DIM), f32)
    c_head = np.concatenate([cos, cos, one], axis=1)
    s1_head = np.concatenate([zero8, sin, zero48], axis=1)
    s2_head = np.concatenate([-sin, zero8, zero48], axis=1)
    dup = lambda t: jnp.asarray(np.concatenate([t, t], axis=1))
    return dup(c_head), dup(s1_head), dup(s2_head)


def _rwkv_constants():
    t = np.arange(CHUNK)
    hh = np.arange(GROUP) // HEAD_DIM
    bd = (hh[:, None] == hh[None, :]).astype(np.float32)
    s = np.arange(GROUP) % CHUNK
    strict = (s[None, :] < t[:, None]).astype(np.float32)
    incl = (s[None, :] <= t[:, None]).astype(np.float32)
    eye = (s[None, :] == t[:, None]).astype(np.float32)
    return (jnp.asarray(bd, BF16), jnp.asarray(np.concatenate([strict, incl], axis=0)),
            jnp.asarray(eye))


def kernel(x, meta_tokens, norm_mix_g, w_in, b_in, attn_sinks, rwkv_mix, rwkv_w0, rwkv_w2, rwkv_a0, rwkv_a2, rwkv_g2, rwkv_k_k, rwkv_k_a, rwkv_r_k, rwkv_ln_w, rwkv_ln_b, w_br_attn, w_br_rwkv, w_o, norm_ffn_g, w_ffn_gate, w_ffn_up, w_ffn_down, norm_final_g):
    batch, seq, _ = x.shape
    layer = 0
    x2 = x.reshape(batch * seq, D_MODEL)

    w = w_in[layer]
    b = b_in[layer]
    scale = LOG2_E * HEAD_DIM ** -0.5

    def qkv_cols(t):
        return jnp.concatenate([t[..., :Q_W] * scale, t[..., Q_W:ATTN_PROJ]], axis=-1)

    def rw_cols(t, n_lead):
        pad = jnp.zeros(t.shape[:n_lead] + (GATE_PAD - GATE_LORA,), t.dtype)
        return jnp.concatenate([t, pad], axis=-1)

    wq = qkv_cols(w).astype(BF16)
    bq = qkv_cols(b)[None]
    wr = rw_cols(w[:, ATTN_PROJ:ATTN_PROJ + RWKV_PROJ], 1).astype(BF16)
    br = rw_cols(b[ATTN_PROJ:ATTN_PROJ + RWKV_PROJ], 0)[None]
    wg = w[:, ATTN_PROJ + RWKV_PROJ:].astype(BF16)
    bg = b[ATTN_PROJ + RWKV_PROJ:][None]
    g_mix = norm_mix_g[layer][None]

    mix = rw_cols(rwkv_mix[layer], 0)[None]
    zl = jnp.zeros((DECAY_LORA, RWKV_DIM), F32)
    w2a = jnp.concatenate(
        [jnp.concatenate([rwkv_w2[layer], zl], axis=1),
         jnp.concatenate([zl, rwkv_a2[layer]], axis=1)], axis=0).astype(BF16)
    g2p = jnp.concatenate(
        [rwkv_g2[layer], jnp.zeros((GATE_PAD - GATE_LORA, RWKV_DIM), F32)], axis=0).astype(BF16)
    vecs = (rwkv_w0[layer][None], rwkv_a0[layer][None], rwkv_k_k[layer][None],
            rwkv_k_a[layer][None], rwkv_r_k[layer].reshape(1, RWKV_DIM),
            rwkv_ln_w[layer][None], rwkv_ln_b[layer][None])

    cos_m, s1_m, s2_m = _rope_tables(0, N_META)
    qkv_m, p_m, _, p_m_tail = _inproj(meta_tokens, g_mix, wq, wr, wg, bq, br, bg,
                                      cos_m, s1_m, s2_m, jnp.zeros((8, RW_W), F32), mix, N_META)
    p_m_pad = jnp.concatenate([jnp.zeros((CHUNK - N_META, RW_W), F32), p_m], axis=0)
    zero_state = jnp.zeros((N_GROUPS, GROUP, GROUP), F32)
    _, s_meta = _rwkv(p_m_pad[None], zero_state, vecs, w2a, g2p, _rwkv_constants(), 1, 1, 1, False)

    cos, s1, s2 = _rope_tables(N_META, seq)
    tm_in = min(512, seq)
    qkv, p, gates, _ = _inproj(x2, g_mix, wq, wr, wg, bq, br, bg, cos, s1, s2,
                               p_m_tail, mix, tm_in)
    y_attn = _attn(attn_sinks[layer] * LOG2_E, qkv, qkv_m, batch, seq, ATTN_BLOCKS_PER_STEP)
    nc = min(RWKV_ROWS_PER_STEP, batch)
    y_rwkv, _ = _rwkv(p.reshape(batch, seq, RW_W), s_meta[0], vecs, w2a, g2p,
                      _rwkv_constants(), batch, seq // CHUNK, nc, True)
    out = _post(y_attn, y_rwkv.reshape(batch * seq, RWKV_DIM), gates, x2,
                w_br_attn[layer].astype(BF16), w_br_rwkv[layer].astype(BF16),
                w_o[layer].astype(BF16), norm_ffn_g[layer][None],
                w_ffn_gate[layer].astype(BF16), w_ffn_up[layer].astype(BF16),
                w_ffn_down[layer].astype(BF16), norm_final_g[None], min(POST_ROWS, seq))
    return out.reshape(batch, seq, D_MODEL)
```

```python
import functools

import jax
import jax.numpy as jnp
import numpy as np
from jax import lax
from jax.experimental import pallas as pl
from jax.experimental.pallas import tpu as pltpu

F32 = jnp.float32
BF16 = jnp.bfloat16

D_MODEL = 1024
N_META = 16
HEAD_DIM = 64
Q_HEADS = 8
KV_HEADS = 2
WINDOW = 128
ROPE_THETA = 500000.0
ROPE_DIM = HEAD_DIM // 4
RWKV_HEADS = 8
RWKV_DIM = RWKV_HEADS * HEAD_DIM
DECAY_LORA = 64
AAA_LORA = 64
GATE_LORA = 160
RWKV_LN_EPS = 64e-5
D_FF = 2816
Q_W = Q_HEADS * HEAD_DIM
KV_W = KV_HEADS * HEAD_DIM
ATTN_PROJ = Q_W + 2 * KV_W
RWKV_PROJ = 3 * RWKV_DIM + DECAY_LORA + AAA_LORA + GATE_LORA
RMS_EPS = 1e-6
NEG_INF = -1e30
LOG2_E = 1.4426950408889634

LANES = 128
MXU_COLS = 256
QKV_W = Q_W + 4 * KV_W
GATE_PAD = 256
RW_W = 3 * RWKV_DIM + LANES + GATE_PAD
CHUNK = 64
GROUP = 4 * HEAD_DIM
N_GROUPS = RWKV_DIM // GROUP
ATTN_BLOCKS_PER_STEP = 8
RWKV_ROWS_PER_STEP = 4
RWKV_CHAINS = 8
INPROJ_PARTS = 2
POST_ROWS = 512
POST_PARTS = 2
VMEM_LIMIT = 56 * 1024 * 1024


def _dot(a, b):
    return jnp.dot(a, b, preferred_element_type=F32)


def _dot_nt(a, b):
    return lax.dot_general(a, b, (((1,), (1,)), ((), ())), preferred_element_type=F32)


def _dot_tn(a, b):
    return lax.dot_general(a, b, (((0,), (0,)), ((), ())), preferred_element_type=F32)


def _sigmoid(x):
    return 1.0 / (1.0 + jnp.exp(-x))


def _params(n_grid_dims):
    return pltpu.CompilerParams(
        dimension_semantics=("arbitrary",) * n_grid_dims,
        vmem_limit_bytes=VMEM_LIMIT)


def _const_spec(shape):
    nd = len(shape)
    return pl.BlockSpec(shape, lambda *_: (0,) * nd, pipeline_mode=pl.Buffered(1))


def _inproj_kernel(x_ref, g_ref, wq_ref, wr_ref, wg_ref, bq_ref, br_ref, bg_ref,
                   cos_ref, s1_ref, s2_ref, pprev_ref, mix_ref,
                   qkv_ref, p_ref, gate_ref, plast_ref, pbuf, *, tm, seq_tiles, parts):
    @pl.when(pl.program_id(0) % seq_tiles == 0)
    def _():
        pbuf[0:8, :] = pprev_ref[...]

    pm = tm // parts
    rows = [slice(i * pm, (i + 1) * pm) for i in range(parts)]
    normed = {}

    def u(i):
        if i not in normed:
            x = x_ref[rows[i], :]
            ms = jnp.mean(x * x, axis=-1, keepdims=True)
            normed[i] = (x * lax.rsqrt(ms + RMS_EPS) * g_ref[...]).astype(BF16)
        return normed[i]

    def rope(t, r):
        return (t * cos_ref[r, :] + pltpu.roll(t, 8, 1) * s1_ref[r, :]
                + pltpu.roll(t, LANES - 8, 1) * s2_ref[r, :])

    def put(slab, r, t):
        qkv_ref[r, slab * LANES:(slab + 1) * LANES] = t.astype(BF16)

    q_slabs = Q_W // LANES
    for j in range(Q_W // MXU_COLS):
        sl = slice(j * MXU_COLS, (j + 1) * MXU_COLS)
        for i, r in enumerate(rows):
            t2 = _dot(u(i), wq_ref[:, sl]) + bq_ref[:, sl]
            for h in range(MXU_COLS // LANES):
                put(j * (MXU_COLS // LANES) + h, r, rope(t2[:, h * LANES:(h + 1) * LANES], r))
    for i, r in enumerate(rows):
        kv = _dot(u(i), wq_ref[:, Q_W:]) + bq_ref[:, Q_W:]
        k = rope(kv[:, :LANES], r)
        v = kv[:, LANES:]
        put(q_slabs, r, k)
        put(q_slabs + 1, r, pltpu.roll(k, HEAD_DIM, 1))
        put(q_slabs + 2, r, v)
        put(q_slabs + 3, r, pltpu.roll(v, HEAD_DIM, 1))

    for lo in range(0, RW_W, MXU_COLS):
        sl = slice(lo, min(lo + MXU_COLS, RW_W))
        pcs = [_dot(u(i), wr_ref[:, sl]) + br_ref[:, sl] for i in range(parts)]
        for r, pc in zip(rows, pcs):
            pbuf[8 + r.start:8 + r.stop, sl] = pc
        for r, pc in zip(rows, pcs):
            psh = pbuf[7 + r.start:7 + r.stop, sl]
            p_ref[r, sl] = pc + (psh - pc) * mix_ref[:, sl]
        tail = pcs[-1][pm - 8:pm, :]
        pbuf[0:8, sl] = tail
        plast_ref[:, sl] = tail

    for lo in range(0, 2 * D_MODEL, MXU_COLS):
        sl = slice(lo, lo + MXU_COLS)
        for i, r in enumerate(rows):
            gate_ref[r, sl] = _sigmoid(_dot(u(i), wg_ref[:, sl]) + bg_ref[:, sl]).astype(BF16)


def _inproj(x2, g, wq, wr, wg, bq, br, bg, cos, s1, s2, pprev, mix, tm):
    n = x2.shape[0]
    seq_tiles = cos.shape[0] // tm
    row = lambda i: (i, 0)
    pos = lambda i: (i % seq_tiles, 0)
    return pl.pallas_call(
        functools.partial(_inproj_kernel, tm=tm, seq_tiles=seq_tiles,
                          parts=INPROJ_PARTS if tm % (INPROJ_PARTS * 128) == 0 else 1),
        grid=(n // tm,),
        in_specs=[
            pl.BlockSpec((tm, D_MODEL), row),
            _const_spec((1, D_MODEL)),
            _const_spec((D_MODEL, ATTN_PROJ)),
            _const_spec((D_MODEL, RW_W)),
            _const_spec((D_MODEL, 2 * D_MODEL)),
            _const_spec((1, ATTN_PROJ)),
            _const_spec((1, RW_W)),
            _const_spec((1, 2 * D_MODEL)),
            pl.BlockSpec((tm, LANES), pos),
            pl.BlockSpec((tm, LANES), pos),
            pl.BlockSpec((tm, LANES), pos),
            _const_spec((8, RW_W)),
            _const_spec((1, RW_W)),
        ],
        out_specs=[
            pl.BlockSpec((tm, QKV_W), row),
            pl.BlockSpec((tm, RW_W), row),
            pl.BlockSpec((tm, 2 * D_MODEL), row),
            pl.BlockSpec((8, RW_W), lambda i: (0, 0)),
        ],
        out_shape=[
            jax.ShapeDtypeStruct((n, QKV_W), BF16),
            jax.ShapeDtypeStruct((n, RW_W), F32),
            jax.ShapeDtypeStruct((n, 2 * D_MODEL), BF16),
            jax.ShapeDtypeStruct((8, RW_W), F32),
        ],
        scratch_shapes=[pltpu.VMEM((8 + tm, RW_W), F32)],
        compiler_params=_params(1),
        name="inproj",
    )(x2, g, wq, wr, wg, bq, br, bg, cos, s1, s2, pprev, mix)


def _attn_kernel(sink_ref, q_ref, kvc_ref, kvp_ref, kvm_ref, band_ref, o_ref, *, qb):
    n = pl.program_id(1)
    blk = WINDOW
    lane = lax.broadcasted_iota(jnp.int32, (1, LANES), 1)
    m_lo = (lane < HEAD_DIM).astype(BF16)
    m_hi = (lane >= HEAD_DIM).astype(BF16)
    lo_half = lane < HEAD_DIM

    kv_all = jnp.concatenate([kvp_ref[...], kvc_ref[...]], axis=0)
    kv_meta = kvm_ref[...]

    band_mask = band_ref[...]
    col = lax.broadcasted_iota(jnp.int32, (1, 2 * blk), 1)
    no_prev = (1 - jnp.minimum(n, 1)).astype(F32)
    first_mask = band_mask + jnp.where(col < blk, NEG_INF, 0.0) * no_prev
    first = lax.broadcasted_iota(jnp.int32, (2 * blk, 2 * N_META), 1) < N_META
    top = lax.broadcasted_iota(jnp.int32, (2 * blk, 1), 0) < blk

    def halves(kv, g):
        k_plain, k_swap = kv[:, 0:LANES], kv[:, LANES:2 * LANES]
        v_plain, v_swap = kv[:, 2 * LANES:3 * LANES], kv[:, 3 * LANES:4 * LANES]
        if g == 0:
            k_lo, k_hi, v_lo, v_hi = k_plain, k_swap, v_plain, v_swap
        else:
            k_lo, k_hi, v_lo, v_hi = k_swap, k_plain, v_swap, v_plain
        return k_lo * m_lo, k_hi * m_hi, v_lo * m_lo, v_hi * m_hi

    band = [halves(kv_all, g) for g in range(KV_HEADS)]
    meta = [tuple(jnp.concatenate(pair, axis=0) for pair in
                  ((h[0], h[1]), (h[2], h[3])))
            for h in (halves(kv_meta, g) for g in range(KV_HEADS))]

    units = [(j, g) for j in range(qb) for g in range(KV_HEADS)]
    st = {}
    for key in units:
        j, g = key
        rows = slice(j * blk, (j + 2) * blk)
        k_lo, k_hi, v_lo, v_hi = band[g]
        kb = jnp.concatenate([k_lo[rows], k_hi[rows]], axis=0)
        vb = jnp.concatenate([v_lo[rows], v_hi[rows]], axis=0)
        q = jnp.concatenate(
            [q_ref[j * blk:(j + 1) * blk, s * LANES:(s + 1) * LANES] for s in (2 * g, 2 * g + 1)],
            axis=0)
        st[key] = dict(vb=vb,
                       sb=_dot_nt(q, kb),
                       sm=_dot_nt(q, meta[g][0]))

    for key in units:
        j, g = key
        u = st[key]
        mask = first_mask if j == 0 else band_mask
        sink_a = jnp.where(top, sink_ref[4 * g], sink_ref[4 * g + 2])
        sink_b = jnp.where(top, sink_ref[4 * g + 1], sink_ref[4 * g + 3])
        sb, sm = u["sb"], u["sm"]
        sa = sb[:, :2 * blk] + mask
        sbb = sb[:, 2 * blk:] + mask
        mx_a = jnp.maximum(
            jnp.maximum(jnp.max(sa, axis=-1, keepdims=True),
                        jnp.max(jnp.where(first, sm, NEG_INF), axis=-1, keepdims=True)),
            sink_a)
        mx_b = jnp.maximum(
            jnp.maximum(jnp.max(sbb, axis=-1, keepdims=True),
                        jnp.max(jnp.where(first, NEG_INF, sm), axis=-1, keepdims=True)),
            sink_b)
        pa = jnp.exp2(sa - mx_a)
        pb = jnp.exp2(sbb - mx_b)
        pm = jnp.exp2(sm - jnp.where(first, mx_a, mx_b))
        den_a = (jnp.sum(pa, axis=-1, keepdims=True)
                 + jnp.sum(jnp.where(first, pm, 0.0), axis=-1, keepdims=True)
                 + jnp.exp2(sink_a - mx_a))
        den_b = (jnp.sum(pb, axis=-1, keepdims=True)
                 + jnp.sum(jnp.where(first, 0.0, pm), axis=-1, keepdims=True)
                 + jnp.exp2(sink_b - mx_b))
        u["inv"] = jnp.where(lo_half, 1.0 / den_a, 1.0 / den_b)
        u["p_band"] = jnp.concatenate([pa, pb], axis=1).astype(BF16)
        u["p_meta"] = pm.astype(BF16)

    for key in units:
        j, g = key
        u = st[key]
        o = ((_dot(u["p_band"], u["vb"]) + _dot(u["p_meta"], meta[g][1])) * u["inv"]).astype(BF16)
        for i, s in enumerate((2 * g, 2 * g + 1)):
            o_ref[j * blk:(j + 1) * blk, s * LANES:(s + 1) * LANES] = o[i * blk:(i + 1) * blk]


def _attn(sinks, qkv, qkv_meta, batch, seq, qb):
    nb = seq // WINDOW
    steps = nb // qb
    n = batch * seq
    kvw = 4 * KV_W
    rows = qb * WINDOW
    qi = np.arange(2 * WINDOW)[:, None] % WINDOW
    c = np.arange(2 * WINDOW)[None, :]
    band_ok = np.where(c < WINDOW, c > qi, c - WINDOW <= qi)
    band_mask = jnp.asarray(np.where(band_ok, 0.0, NEG_INF).astype(np.float32))
    return pl.pallas_call(
        functools.partial(_attn_kernel, qb=qb),
        grid=(batch, steps),
        in_specs=[
            pl.BlockSpec(memory_space=pltpu.SMEM),
            pl.BlockSpec((rows, Q_W), lambda b, i: (b * steps + i, 0)),
            pl.BlockSpec((rows, kvw), lambda b, i: (b * steps + i, 1)),
            pl.BlockSpec((WINDOW, kvw), lambda b, i: (jnp.maximum((b * steps + i) * qb - 1, 0), 1)),
            pl.BlockSpec((N_META, kvw), lambda b, i: (0, 1)),
            _const_spec((2 * WINDOW, 2 * WINDOW)),
        ],
        out_specs=pl.BlockSpec((rows, Q_W), lambda b, i: (b * steps + i, 0)),
        out_shape=jax.ShapeDtypeStruct((n, Q_W), BF16),
        compiler_params=_params(2),
        name="attn",
    )(sinks, qkv, qkv, qkv, qkv_meta, band_mask)


def _emit_interleaved(*segment_lists):
    keyed = []
    for li, segs in enumerate(segment_lists):
        for j, seg in enumerate(segs):
            keyed.append(((j + 0.5) / len(segs), li, j, seg))
    for _, _, _, seg in sorted(keyed, key=lambda t: t[:3]):
        seg()


def _rwkv_kernel(*refs, nc, pipelined):
    if pipelined:
        (p0_ref, p1_ref, p2_ref, s0_ref, w0_ref, a0_ref, w2a_ref, g2_ref, kk_ref, ka_ref, rk_ref,
         lnw_ref, lnb_ref, bd_ref, tri_ref, eye_ref,
         y_ref, sfin_ref, s_scr, c_f32, c_pc, c_blk, c_rows, c_chunk) = refs
    else:
        (p0_ref, s0_ref, w0_ref, a0_ref, w2a_ref, g2_ref, kk_ref, ka_ref, rk_ref,
         lnw_ref, lnb_ref, bd_ref, tri_ref, eye_ref,
         y_ref, sfin_ref, s_scr) = refs
    c = pl.program_id(1)
    tok = nc * CHUNK
    o1, o2, o3 = RWKV_DIM, 2 * RWKV_DIM, 3 * RWKV_DIM
    pairs = [(gi, ri) for ri in range(nc) for gi in range(N_GROUPS)]

    bd = bd_ref[...]
    bd_f = bd.astype(F32)
    tri_mask = tri_ref[...]
    eye = eye_ref[...]
    lane = lax.broadcasted_iota(jnp.int32, (tok, LANES), 1)
    lane1 = lax.broadcasted_iota(jnp.int32, (1, LANES), 1)
    head_lo = (lane1 < HEAD_DIM).astype(BF16)
    head_hi = (lane1 >= HEAD_DIM).astype(BF16)
    zero_slab = jnp.zeros((CHUNK, LANES), BF16)

    def blk(x):
        xb = x.astype(BF16)
        left, right = xb[:, :LANES], xb[:, LANES:]
        return jnp.concatenate(
            [jnp.concatenate([left * head_lo, zero_slab], axis=1),
             jnp.concatenate([left * head_hi, zero_slab], axis=1),
             jnp.concatenate([zero_slab, right * head_lo], axis=1),
             jnp.concatenate([zero_slab, right * head_hi], axis=1)], axis=0)

    pos_in_chunk = lax.broadcasted_iota(jnp.int32, (tok, 1), 0) % CHUNK

    def seg_sums(xs):
        out = _dot(jnp.concatenate([x.astype(BF16) for x in xs], axis=0), bd)
        return [out[i * tok:(i + 1) * tok] for i in range(len(xs))]

    def grp(x, gi):
        return x[:, gi * GROUP:(gi + 1) * GROUP]

    def val(q, name):
        item = q[name]
        return item() if callable(item) else item

    def front_segments(p_ref, out, to_carry):
        f = {}

        def lora():
            pf = p_ref[...].reshape(tok, RW_W)
            f["r"], f["k"], f["v"] = pf[:, :o1], pf[:, o1:o2], pf[:, o2:o3]
            dwa = pf[:, o3:o3 + LANES]
            f["dg"] = pf[:, o3 + LANES:]
            z = jnp.where(lane < DECAY_LORA, jnp.tanh(dwa), dwa).astype(BF16)
            f["wa"] = _dot(z, w2a_ref[...])

        def decay_gate():
            zw = -(w0_ref[...] + f["wa"][:, :o1])
            softplus = jnp.maximum(zw, 0.0) + jnp.log(1.0 + jnp.exp(-jnp.abs(zw)))
            f["logw"] = -jnp.exp(-softplus - 0.5)
            f["a"] = _sigmoid(a0_ref[...] + f["wa"][:, o1:])
            f["gate"] = _dot(_sigmoid(f["dg"]).astype(BF16), g2_ref[...])

        def norms():
            f["kkv"] = f["k"] * kk_ref[...]
            f["kp"] = f["k"] * (1.0 + (f["a"] - 1.0) * ka_ref[...])
            rkk = f["r"] * f["kp"] * rk_ref[...]
            sq = f["kkv"] * f["kkv"]
            f["sums"] = seg_sums([grp(sq, 0), grp(sq, 1), grp(rkk, 0), grp(rkk, 1)])

        def cumulative():
            ss = jnp.concatenate(f["sums"][0:2], axis=1)
            f["bonus"] = jnp.concatenate(f["sums"][2:4], axis=1)
            kkn = f["kkv"] / jnp.maximum(jnp.sqrt(ss), 1e-12)
            f["aa"] = -kkn
            f["bb"] = kkn * f["a"]
            lp = f["logw"]
            step = 1
            while step < CHUNK:
                lp = lp + jnp.where(pos_in_chunk >= step, pltpu.roll(lp, step, 0), 0.0)
                step *= 2
            f["lps"] = [grp(lp, gi) for gi in range(N_GROUPS)]
            chunk = dict(gate=f["gate"], bonus=f["bonus"], v=f["v"])
            if to_carry is not None:
                c_chunk[to_carry, 0] = chunk["gate"]
                c_chunk[to_carry, 1] = chunk["bonus"]
                c_chunk[to_carry, 2] = chunk["v"]
            out["chunk"] = chunk
            out["pairs"] = {}

        def pair_prep(i, key):
            def run():
                gi, ri = key
                rows = slice(ri * CHUNK, (ri + 1) * CHUNK)
                cut = lambda x: x[rows, gi * GROUP:(gi + 1) * GROUP]
                lp = f["lps"][gi][rows]
                lw = cut(f["logw"])
                lpc = lp[CHUNK - 1:CHUNK, :]
                e_neg = jnp.exp(-lp)
                e_end = jnp.exp(lpc - lp)
                rg, kg, vg = cut(f["r"]), cut(f["kp"]), cut(f["v"])
                ag, bg = cut(f["aa"]), cut(f["bb"])
                at = ag * jnp.exp(lp - lw)
                rt = rg * jnp.exp(lp)
                q = dict(
                    at=at, vg=vg, pc=jnp.exp(lpc), vblk=blk(vg),
                    bt=blk(bg * e_neg), kt=blk(kg * e_neg),
                    ar=jnp.concatenate([at, rt], axis=0).astype(BF16),
                    bk_end=jnp.concatenate([bg * e_end, kg * e_end], axis=0).astype(BF16))
                if to_carry is not None:
                    c_f32[to_carry, i, 0], c_f32[to_carry, i, 1] = q["at"], q["vg"]
                    c_pc[to_carry, i, 0:1, :] = q["pc"]
                    c_blk[to_carry, i, 0], c_blk[to_carry, i, 1] = q["bt"], q["kt"]
                    c_blk[to_carry, i, 2] = q["vblk"]
                    c_rows[to_carry, i, 0], c_rows[to_carry, i, 1] = q["ar"], q["bk_end"]
                out["pairs"][key] = q
            return run

        return [lora, decay_gate, norms, cumulative] + [pair_prep(i, k) for i, k in enumerate(pairs)]

    def carried(slot):
        def pair(i):
            return dict(
                at=lambda: c_f32[slot, i, 0], vg=lambda: c_f32[slot, i, 1],
                pc=lambda: c_pc[slot, i, 0:1, :],
                bt=lambda: c_blk[slot, i, 0], kt=lambda: c_blk[slot, i, 1],
                vblk=lambda: c_blk[slot, i, 2],
                ar=lambda: c_rows[slot, i, 0], bk_end=lambda: c_rows[slot, i, 1])
        chunk = dict(gate=lambda: c_chunk[slot, 0], bonus=lambda: c_chunk[slot, 1],
                     v=lambda: c_chunk[slot, 2])
        return {key: pair(i) for i, key in enumerate(pairs)}, chunk

    def intra_segments(st):
        def stages(batch):
            def scores():
                for key in batch:
                    q = st[key]
                    ar = val(q, "ar")
                    sb = _dot_nt(ar, val(q, "bt")) * tri_mask
                    sk = _dot_nt(ar, val(q, "kt")) * tri_mask
                    q["a_ab"] = sb[:CHUNK]
                    q["a_rb"] = sb[CHUNK:].astype(BF16)
                    q["a_k"] = sk.astype(BF16)

            def square():
                for key in batch:
                    q = st[key]
                    q["pw"] = _dot(q["a_ab"].astype(BF16), blk(q["a_ab"]))
                    q["minv"] = eye + q["a_ab"]

            def neumann(j):
                def run():
                    for key in batch:
                        q = st[key]
                        if j < 5:
                            out = _dot(q["pw"].astype(BF16),
                                       jnp.concatenate([blk(q["pw"]), blk(q["minv"])], axis=1))
                            q["pw"] = out[:, :GROUP]
                            q["minv"] = q["minv"] + out[:, GROUP:]
                        else:
                            q["minv"] = q["minv"] + _dot(q["pw"].astype(BF16), blk(q["minv"]))
                return run

            def values():
                for key in batch:
                    q = st[key]
                    kv = _dot(q["a_k"], val(q, "vblk"))
                    q["av"] = kv[:CHUNK]
                    q["y_rk"] = kv[CHUNK:]

            def solve():
                for key in batch:
                    q = st[key]
                    wu = _dot(q["minv"].astype(BF16),
                              jnp.concatenate([blk(val(q, "at")), blk(q["av"])], axis=1))
                    q["w"] = wu[:, :GROUP].astype(BF16)
                    q["u0"] = wu[:, GROUP:]

            return [scores, square] + [neumann(j) for j in range(1, 6)] + [values, solve]

        segs = []
        for lo in range(0, len(pairs), RWKV_CHAINS):
            segs.extend(stages(pairs[lo:lo + RWKV_CHAINS]))
        return segs

    def state_back_segments(st, chunk, write_y):
        def apply_state():
            for key in pairs:
                gi, ri = key
                q = st[key]
                q["s"] = s_scr[ri, gi]
                rt = val(q, "ar")[CHUNK:]
                ws = _dot_nt(jnp.concatenate([q["w"], rt], axis=0), q["s"].astype(BF16))
                q["u"] = ws[:CHUNK] + q["u0"]
                q["y_rs"] = ws[CHUNK:]

        def outputs():
            for key in pairs:
                q = st[key]
                q["y"] = q["y_rs"] + _dot(q["a_rb"], blk(q["u"])) + q["y_rk"]

        def update_state():
            for key in pairs:
                gi, ri = key
                q = st[key]
                upd = _dot_tn(jnp.concatenate([q["u"], val(q, "vg")], axis=0).astype(BF16),
                              val(q, "bk_end"))
                s_scr[ri, gi] = q["s"] * val(q, "pc") + upd * bd_f

        g = {}

        def centre():
            g["ys"] = [jnp.concatenate([st[gi, ri]["y"] for ri in range(nc)], axis=0)
                       if nc > 1 else st[gi, 0]["y"] for gi in range(N_GROUPS)]
            means = seg_sums(g["ys"])
            g["ds"] = [g["ys"][i] - means[i] * (1.0 / HEAD_DIM) for i in range(N_GROUPS)]

        def variance():
            g["vars"] = seg_sums([d * d for d in g["ds"]])

        def finish():
            yn = jnp.concatenate(
                [g["ds"][i] * lax.rsqrt(g["vars"][i] * (1.0 / HEAD_DIM) + RWKV_LN_EPS)
                 for i in range(N_GROUPS)], axis=1)
            yn = yn * lnw_ref[...] + lnb_ref[...]
            out = (yn + val(chunk, "bonus") * val(chunk, "v")) * val(chunk, "gate")
            write_y(out.astype(BF16).reshape(nc, CHUNK, RWKV_DIM))

        return [apply_state, outputs, update_state, centre, variance, finish]

    def init_state():
        for ri in range(nc):
            s_scr[ri] = s0_ref[...]

    if pipelined:
        slot = c % 2

        @pl.when(c == 0)
        def _():
            init_state()
            _emit_interleaved(front_segments(p0_ref, {}, 0))

        def write_half(half):
            def write(y):
                y_ref[:, half * CHUNK:(half + 1) * CHUNK, :] = y
            return write

        first, first_chunk = carried(slot)
        second = {}
        _emit_interleaved(intra_segments(first), front_segments(p1_ref, second, None))
        ahead = front_segments(p2_ref, {}, 1 - slot)
        cut = len(ahead) // 2
        _emit_interleaved(intra_segments(second["pairs"]),
                          state_back_segments(first, first_chunk, write_half(0)), ahead[:cut])
        _emit_interleaved(state_back_segments(second["pairs"], second["chunk"], write_half(1)),
                          ahead[cut:])
    else:
        pl.when(c == 0)(init_state)
        only = {}

        def write_all(y):
            y_ref[...] = y

        _emit_interleaved(front_segments(p0_ref, only, None))
        _emit_interleaved(intra_segments(only["pairs"]))
        _emit_interleaved(state_back_segments(only["pairs"], only["chunk"], write_all))

    @pl.when(c == pl.num_programs(1) - 1)
    def _():
        sfin_ref[...] = s_scr[...]


def _rwkv(p, s0, vecs, w2a, g2p, consts, batch, n_chunks, nc, pipelined):
    w0, a0, k_k, k_a, r_k, ln_w, ln_b = vecs
    bd, tri_mask, eye = consts
    tok = nc * CHUNK
    n_pairs = nc * N_GROUPS
    vec = _const_spec((1, RWKV_DIM))
    chunk_spec = lambda index: pl.BlockSpec((nc, CHUNK, RW_W), index)
    if pipelined:
        steps = n_chunks // 2
        p_specs = [chunk_spec(lambda b, c: (b, 0, 0)),
                   chunk_spec(lambda b, c: (b, 2 * c + 1, 0)),
                   chunk_spec(lambda b, c: (b, jnp.minimum(2 * c + 2, n_chunks - 1), 0))]
        p_args = [p, p, p]
        y_rows = 2 * CHUNK
        carry = [pltpu.VMEM((2, n_pairs, 2, CHUNK, GROUP), F32),
                 pltpu.VMEM((2, n_pairs, 8, GROUP), F32),
                 pltpu.VMEM((2, n_pairs, 3, GROUP, GROUP), BF16),
                 pltpu.VMEM((2, n_pairs, 2, 2 * CHUNK, GROUP), BF16),
                 pltpu.VMEM((2, 3, tok, RWKV_DIM), F32)]
    else:
        steps = n_chunks
        p_specs = [chunk_spec(lambda b, c: (b, c, 0))]
        p_args = [p]
        y_rows = CHUNK
        carry = []
    return pl.pallas_call(
        functools.partial(_rwkv_kernel, nc=nc, pipelined=pipelined),
        grid=(batch // nc, steps),
        in_specs=p_specs + [
            _const_spec((N_GROUPS, GROUP, GROUP)),
            vec, vec,
            _const_spec((LANES, 2 * RWKV_DIM)),
            _const_spec((GATE_PAD, RWKV_DIM)),
            vec, vec, vec, vec, vec,
            _const_spec((GROUP, GROUP)),
            _const_spec((2 * CHUNK, GROUP)),
            _const_spec((CHUNK, GROUP)),
        ],
        out_specs=[
            pl.BlockSpec((nc, y_rows, RWKV_DIM), lambda b, c: (b, c, 0)),
            pl.BlockSpec((nc, N_GROUPS, GROUP, GROUP), lambda b, c: (b, 0, 0, 0)),
        ],
        out_shape=[
            jax.ShapeDtypeStruct((batch, n_chunks * CHUNK, RWKV_DIM), BF16),
            jax.ShapeDtypeStruct((batch, N_GROUPS, GROUP, GROUP), F32),
        ],
        scratch_shapes=[pltpu.VMEM((nc, N_GROUPS, GROUP, GROUP), F32)] + carry,
        compiler_params=_params(2),
        name="rwkv",
    )(*p_args, s0, w0, a0, w2a, g2p, k_k, k_a, r_k, ln_w, ln_b, bd, tri_mask, eye)


def _post_kernel(ya_ref, yr_ref, gate_ref, x_ref, wba_ref, wbr_ref, wo_ref, gf_ref,
                 wg_ref, wu_ref, wd_ref, gn_ref, o_ref, *, parts):
    pm = o_ref.shape[0] // parts
    rows = [slice(i * pm, (i + 1) * pm) for i in range(parts)]

    def rms(h, g_ref):
        ms = jnp.mean(h * h, axis=-1, keepdims=True)
        return h * lax.rsqrt(ms + RMS_EPS) * g_ref[...]

    merged = []
    for r in rows:
        gates = gate_ref[r, :].astype(F32)
        merged.append((gates[:, :D_MODEL] * _dot(ya_ref[r, :], wba_ref[...])
                       + gates[:, D_MODEL:] * _dot(yr_ref[r, :], wbr_ref[...])).astype(BF16))
    h1 = [x_ref[r, :] + _dot(m, wo_ref[...]) for r, m in zip(rows, merged)]
    acts = []
    for h in h1:
        f = rms(h, gf_ref).astype(BF16)
        gt = _dot(f, wg_ref[...])
        up = _dot(f, wu_ref[...])
        acts.append((gt * _sigmoid(gt) * up).astype(BF16))
    h2 = [h + _dot(a, wd_ref[...]) for h, a in zip(h1, acts)]
    for r, h in zip(rows, h2):
        o_ref[r, :] = rms(h, gn_ref)


def _post(ya, yr, gates, x2, wba, wbr, wo, gf, wg, wu, wd, gn, tm):
    n = x2.shape[0]
    row = lambda i: (i, 0)
    return pl.pallas_call(
        functools.partial(_post_kernel, parts=POST_PARTS),
        grid=(n // tm,),
        in_specs=[
            pl.BlockSpec((tm, Q_W), row),
            pl.BlockSpec((tm, RWKV_DIM), row),
            pl.BlockSpec((tm, 2 * D_MODEL), row),
            pl.BlockSpec((tm, D_MODEL), row),
            _const_spec((Q_W, D_MODEL)),
            _const_spec((RWKV_DIM, D_MODEL)),
            _const_spec((D_MODEL, D_MODEL)),
            _const_spec((1, D_MODEL)),
            _const_spec((D_MODEL, D_FF)),
            _const_spec((D_MODEL, D_FF)),
            _const_spec((D_FF, D_MODEL)),
            _const_spec((1, D_MODEL)),
        ],
        out_specs=pl.BlockSpec((tm, D_MODEL), row),
        out_shape=jax.ShapeDtypeStruct((n, D_MODEL), F32),
        compiler_params=_params(1),
        name="post",
    )(ya, yr, gates, x2, wba, wbr, wo, gf, wg, wu, wd, gn)


def _rope_tables(first_pos, n):
    half = ROPE_DIM // 2
    f32 = np.float32
    inv_freq = np.power(f32(ROPE_THETA), -np.arange(half, dtype=f32) * f32(2.0 / ROPE_DIM))
    pos = (first_pos + np.arange(n)).astype(f32)
    ang = (pos[:, None] * inv_freq[None, :]).astype(f32)
    cos, sin = np.cos(ang).astype(f32), np.sin(ang).astype(f32)
    one = np.ones((n, HEAD_DIM - ROPE_DIM), f32)
    zero8 = np.zeros((n, half), f32)
    zero48 = np.zeros((n, HEAD_DIM - ROPE_DIM), f32)
    c_head = np.concatenate([cos, cos, one], axis=1)
    s1_head = np.concatenate([zero8, sin, zero48], axis=1)
    s2_head = np.concatenate([-sin, zero8, zero48], axis=1)
    dup = lambda t: jnp.asarray(np.concatenate([t, t], axis=1))
    return dup(c_head), dup(s1_head), dup(s2_head)


def _rwkv_constants():
    t = np.arange(CHUNK)
    hh = np.arange(GROUP) // HEAD_DIM
    bd = (hh[:, None] == hh[None, :]).astype(np.float32)
    s = np.arange(GROUP) % CHUNK
    strict = (s[None, :] < t[:, None]).astype(np.float32)
    incl = (s[None, :] <= t[:, None]).astype(np.float32)
    eye = (s[None, :] == t[:, None]).astype(np.float32)
    return (jnp.asarray(bd, BF16), jnp.asarray(np.concatenate([strict, incl], axis=0)),
            jnp.asarray(eye))


def kernel(x, meta_tokens, norm_mix_g, w_in, b_in, attn_sinks, rwkv_mix, rwkv_w0, rwkv_w2, rwkv_a0, rwkv_a2, rwkv_g2, rwkv_k_k, rwkv_k_a, rwkv_r_k, rwkv_ln_w, rwkv_ln_b, w_br_attn, w_br_rwkv, w_o, norm_ffn_g, w_ffn_gate, w_ffn_up, w_ffn_down, norm_final_g):
    batch, seq, _ = x.shape
    layer = 0
    x2 = x.reshape(batch * seq, D_MODEL)

    w = w_in[layer]
    b = b_in[layer]
    scale = LOG2_E * HEAD_DIM ** -0.5

    def qkv_cols(t):
        return jnp.concatenate([t[..., :Q_W] * scale, t[..., Q_W:ATTN_PROJ]], axis=-1)

    def rw_cols(t, n_lead):
        pad = jnp.zeros(t.shape[:n_lead] + (GATE_PAD - GATE_LORA,), t.dtype)
        return jnp.concatenate([t, pad], axis=-1)

    wq = qkv_cols(w).astype(BF16)
    bq = qkv_cols(b)[None]
    wr = rw_cols(w[:, ATTN_PROJ:ATTN_PROJ + RWKV_PROJ], 1).astype(BF16)
    br = rw_cols(b[ATTN_PROJ:ATTN_PROJ + RWKV_PROJ], 0)[None]
    wg = w[:, ATTN_PROJ + RWKV_PROJ:].astype(BF16)
    bg = b[ATTN_PROJ + RWKV_PROJ:][None]
    g_mix = norm_mix_g[layer][None]

    mix = rw_cols(rwkv_mix[layer], 0)[None]
    zl = jnp.zeros((DECAY_LORA, RWKV_DIM), F32)
    w2a = jnp.concatenate(
        [jnp.concatenate([rwkv_w2[layer], zl], axis=1),
         jnp.concatenate([zl, rwkv_a2[layer]], axis=1)], axis=0).astype(BF16)
    g2p = jnp.concatenate(
        [rwkv_g2[layer], jnp.zeros((GATE_PAD - GATE_LORA, RWKV_DIM), F32)], axis=0).astype(BF16)
    vecs = (rwkv_w0[layer][None], rwkv_a0[layer][None], rwkv_k_k[layer][None],
            rwkv_k_a[layer][None], rwkv_r_k[layer].reshape(1, RWKV_DIM),
            rwkv_ln_w[layer][None], rwkv_ln_b[layer][None])

    cos_m, s1_m, s2_m = _rope_tables(0, N_META)
    qkv_m, p_m, _, p_m_tail = _inproj(meta_tokens, g_mix, wq, wr, wg, bq, br, bg,
                                      cos_m, s1_m, s2_m, jnp.zeros((8, RW_W), F32), mix, N_META)
    p_m_pad = jnp.concatenate([jnp.zeros((CHUNK - N_META, RW_W), F32), p_m], axis=0)
    zero_state = jnp.zeros((N_GROUPS, GROUP, GROUP), F32)
    _, s_meta = _rwkv(p_m_pad[None], zero_state, vecs, w2a, g2p, _rwkv_constants(), 1, 1, 1, False)

    cos, s1, s2 = _rope_tables(N_META, seq)
    tm_in = min(512, seq)
    qkv, p, gates, _ = _inproj(x2, g_mix, wq, wr, wg, bq, br, bg, cos, s1, s2,
                               p_m_tail, mix, tm_in)
    y_attn = _attn(attn_sinks[layer] * LOG2_E, qkv, qkv_m, batch, seq, ATTN_BLOCKS_PER_STEP)
    nc = min(RWKV_ROWS_PER_STEP, batch)
    y_rwkv, _ = _rwkv(p.reshape(batch, seq, RW_W), s_meta[0], vecs, w2a, g2p,
                      _rwkv_constants(), batch, seq // CHUNK, nc, True)
    out = _post(y_attn, y_rwkv.reshape(batch * seq, RWKV_DIM), gates, x2,
                w_br_attn[layer].astype(BF16), w_br_rwkv[layer].astype(BF16),
                w_o[layer].astype(BF16), norm_ffn_g[layer][None],
                w_ffn_gate[layer].astype(BF16), w_ffn_up[layer].astype(BF16),
                w_ffn_down[layer].astype(BF16), norm_final_g[None], min(POST_ROWS, seq))
    return out.reshape(batch, seq, D_MODEL)
```

```python
import functools

import jax
import jax.numpy as jnp
import numpy as np
from jax import lax
from jax.experimental import pallas as pl
from jax.experimental.pallas import tpu as pltpu

F32 = jnp.float32
BF16 = jnp.bfloat16

D_MODEL = 1024
N_META = 16
HEAD_DIM = 64
Q_HEADS = 8
KV_HEADS = 2
WINDOW = 128
ROPE_THETA = 500000.0
ROPE_DIM = HEAD_DIM // 4
RWKV_HEADS = 8
RWKV_DIM = RWKV_HEADS * HEAD_DIM
DECAY_LORA = 64
AAA_LORA = 64
GATE_LORA = 160
RWKV_LN_EPS = 64e-5
D_FF = 2816
Q_W = Q_HEADS * HEAD_DIM
KV_W = KV_HEADS * HEAD_DIM
ATTN_PROJ = Q_W + 2 * KV_W
RWKV_PROJ = 3 * RWKV_DIM + DECAY_LORA + AAA_LORA + GATE_LORA
RMS_EPS = 1e-6
NEG_INF = -1e30
LOG2_E = 1.4426950408889634

LANES = 128
MXU_COLS = 256
QKV_W = Q_W + 4 * KV_W
GATE_PAD = 256
RW_W = 3 * RWKV_DIM + LANES + GATE_PAD
CHUNK = 64
GROUP = 4 * HEAD_DIM
N_GROUPS = RWKV_DIM // GROUP
ATTN_BLOCKS_PER_STEP = 8
RWKV_ROWS_PER_STEP = 4
RWKV_CHUNKS_PER_STEP = 2
RWKV_CHAINS = 4
INPROJ_PARTS = 2
POST_ROWS = 512
POST_PARTS = 2
VMEM_LIMIT = 56 * 1024 * 1024


def _dot(a, b):
    return jnp.dot(a, b, preferred_element_type=F32)


def _dot_nt(a, b):
    return lax.dot_general(a, b, (((1,), (1,)), ((), ())), preferred_element_type=F32)


def _dot_tn(a, b):
    return lax.dot_general(a, b, (((0,), (0,)), ((), ())), preferred_element_type=F32)


def _sigmoid(x):
    return 1.0 / (1.0 + jnp.exp(-x))


def _params(n_grid_dims):
    return pltpu.CompilerParams(
        dimension_semantics=("arbitrary",) * n_grid_dims,
        vmem_limit_bytes=VMEM_LIMIT)


def _const_spec(shape):
    nd = len(shape)
    return pl.BlockSpec(shape, lambda *_: (0,) * nd, pipeline_mode=pl.Buffered(1))


def _inproj_kernel(x_ref, g_ref, wq_ref, wr_ref, wg_ref, bq_ref, br_ref, bg_ref,
                   cos_ref, s1_ref, s2_ref, pprev_ref, mix_ref,
                   qkv_ref, p_ref, gate_ref, plast_ref, pbuf, *, tm, seq_tiles, parts):
    @pl.when(pl.program_id(0) % seq_tiles == 0)
    def _():
        pbuf[0:8, :] = pprev_ref[...]

    pm = tm // parts
    rows = [slice(i * pm, (i + 1) * pm) for i in range(parts)]
    normed = {}

    def u(i):
        if i not in normed:
            x = x_ref[rows[i], :]
            ms = jnp.mean(x * x, axis=-1, keepdims=True)
            normed[i] = (x * lax.rsqrt(ms + RMS_EPS) * g_ref[...]).astype(BF16)
        return normed[i]

    def rope(t, r):
        return (t * cos_ref[r, :] + pltpu.roll(t, 8, 1) * s1_ref[r, :]
                + pltpu.roll(t, LANES - 8, 1) * s2_ref[r, :])

    def put(slab, r, t):
        qkv_ref[r, slab * LANES:(slab + 1) * LANES] = t.astype(BF16)

    q_slabs = Q_W // LANES
    for j in range(Q_W // MXU_COLS):
        sl = slice(j * MXU_COLS, (j + 1) * MXU_COLS)
        for i, r in enumerate(rows):
            t2 = _dot(u(i), wq_ref[:, sl]) + bq_ref[:, sl]
            for h in range(MXU_COLS // LANES):
                put(j * (MXU_COLS // LANES) + h, r, rope(t2[:, h * LANES:(h + 1) * LANES], r))
    for i, r in enumerate(rows):
        kv = _dot(u(i), wq_ref[:, Q_W:]) + bq_ref[:, Q_W:]
        k = rope(kv[:, :LANES], r)
        v = kv[:, LANES:]
        put(q_slabs, r, k)
        put(q_slabs + 1, r, pltpu.roll(k, HEAD_DIM, 1))
        put(q_slabs + 2, r, v)
        put(q_slabs + 3, r, pltpu.roll(v, HEAD_DIM, 1))

    for lo in range(0, RW_W, MXU_COLS):
        sl = slice(lo, min(lo + MXU_COLS, RW_W))
        pcs = [_dot(u(i), wr_ref[:, sl]) + br_ref[:, sl] for i in range(parts)]
        for r, pc in zip(rows, pcs):
            pbuf[8 + r.start:8 + r.stop, sl] = pc
        for r, pc in zip(rows, pcs):
            psh = pbuf[7 + r.start:7 + r.stop, sl]
            p_ref[r, sl] = pc + (psh - pc) * mix_ref[:, sl]
        tail = pcs[-1][pm - 8:pm, :]
        pbuf[0:8, sl] = tail
        plast_ref[:, sl] = tail

    for lo in range(0, 2 * D_MODEL, MXU_COLS):
        sl = slice(lo, lo + MXU_COLS)
        for i, r in enumerate(rows):
            gate_ref[r, sl] = _sigmoid(_dot(u(i), wg_ref[:, sl]) + bg_ref[:, sl]).astype(BF16)


def _inproj(x2, g, wq, wr, wg, bq, br, bg, cos, s1, s2, pprev, mix, tm):
    n = x2.shape[0]
    seq_tiles = cos.shape[0] // tm
    row = lambda i: (i, 0)
    pos = lambda i: (i % seq_tiles, 0)
    return pl.pallas_call(
        functools.partial(_inproj_kernel, tm=tm, seq_tiles=seq_tiles,
                          parts=INPROJ_PARTS if tm % (INPROJ_PARTS * 128) == 0 else 1),
        grid=(n // tm,),
        in_specs=[
            pl.BlockSpec((tm, D_MODEL), row),
            _const_spec((1, D_MODEL)),
            _const_spec((D_MODEL, ATTN_PROJ)),
            _const_spec((D_MODEL, RW_W)),
            _const_spec((D_MODEL, 2 * D_MODEL)),
            _const_spec((1, ATTN_PROJ)),
            _const_spec((1, RW_W)),
            _const_spec((1, 2 * D_MODEL)),
            pl.BlockSpec((tm, LANES), pos),
            pl.BlockSpec((tm, LANES), pos),
            pl.BlockSpec((tm, LANES), pos),
            _const_spec((8, RW_W)),
            _const_spec((1, RW_W)),
        ],
        out_specs=[
            pl.BlockSpec((tm, QKV_W), row),
            pl.BlockSpec((tm, RW_W), row),
            pl.BlockSpec((tm, 2 * D_MODEL), row),
            pl.BlockSpec((8, RW_W), lambda i: (0, 0)),
        ],
        out_shape=[
            jax.ShapeDtypeStruct((n, QKV_W), BF16),
            jax.ShapeDtypeStruct((n, RW_W), F32),
            jax.ShapeDtypeStruct((n, 2 * D_MODEL), BF16),
            jax.ShapeDtypeStruct((8, RW_W), F32),
        ],
        scratch_shapes=[pltpu.VMEM((8 + tm, RW_W), F32)],
        compiler_params=_params(1),
        name="inproj",
    )(x2, g, wq, wr, wg, bq, br, bg, cos, s1, s2, pprev, mix)


def _attn_kernel(sink_ref, q_ref, kvc_ref, kvp_ref, kvm_ref, band_ref, o_ref, *, qb):
    n = pl.program_id(1)
    blk = WINDOW
    lane = lax.broadcasted_iota(jnp.int32, (1, LANES), 1)
    m_lo = (lane < HEAD_DIM).astype(BF16)
    m_hi = (lane >= HEAD_DIM).astype(BF16)
    lo_half = lane < HEAD_DIM

    kv_all = jnp.concatenate([kvp_ref[...], kvc_ref[...]], axis=0)
    kv_meta = kvm_ref[...]

    band_mask = band_ref[...]
    col = lax.broadcasted_iota(jnp.int32, (1, 2 * blk), 1)
    no_prev = (1 - jnp.minimum(n, 1)).astype(F32)
    first_mask = band_mask + jnp.where(col < blk, NEG_INF, 0.0) * no_prev
    first = lax.broadcasted_iota(jnp.int32, (2 * blk, 2 * N_META), 1) < N_META
    top = lax.broadcasted_iota(jnp.int32, (2 * blk, 1), 0) < blk

    def halves(kv, g):
        k_plain, k_swap = kv[:, 0:LANES], kv[:, LANES:2 * LANES]
        v_plain, v_swap = kv[:, 2 * LANES:3 * LANES], kv[:, 3 * LANES:4 * LANES]
        if g == 0:
            k_lo, k_hi, v_lo, v_hi = k_plain, k_swap, v_plain, v_swap
        else:
            k_lo, k_hi, v_lo, v_hi = k_swap, k_plain, v_swap, v_plain
        return k_lo * m_lo, k_hi * m_hi, v_lo * m_lo, v_hi * m_hi

    band = [halves(kv_all, g) for g in range(KV_HEADS)]
    meta = [tuple(jnp.concatenate(pair, axis=0) for pair in
                  ((h[0], h[1]), (h[2], h[3])))
            for h in (halves(kv_meta, g) for g in range(KV_HEADS))]

    units = [(j, g) for j in range(qb) for g in range(KV_HEADS)]
    st = {}
    for key in units:
        j, g = key
        rows = slice(j * blk, (j + 2) * blk)
        k_lo, k_hi, v_lo, v_hi = band[g]
        kb = jnp.concatenate([k_lo[rows], k_hi[rows]], axis=0)
        vb = jnp.concatenate([v_lo[rows], v_hi[rows]], axis=0)
        q = jnp.concatenate(
            [q_ref[j * blk:(j + 1) * blk, s * LANES:(s + 1) * LANES] for s in (2 * g, 2 * g + 1)],
            axis=0)
        st[key] = dict(vb=vb,
                       sb=_dot_nt(q, kb),
                       sm=_dot_nt(q, meta[g][0]))

    for key in units:
        j, g = key
        u = st[key]
        mask = first_mask if j == 0 else band_mask
        sink_a = jnp.where(top, sink_ref[4 * g], sink_ref[4 * g + 2])
        sink_b = jnp.where(top, sink_ref[4 * g + 1], sink_ref[4 * g + 3])
        sb, sm = u["sb"], u["sm"]
        sa = sb[:, :2 * blk] + mask
        sbb = sb[:, 2 * blk:] + mask
        mx_a = jnp.maximum(
            jnp.maximum(jnp.max(sa, axis=-1, keepdims=True),
                        jnp.max(jnp.where(first, sm, NEG_INF), axis=-1, keepdims=True)),
            sink_a)
        mx_b = jnp.maximum(
            jnp.maximum(jnp.max(sbb, axis=-1, keepdims=True),
                        jnp.max(jnp.where(first, NEG_INF, sm), axis=-1, keepdims=True)),
            sink_b)
        pa = jnp.exp2(sa - mx_a)
        pb = jnp.exp2(sbb - mx_b)
        pm = jnp.exp2(sm - jnp.where(first, mx_a, mx_b))
        den_a = (jnp.sum(pa, axis=-1, keepdims=True)
                 + jnp.sum(jnp.where(first, pm, 0.0), axis=-1, keepdims=True)
                 + jnp.exp2(sink_a - mx_a))
        den_b = (jnp.sum(pb, axis=-1, keepdims=True)
                 + jnp.sum(jnp.where(first, 0.0, pm), axis=-1, keepdims=True)
                 + jnp.exp2(sink_b - mx_b))
        u["inv"] = jnp.where(lo_half, 1.0 / den_a, 1.0 / den_b)
        u["p_band"] = jnp.concatenate([pa, pb], axis=1).astype(BF16)
        u["p_meta"] = pm.astype(BF16)

    for key in units:
        j, g = key
        u = st[key]
        o = ((_dot(u["p_band"], u["vb"]) + _dot(u["p_meta"], meta[g][1])) * u["inv"]).astype(BF16)
        for i, s in enumerate((2 * g, 2 * g + 1)):
            o_ref[j * blk:(j + 1) * blk, s * LANES:(s + 1) * LANES] = o[i * blk:(i + 1) * blk]


def _attn(sinks, qkv, qkv_meta, batch, seq, qb):
    nb = seq // WINDOW
    steps = nb // qb
    n = batch * seq
    kvw = 4 * KV_W
    rows = qb * WINDOW
    qi = np.arange(2 * WINDOW)[:, None] % WINDOW
    c = np.arange(2 * WINDOW)[None, :]
    band_ok = np.where(c < WINDOW, c > qi, c - WINDOW <= qi)
    band_mask = jnp.asarray(np.where(band_ok, 0.0, NEG_INF).astype(np.float32))
    return pl.pallas_call(
        functools.partial(_attn_kernel, qb=qb),
        grid=(batch, steps),
        in_specs=[
            pl.BlockSpec(memory_space=pltpu.SMEM),
            pl.BlockSpec((rows, Q_W), lambda b, i: (b * steps + i, 0)),
            pl.BlockSpec((rows, kvw), lambda b, i: (b * steps + i, 1)),
            pl.BlockSpec((WINDOW, kvw), lambda b, i: (jnp.maximum((b * steps + i) * qb - 1, 0), 1)),
            pl.BlockSpec((N_META, kvw), lambda b, i: (0, 1)),
            _const_spec((2 * WINDOW, 2 * WINDOW)),
        ],
        out_specs=pl.BlockSpec((rows, Q_W), lambda b, i: (b * steps + i, 0)),
        out_shape=jax.ShapeDtypeStruct((n, Q_W), BF16),
        compiler_params=_params(2),
        name="attn",
    )(sinks, qkv, qkv, qkv, qkv_meta, band_mask)


def _emit_interleaved(*segment_lists):
    keyed = []
    for li, segs in enumerate(segment_lists):
        costs = [_SEGMENT_COST.get(seg.__name__, 1.0) for seg in segs]
        total, done = sum(costs), 0.0
        for j, (seg, cost) in enumerate(zip(segs, costs)):
            keyed.append(((done + 0.5 * cost) / total, li, j, seg))
            done += cost
    for _, _, _, seg in sorted(keyed, key=lambda t: t[:3]):
        seg()


_SEGMENT_COST = dict(
    lora=5, decay_gate=8, norms=5, cumulative=9, pair_prep=2.5,
    scores=6, square=3, neumann=6, values=3, solve=6,
    apply_state=5, outputs=4, update_state=7, centre=3, variance=2, finish=4)


def _rwkv_kernel(*refs, nc, pipelined):
    if pipelined:
        p0_ref, later_p_refs, next_p_ref = refs[0], refs[1:pipelined], refs[pipelined]
        (s0_ref, w0_ref, a0_ref, w2a_ref, g2_ref, kk_ref, ka_ref, rk_ref,
         lnw_ref, lnb_ref, bd_ref, tri_ref, eye_ref,
         y_ref, sfin_ref, s_scr, c_f32, c_pc, c_blk, c_rows, c_chunk) = refs[pipelined + 1:]
    else:
        (p0_ref, s0_ref, w0_ref, a0_ref, w2a_ref, g2_ref, kk_ref, ka_ref, rk_ref,
         lnw_ref, lnb_ref, bd_ref, tri_ref, eye_ref,
         y_ref, sfin_ref, s_scr) = refs
    c = pl.program_id(1)
    tok = nc * CHUNK
    o1, o2, o3 = RWKV_DIM, 2 * RWKV_DIM, 3 * RWKV_DIM
    pairs = [(gi, ri) for ri in range(nc) for gi in range(N_GROUPS)]

    bd = bd_ref[...]
    bd_f = bd.astype(F32)
    tri_mask = tri_ref[...]
    eye = eye_ref[...]
    lane = lax.broadcasted_iota(jnp.int32, (tok, LANES), 1)
    lane1 = lax.broadcasted_iota(jnp.int32, (1, LANES), 1)
    head_lo = (lane1 < HEAD_DIM).astype(BF16)
    head_hi = (lane1 >= HEAD_DIM).astype(BF16)
    zero_slab = jnp.zeros((CHUNK, LANES), BF16)

    def blk(x):
        xb = x.astype(BF16)
        left, right = xb[:, :LANES], xb[:, LANES:]
        return jnp.concatenate(
            [jnp.concatenate([left * head_lo, zero_slab], axis=1),
             jnp.concatenate([left * head_hi, zero_slab], axis=1),
             jnp.concatenate([zero_slab, right * head_lo], axis=1),
             jnp.concatenate([zero_slab, right * head_hi], axis=1)], axis=0)

    pos_in_chunk = lax.broadcasted_iota(jnp.int32, (tok, 1), 0) % CHUNK

    def seg_sums(xs):
        out = _dot(jnp.concatenate([x.astype(BF16) for x in xs], axis=0), bd)
        return [out[i * tok:(i + 1) * tok] for i in range(len(xs))]

    def grp(x, gi):
        return x[:, gi * GROUP:(gi + 1) * GROUP]

    def val(q, name):
        item = q[name]
        return item() if callable(item) else item

    def front_segments(p_ref, out, to_carry):
        f = {}

        def lora():
            pf = p_ref[...].reshape(tok, RW_W)
            f["r"], f["k"], f["v"] = pf[:, :o1], pf[:, o1:o2], pf[:, o2:o3]
            dwa = pf[:, o3:o3 + LANES]
            f["dg"] = pf[:, o3 + LANES:]
            z = jnp.where(lane < DECAY_LORA, jnp.tanh(dwa), dwa).astype(BF16)
            f["wa"] = _dot(z, w2a_ref[...])

        def decay_gate():
            zw = -(w0_ref[...] + f["wa"][:, :o1])
            softplus = jnp.maximum(zw, 0.0) + jnp.log(1.0 + jnp.exp(-jnp.abs(zw)))
            f["logw"] = -jnp.exp(-softplus - 0.5)
            f["a"] = _sigmoid(a0_ref[...] + f["wa"][:, o1:])
            f["gate"] = _dot(_sigmoid(f["dg"]).astype(BF16), g2_ref[...])

        def norms():
            f["kkv"] = f["k"] * kk_ref[...]
            f["kp"] = f["k"] * (1.0 + (f["a"] - 1.0) * ka_ref[...])
            rkk = f["r"] * f["kp"] * rk_ref[...]
            sq = f["kkv"] * f["kkv"]
            f["sums"] = seg_sums([grp(sq, 0), grp(sq, 1), grp(rkk, 0), grp(rkk, 1)])

        def cumulative():
            ss = jnp.concatenate(f["sums"][0:2], axis=1)
            f["bonus"] = jnp.concatenate(f["sums"][2:4], axis=1)
            kkn = f["kkv"] / jnp.maximum(jnp.sqrt(ss), 1e-12)
            f["aa"] = -kkn
            f["bb"] = kkn * f["a"]
            lp = f["logw"]
            step = 1
            while step < CHUNK:
                lp = lp + jnp.where(pos_in_chunk >= step, pltpu.roll(lp, step, 0), 0.0)
                step *= 2
            f["lps"] = [grp(lp, gi) for gi in range(N_GROUPS)]
            chunk = dict(gate=f["gate"], bonus=f["bonus"], v=f["v"])
            if to_carry is not None:
                c_chunk[to_carry, 0] = chunk["gate"]
                c_chunk[to_carry, 1] = chunk["bonus"]
                c_chunk[to_carry, 2] = chunk["v"]
            out["chunk"] = chunk
            out["pairs"] = {}

        def pair_prep(i, key):
            def run():
                gi, ri = key
                rows = slice(ri * CHUNK, (ri + 1) * CHUNK)
                cut = lambda x: x[rows, gi * GROUP:(gi + 1) * GROUP]
                lp = f["lps"][gi][rows]
                lw = cut(f["logw"])
                lpc = lp[CHUNK - 1:CHUNK, :]
                e_neg = jnp.exp(-lp)
                e_end = jnp.exp(lpc - lp)
                rg, kg, vg = cut(f["r"]), cut(f["kp"]), cut(f["v"])
                ag, bg = cut(f["aa"]), cut(f["bb"])
                at = ag * jnp.exp(lp - lw)
                rt = rg * jnp.exp(lp)
                q = dict(
                    at=at, vg=vg, pc=jnp.exp(lpc), vblk=blk(vg),
                    bt=blk(bg * e_neg), kt=blk(kg * e_neg),
                    ar=jnp.concatenate([at, rt], axis=0).astype(BF16),
                    bk_end=jnp.concatenate([bg * e_end, kg * e_end], axis=0).astype(BF16))
                if to_carry is not None:
                    c_f32[to_carry, i, 0], c_f32[to_carry, i, 1] = q["at"], q["vg"]
                    c_pc[to_carry, i, 0:1, :] = q["pc"]
                    c_blk[to_carry, i, 0], c_blk[to_carry, i, 1] = q["bt"], q["kt"]
                    c_blk[to_carry, i, 2] = q["vblk"]
                    c_rows[to_carry, i, 0], c_rows[to_carry, i, 1] = q["ar"], q["bk_end"]
                out["pairs"][key] = q
            run.__name__ = "pair_prep"
            return run

        return [lora, decay_gate, norms, cumulative] + [pair_prep(i, k) for i, k in enumerate(pairs)]

    def carried(slot):
        def pair(i):
            return dict(
                at=lambda: c_f32[slot, i, 0], vg=lambda: c_f32[slot, i, 1],
                pc=lambda: c_pc[slot, i, 0:1, :],
                bt=lambda: c_blk[slot, i, 0], kt=lambda: c_blk[slot, i, 1],
                vblk=lambda: c_blk[slot, i, 2],
                ar=lambda: c_rows[slot, i, 0], bk_end=lambda: c_rows[slot, i, 1])
        chunk = dict(gate=lambda: c_chunk[slot, 0], bonus=lambda: c_chunk[slot, 1],
                     v=lambda: c_chunk[slot, 2])
        return {key: pair(i) for i, key in enumerate(pairs)}, chunk

    def intra_segments(st):
        def stages(batch):
            def scores():
                for key in batch:
                    q = st[key]
                    ar = val(q, "ar")
                    sb = _dot_nt(ar, val(q, "bt")) * tri_mask
                    sk = _dot_nt(ar, val(q, "kt")) * tri_mask
                    q["a_ab"] = sb[:CHUNK]
                    q["a_rb"] = sb[CHUNK:].astype(BF16)
                    q["a_k"] = sk.astype(BF16)

            def square():
                for key in batch:
                    q = st[key]
                    q["pw"] = _dot(q["a_ab"].astype(BF16), blk(q["a_ab"]))
                    q["minv"] = eye + q["a_ab"]

            def neumann(j):
                def run():
                    for key in batch:
                        q = st[key]
                        if j < 5:
                            out = _dot(q["pw"].astype(BF16),
                                       jnp.concatenate([blk(q["pw"]), blk(q["minv"])], axis=1))
                            q["pw"] = out[:, :GROUP]
                            q["minv"] = q["minv"] + out[:, GROUP:]
                        else:
                            q["minv"] = q["minv"] + _dot(q["pw"].astype(BF16), blk(q["minv"]))
                run.__name__ = "neumann"
                return run

            def values():
                for key in batch:
                    q = st[key]
                    kv = _dot(q["a_k"], val(q, "vblk"))
                    q["av"] = kv[:CHUNK]
                    q["y_rk"] = kv[CHUNK:]

            def solve():
                for key in batch:
                    q = st[key]
                    wu = _dot(q["minv"].astype(BF16),
                              jnp.concatenate([blk(val(q, "at")), blk(q["av"])], axis=1))
                    q["w"] = wu[:, :GROUP].astype(BF16)
                    q["u0"] = wu[:, GROUP:]

            return [scores, square] + [neumann(j) for j in range(1, 6)] + [values, solve]

        segs = []
        for lo in range(0, len(pairs), RWKV_CHAINS):
            segs.extend(stages(pairs[lo:lo + RWKV_CHAINS]))
        return segs

    def state_back_segments(st, chunk, write_y):
        def apply_state():
            for key in pairs:
                gi, ri = key
                q = st[key]
                q["s"] = s_scr[ri, gi]
                rt = val(q, "ar")[CHUNK:]
                ws = _dot_nt(jnp.concatenate([q["w"], rt], axis=0), q["s"].astype(BF16))
                q["u"] = ws[:CHUNK] + q["u0"]
                q["y_rs"] = ws[CHUNK:]

        def outputs():
            for key in pairs:
                q = st[key]
                q["y"] = q["y_rs"] + _dot(q["a_rb"], blk(q["u"])) + q["y_rk"]

        def update_state():
            for key in pairs:
                gi, ri = key
                q = st[key]
                upd = _dot_tn(jnp.concatenate([q["u"], val(q, "vg")], axis=0).astype(BF16),
                              val(q, "bk_end"))
                s_scr[ri, gi] = q["s"] * val(q, "pc") + upd * bd_f

        g = {}

        def centre():
            g["ys"] = [jnp.concatenate([st[gi, ri]["y"] for ri in range(nc)], axis=0)
                       if nc > 1 else st[gi, 0]["y"] for gi in range(N_GROUPS)]
            means = seg_sums(g["ys"])
            g["ds"] = [g["ys"][i] - means[i] * (1.0 / HEAD_DIM) for i in range(N_GROUPS)]

        def variance():
            g["vars"] = seg_sums([d * d for d in g["ds"]])

        def finish():
            yn = jnp.concatenate(
                [g["ds"][i] * lax.rsqrt(g["vars"][i] * (1.0 / HEAD_DIM) + RWKV_LN_EPS)
                 for i in range(N_GROUPS)], axis=1)
            yn = yn * lnw_ref[...] + lnb_ref[...]
            out = (yn + val(chunk, "bonus") * val(chunk, "v")) * val(chunk, "gate")
            write_y(out.astype(BF16).reshape(nc, CHUNK, RWKV_DIM))

        return [apply_state, outputs, update_state, centre, variance, finish]

    def init_state():
        for ri in range(nc):
            s_scr[ri] = s0_ref[...]

    if pipelined:
        slot = c % 2

        @pl.when(c == 0)
        def _():
            init_state()
            _emit_interleaved(front_segments(p0_ref, {}, 0))

        def write_part(k):
            def write(y):
                y_ref[:, k * CHUNK:(k + 1) * CHUNK, :] = y
            return write

        cur, cur_chunk = carried(slot)
        done = None
        for k, p_ref in enumerate(later_p_refs):
            nxt = {}
            lists = [intra_segments(cur), front_segments(p_ref, nxt, None)]
            if done is not None:
                lists.append(state_back_segments(*done, write_part(k - 1)))
            _emit_interleaved(*lists)
            done = (cur, cur_chunk)
            cur, cur_chunk = nxt["pairs"], nxt["chunk"]
        last = len(later_p_refs)
        _emit_interleaved(intra_segments(cur),
                          state_back_segments(*done, write_part(last - 1)),
                          front_segments(next_p_ref, {}, 1 - slot))
        _emit_interleaved(state_back_segments(cur, cur_chunk, write_part(last)))
    else:
        pl.when(c == 0)(init_state)
        only = {}

        def write_all(y):
            y_ref[...] = y

        _emit_interleaved(front_segments(p0_ref, only, None))
        _emit_interleaved(intra_segments(only["pairs"]))
        _emit_interleaved(state_back_segments(only["pairs"], only["chunk"], write_all))

    @pl.when(c == pl.num_programs(1) - 1)
    def _():
        sfin_ref[...] = s_scr[...]


def _rwkv(p, s0, vecs, w2a, g2p, consts, batch, n_chunks, nc, pipelined):
    w0, a0, k_k, k_a, r_k, ln_w, ln_b = vecs
    bd, tri_mask, eye = consts
    tok = nc * CHUNK
    n_pairs = nc * N_GROUPS
    vec = _const_spec((1, RWKV_DIM))
    chunk_spec = lambda index: pl.BlockSpec((nc, CHUNK, RW_W), index)
    if pipelined:
        per_step = pipelined
        steps = n_chunks // per_step
        p_specs = ([pl.BlockSpec((nc, CHUNK, RW_W), lambda b, c: (b, 0, 0),
                                 pipeline_mode=pl.Buffered(1))]
                   + [chunk_spec(functools.partial(lambda b, c, k: (b, per_step * c + k, 0), k=k))
                      for k in range(1, per_step)]
                   + [chunk_spec(lambda b, c: (b, jnp.minimum(per_step * (c + 1), n_chunks - 1), 0))])
        p_args = [p] * (per_step + 1)
        y_rows = per_step * CHUNK
        carry = [pltpu.VMEM((2, n_pairs, 2, CHUNK, GROUP), F32),
                 pltpu.VMEM((2, n_pairs, 8, GROUP), F32),
                 pltpu.VMEM((2, n_pairs, 3, GROUP, GROUP), BF16),
                 pltpu.VMEM((2, n_pairs, 2, 2 * CHUNK, GROUP), BF16),
                 pltpu.VMEM((2, 3, tok, RWKV_DIM), F32)]
    else:
        steps = n_chunks
        p_specs = [chunk_spec(lambda b, c: (b, c, 0))]
        p_args = [p]
        y_rows = CHUNK
        carry = []
    return pl.pallas_call(
        functools.partial(_rwkv_kernel, nc=nc, pipelined=pipelined),
        grid=(batch // nc, steps),
        in_specs=p_specs + [
            _const_spec((N_GROUPS, GROUP, GROUP)),
            vec, vec,
            _const_spec((LANES, 2 * RWKV_DIM)),
            _const_spec((GATE_PAD, RWKV_DIM)),
            vec, vec, vec, vec, vec,
            _const_spec((GROUP, GROUP)),
            _const_spec((2 * CHUNK, GROUP)),
            _const_spec((CHUNK, GROUP)),
        ],
        out_specs=[
            pl.BlockSpec((nc, y_rows, RWKV_DIM), lambda b, c: (b, c, 0)),
            pl.BlockSpec((nc, N_GROUPS, GROUP, GROUP), lambda b, c: (b, 0, 0, 0),
                         pipeline_mode=pl.Buffered(1)),
        ],
        out_shape=[
            jax.ShapeDtypeStruct((batch, n_chunks * CHUNK, RWKV_DIM), BF16),
            jax.ShapeDtypeStruct((batch, N_GROUPS, GROUP, GROUP), F32),
        ],
        scratch_shapes=[pltpu.VMEM((nc, N_GROUPS, GROUP, GROUP), F32)] + carry,
        compiler_params=_params(2),
        name="rwkv",
    )(*p_args, s0, w0, a0, w2a, g2p, k_k, k_a, r_k, ln_w, ln_b, bd, tri_mask, eye)


def _post_kernel(ya_ref, yr_ref, gate_ref, x_ref, wba_ref, wbr_ref, wo_ref, gf_ref,
                 wg_ref, wu_ref, wd_ref, gn_ref, o_ref, *, parts):
    pm = o_ref.shape[0] // parts
    rows = [slice(i * pm, (i + 1) * pm) for i in range(parts)]

    def rms(h, g_ref):
        ms = jnp.mean(h * h, axis=-1, keepdims=True)
        return h * lax.rsqrt(ms + RMS_EPS) * g_ref[...]

    merged = []
    for r in rows:
        gates = gate_ref[r, :].astype(F32)
        merged.append((gates[:, :D_MODEL] * _dot(ya_ref[r, :], wba_ref[...])
                       + gates[:, D_MODEL:] * _dot(yr_ref[r, :], wbr_ref[...])).astype(BF16))
    h1 = [x_ref[r, :] + _dot(m, wo_ref[...]) for r, m in zip(rows, merged)]
    acts = []
    for h in h1:
        f = rms(h, gf_ref).astype(BF16)
        gt = _dot(f, wg_ref[...])
        up = _dot(f, wu_ref[...])
        acts.append((gt * _sigmoid(gt) * up).astype(BF16))
    h2 = [h + _dot(a, wd_ref[...]) for h, a in zip(h1, acts)]
    for r, h in zip(rows, h2):
        o_ref[r, :] = rms(h, gn_ref)


def _post(ya, yr, gates, x2, wba, wbr, wo, gf, wg, wu, wd, gn, tm):
    n = x2.shape[0]
    row = lambda i: (i, 0)
    return pl.pallas_call(
        functools.partial(_post_kernel, parts=POST_PARTS),
        grid=(n // tm,),
        in_specs=[
            pl.BlockSpec((tm, Q_W), row),
            pl.BlockSpec((tm, RWKV_DIM), row),
            pl.BlockSpec((tm, 2 * D_MODEL), row),
            pl.BlockSpec((tm, D_MODEL), row),
            _const_spec((Q_W, D_MODEL)),
            _const_spec((RWKV_DIM, D_MODEL)),
            _const_spec((D_MODEL, D_MODEL)),
            _const_spec((1, D_MODEL)),
            _const_spec((D_MODEL, D_FF)),
            _const_spec((D_MODEL, D_FF)),
            _const_spec((D_FF, D_MODEL)),
            _const_spec((1, D_MODEL)),
        ],
        out_specs=pl.BlockSpec((tm, D_MODEL), row),
        out_shape=jax.ShapeDtypeStruct((n, D_MODEL), F32),
        compiler_params=_params(1),
        name="post",
    )(ya, yr, gates, x2, wba, wbr, wo, gf, wg, wu, wd, gn)


def _rope_tables(first_pos, n):
    half = ROPE_DIM // 2
    f32 = np.float32
    inv_freq = np.power(f32(ROPE_THETA), -np.arange(half, dtype=f32) * f32(2.0 / ROPE_DIM))
    pos = (first_pos + np.arange(n)).astype(f32)
    ang = (pos[:, None] * inv_freq[None, :]).astype(f32)
    cos, sin = np.cos(ang).astype(f32), np.sin(ang).astype(f32)
    one = np.ones((n, HEAD_DIM - ROPE_DIM), f32)
    zero8 = np.zeros((n, half), f32)
    zero48 = np.zeros((n, HEAD_DIM - ROPE_DIM), f32)
    c_head = np.concatenate([cos, cos, one], axis=1)
    s1_head = np.concatenate([zero8, sin, zero48], axis=1)
    s2_head = np.concatenate([-sin, zero8, zero48], axis=1)
    dup = lambda t: jnp.asarray(np.concatenate([t, t], axis=1))
    return dup(c_head), dup(s1_head), dup(s2_head)


def _rwkv_constants():
    t = np.arange(CHUNK)
    hh = np.arange(GROUP) // HEAD_DIM
    bd = (hh[:, None] == hh[None, :]).astype(np.float32)
    s = np.arange(GROUP) % CHUNK
    strict = (s[None, :] < t[:, None]).astype(np.float32)
    incl = (s[None, :] <= t[:, None]).astype(np.float32)
    eye = (s[None, :] == t[:, None]).astype(np.float32)
    return (jnp.asarray(bd, BF16), jnp.asarray(np.concatenate([strict, incl], axis=0)),
            jnp.asarray(eye))


def kernel(x, meta_tokens, norm_mix_g, w_in, b_in, attn_sinks, rwkv_mix, rwkv_w0, rwkv_w2, rwkv_a0, rwkv_a2, rwkv_g2, rwkv_k_k, rwkv_k_a, rwkv_r_k, rwkv_ln_w, rwkv_ln_b, w_br_attn, w_br_rwkv, w_o, norm_ffn_g, w_ffn_gate, w_ffn_up, w_ffn_down, norm_final_g):
    batch, seq, _ = x.shape
    layer = 0
    x2 = x.reshape(batch * seq, D_MODEL)

    w = w_in[layer]
    b = b_in[layer]
    scale = LOG2_E * HEAD_DIM ** -0.5

    col_scale = jnp.where(jnp.arange(w.shape[-1]) < Q_W, scale, 1.0).astype(F32)

    def rw_cols(t, n_lead):
        pad = jnp.zeros(t.shape[:n_lead] + (GATE_PAD - GATE_LORA,), t.dtype)
        return jnp.concatenate([t, pad], axis=-1)

    wb = (w * col_scale).astype(BF16)
    b = b * col_scale
    wq = wb[:, :ATTN_PROJ]
    bq = b[:ATTN_PROJ][None]
    wr = rw_cols(wb[:, ATTN_PROJ:ATTN_PROJ + RWKV_PROJ], 1)
    br = rw_cols(b[ATTN_PROJ:ATTN_PROJ + RWKV_PROJ], 0)[None]
    wg = wb[:, ATTN_PROJ + RWKV_PROJ:]
    bg = b[ATTN_PROJ + RWKV_PROJ:][None]
    g_mix = norm_mix_g[layer][None]

    mix = rw_cols(rwkv_mix[layer], 0)[None]
    zl = jnp.zeros((DECAY_LORA, RWKV_DIM), F32)
    w2a = jnp.concatenate(
        [jnp.concatenate([rwkv_w2[layer], zl], axis=1),
         jnp.concatenate([zl, rwkv_a2[layer]], axis=1)], axis=0).astype(BF16)
    g2p = jnp.concatenate(
        [rwkv_g2[layer], jnp.zeros((GATE_PAD - GATE_LORA, RWKV_DIM), F32)], axis=0).astype(BF16)
    vecs = (rwkv_w0[layer][None], rwkv_a0[layer][None], rwkv_k_k[layer][None],
            rwkv_k_a[layer][None], rwkv_r_k[layer].reshape(1, RWKV_DIM),
            rwkv_ln_w[layer][None], rwkv_ln_b[layer][None])

    cos_m, s1_m, s2_m = _rope_tables(0, N_META)
    qkv_m, p_m, _, p_m_tail = _inproj(meta_tokens, g_mix, wq, wr, wg, bq, br, bg,
                                      cos_m, s1_m, s2_m, jnp.zeros((8, RW_W), F32), mix, N_META)
    p_m_pad = jnp.concatenate([jnp.zeros((CHUNK - N_META, RW_W), F32), p_m], axis=0)
    zero_state = jnp.zeros((N_GROUPS, GROUP, GROUP), F32)
    _, s_meta = _rwkv(p_m_pad[None], zero_state, vecs, w2a, g2p, _rwkv_constants(), 1, 1, 1, 0)

    cos, s1, s2 = _rope_tables(N_META, seq)
    tm_in = min(512, seq)
    qkv, p, gates, _ = _inproj(x2, g_mix, wq, wr, wg, bq, br, bg, cos, s1, s2,
                               p_m_tail, mix, tm_in)
    y_attn = _attn(attn_sinks[layer] * LOG2_E, qkv, qkv_m, batch, seq, ATTN_BLOCKS_PER_STEP)
    nc = min(RWKV_ROWS_PER_STEP, batch)
    y_rwkv, _ = _rwkv(p.reshape(batch, seq, RW_W), s_meta[0], vecs, w2a, g2p,
                      _rwkv_constants(), batch, seq // CHUNK, nc, RWKV_CHUNKS_PER_STEP)
    out = _post(y_attn, y_rwkv.reshape(batch * seq, RWKV_DIM), gates, x2,
                w_br_attn[layer].astype(BF16), w_br_rwkv[layer].astype(BF16),
                w_o[layer].astype(BF16), norm_ffn_g[layer][None],
                w_ffn_gate[layer].astype(BF16), w_ffn_up[layer].astype(BF16),
                w_ffn_down[layer].astype(BF16), norm_final_g[None], min(POST_ROWS, seq))
    return out.reshape(batch, seq, D_MODEL)
```

```python
import functools

import jax
import jax.numpy as jnp
import numpy as np
from jax import lax
from jax.experimental import pallas as pl
from jax.experimental.pallas import tpu as pltpu

F32 = jnp.float32
BF16 = jnp.bfloat16

D_MODEL = 1024
N_META = 16
HEAD_DIM = 64
Q_HEADS = 8
KV_HEADS = 2
WINDOW = 128
ROPE_THETA = 500000.0
ROPE_DIM = HEAD_DIM // 4
RWKV_HEADS = 8
RWKV_DIM = RWKV_HEADS * HEAD_DIM
DECAY_LORA = 64
AAA_LORA = 64
GATE_LORA = 160
RWKV_LN_EPS = 64e-5
D_FF = 2816
Q_W = Q_HEADS * HEAD_DIM
KV_W = KV_HEADS * HEAD_DIM
ATTN_PROJ = Q_W + 2 * KV_W
RWKV_PROJ = 3 * RWKV_DIM + DECAY_LORA + AAA_LORA + GATE_LORA
RMS_EPS = 1e-6
NEG_INF = -1e30
LOG2_E = 1.4426950408889634

LANES = 128
MXU_COLS = 256
QKV_W = Q_W + 4 * KV_W
GATE_PAD = 256
RW_W = 3 * RWKV_DIM + LANES + GATE_PAD
CHUNK = 64
GROUP = 4 * HEAD_DIM
N_GROUPS = RWKV_DIM // GROUP
ATTN_BLOCKS_PER_STEP = 8
RWKV_ROWS_PER_STEP = 4
RWKV_CHUNKS_PER_STEP = 2
RWKV_CHAINS = 8
INPROJ_PARTS = 2
POST_ROWS = 512
POST_PARTS = 2
VMEM_LIMIT = 56 * 1024 * 1024


def _dot(a, b):
    return jnp.dot(a, b, preferred_element_type=F32)


def _dot_nt(a, b):
    return lax.dot_general(a, b, (((1,), (1,)), ((), ())), preferred_element_type=F32)


def _dot_tn(a, b):
    return lax.dot_general(a, b, (((0,), (0,)), ((), ())), preferred_element_type=F32)


def _sigmoid(x):
    return 1.0 / (1.0 + jnp.exp(-x))


def _params(n_grid_dims):
    return pltpu.CompilerParams(
        dimension_semantics=("arbitrary",) * n_grid_dims,
        vmem_limit_bytes=VMEM_LIMIT)


def _const_spec(shape):
    nd = len(shape)
    return pl.BlockSpec(shape, lambda *_: (0,) * nd, pipeline_mode=pl.Buffered(1))


def _inproj_kernel(x_ref, g_ref, wq_ref, wr_ref, wg_ref, bq_ref, br_ref, bg_ref,
                   cos_ref, s1_ref, s2_ref, pprev_ref, mix_ref,
                   qkv_ref, p_ref, gate_ref, plast_ref, pbuf, *, tm, seq_tiles, parts):
    @pl.when(pl.program_id(0) % seq_tiles == 0)
    def _():
        pbuf[0:8, :] = pprev_ref[...]

    pm = tm // parts
    rows = [slice(i * pm, (i + 1) * pm) for i in range(parts)]
    normed = {}

    def u(i):
        if i not in normed:
            x = x_ref[rows[i], :]
            ms = jnp.mean(x * x, axis=-1, keepdims=True)
            normed[i] = (x * lax.rsqrt(ms + RMS_EPS) * g_ref[...]).astype(BF16)
        return normed[i]

    def rope(t, r):
        return (t * cos_ref[r, :] + pltpu.roll(t, 8, 1) * s1_ref[r, :]
                + pltpu.roll(t, LANES - 8, 1) * s2_ref[r, :])

    def put(slab, r, t):
        qkv_ref[r, slab * LANES:(slab + 1) * LANES] = t.astype(BF16)

    q_slabs = Q_W // LANES
    for j in range(Q_W // MXU_COLS):
        sl = slice(j * MXU_COLS, (j + 1) * MXU_COLS)
        for i, r in enumerate(rows):
            t2 = _dot(u(i), wq_ref[:, sl]) + bq_ref[:, sl]
            for h in range(MXU_COLS // LANES):
                put(j * (MXU_COLS // LANES) + h, r, rope(t2[:, h * LANES:(h + 1) * LANES], r))
    for i, r in enumerate(rows):
        kv = _dot(u(i), wq_ref[:, Q_W:]) + bq_ref[:, Q_W:]
        k = rope(kv[:, :LANES], r)
        v = kv[:, LANES:]
        put(q_slabs, r, k)
        put(q_slabs + 1, r, pltpu.roll(k, HEAD_DIM, 1))
        put(q_slabs + 2, r, v)
        put(q_slabs + 3, r, pltpu.roll(v, HEAD_DIM, 1))

    for lo in range(0, RW_W, MXU_COLS):
        sl = slice(lo, min(lo + MXU_COLS, RW_W))
        pcs = [_dot(u(i), wr_ref[:, sl]) + br_ref[:, sl] for i in range(parts)]
        for r, pc in zip(rows, pcs):
            pbuf[8 + r.start:8 + r.stop, sl] = pc
        for r, pc in zip(rows, pcs):
            psh = pbuf[7 + r.start:7 + r.stop, sl]
            p_ref[r, sl] = pc + (psh - pc) * mix_ref[:, sl]
        tail = pcs[-1][pm - 8:pm, :]
        pbuf[0:8, sl] = tail
        plast_ref[:, sl] = tail

    for lo in range(0, 2 * D_MODEL, MXU_COLS):
        sl = slice(lo, lo + MXU_COLS)
        for i, r in enumerate(rows):
            gate_ref[r, sl] = _sigmoid(_dot(u(i), wg_ref[:, sl]) + bg_ref[:, sl]).astype(BF16)


def _inproj(x2, g, wq, wr, wg, bq, br, bg, cos, s1, s2, pprev, mix, tm):
    n = x2.shape[0]
    seq_tiles = cos.shape[0] // tm
    row = lambda i: (i, 0)
    pos = lambda i: (i % seq_tiles, 0)
    return pl.pallas_call(
        functools.partial(_inproj_kernel, tm=tm, seq_tiles=seq_tiles,
                          parts=INPROJ_PARTS if tm % (INPROJ_PARTS * 128) == 0 else 1),
        grid=(n // tm,),
        in_specs=[
            pl.BlockSpec((tm, D_MODEL), row),
            _const_spec((1, D_MODEL)),
            _const_spec((D_MODEL, ATTN_PROJ)),
            _const_spec((D_MODEL, RW_W)),
            _const_spec((D_MODEL, 2 * D_MODEL)),
            _const_spec((1, ATTN_PROJ)),
            _const_spec((1, RW_W)),
            _const_spec((1, 2 * D_MODEL)),
            pl.BlockSpec((tm, LANES), pos),
            pl.BlockSpec((tm, LANES), pos),
            pl.BlockSpec((tm, LANES), pos),
            _const_spec((8, RW_W)),
            _const_spec((1, RW_W)),
        ],
        out_specs=[
            pl.BlockSpec((tm, QKV_W), row),
            pl.BlockSpec((tm, RW_W), row),
            pl.BlockSpec((tm, 2 * D_MODEL), row),
            pl.BlockSpec((8, RW_W), lambda i: (0, 0)),
        ],
        out_shape=[
            jax.ShapeDtypeStruct((n, QKV_W), BF16),
            jax.ShapeDtypeStruct((n, RW_W), F32),
            jax.ShapeDtypeStruct((n, 2 * D_MODEL), BF16),
            jax.ShapeDtypeStruct((8, RW_W), F32),
        ],
        scratch_shapes=[pltpu.VMEM((8 + tm, RW_W), F32)],
        compiler_params=_params(1),
        name="inproj",
    )(x2, g, wq, wr, wg, bq, br, bg, cos, s1, s2, pprev, mix)


def _attn_kernel(sink_ref, q_ref, kvc_ref, kvp_ref, kvm_ref, band_ref, o_ref, *, qb):
    n = pl.program_id(1)
    blk = WINDOW
    lane = lax.broadcasted_iota(jnp.int32, (1, LANES), 1)
    m_lo = (lane < HEAD_DIM).astype(BF16)
    m_hi = (lane >= HEAD_DIM).astype(BF16)
    lo_half = lane < HEAD_DIM

    kv_all = jnp.concatenate([kvp_ref[...], kvc_ref[...]], axis=0)
    kv_meta = kvm_ref[...]

    band_mask = band_ref[...]
    col = lax.broadcasted_iota(jnp.int32, (1, 2 * blk), 1)
    no_prev = (1 - jnp.minimum(n, 1)).astype(F32)
    first_mask = band_mask + jnp.where(col < blk, NEG_INF, 0.0) * no_prev
    first = lax.broadcasted_iota(jnp.int32, (2 * blk, 2 * N_META), 1) < N_META
    top = lax.broadcasted_iota(jnp.int32, (2 * blk, 1), 0) < blk

    def halves(kv, g):
        k_plain, k_swap = kv[:, 0:LANES], kv[:, LANES:2 * LANES]
        v_plain, v_swap = kv[:, 2 * LANES:3 * LANES], kv[:, 3 * LANES:4 * LANES]
        if g == 0:
            k_lo, k_hi, v_lo, v_hi = k_plain, k_swap, v_plain, v_swap
        else:
            k_lo, k_hi, v_lo, v_hi = k_swap, k_plain, v_swap, v_plain
        return k_lo * m_lo, k_hi * m_hi, v_lo * m_lo, v_hi * m_hi

    band = [halves(kv_all, g) for g in range(KV_HEADS)]
    meta = [tuple(jnp.concatenate(pair, axis=0) for pair in
                  ((h[0], h[1]), (h[2], h[3])))
            for h in (halves(kv_meta, g) for g in range(KV_HEADS))]

    units = [(j, g) for j in range(qb) for g in range(KV_HEADS)]
    st = {}
    for key in units:
        j, g = key
        rows = slice(j * blk, (j + 2) * blk)
        k_lo, k_hi, v_lo, v_hi = band[g]
        kb = jnp.concatenate([k_lo[rows], k_hi[rows]], axis=0)
        vb = jnp.concatenate([v_lo[rows], v_hi[rows]], axis=0)
        q = jnp.concatenate(
            [q_ref[j * blk:(j + 1) * blk, s * LANES:(s + 1) * LANES] for s in (2 * g, 2 * g + 1)],
            axis=0)
        st[key] = dict(vb=vb,
                       sb=_dot_nt(q, kb),
                       sm=_dot_nt(q, meta[g][0]))

    for key in units:
        j, g = key
        u = st[key]
        mask = first_mask if j == 0 else band_mask
        sink_a = jnp.where(top, sink_ref[4 * g], sink_ref[4 * g + 2])
        sink_b = jnp.where(top, sink_ref[4 * g + 1], sink_ref[4 * g + 3])
        sb, sm = u["sb"], u["sm"]
        sa = sb[:, :2 * blk] + mask
        sbb = sb[:, 2 * blk:] + mask
        mx_a = jnp.maximum(
            jnp.maximum(jnp.max(sa, axis=-1, keepdims=True),
                        jnp.max(jnp.where(first, sm, NEG_INF), axis=-1, keepdims=True)),
            sink_a)
        mx_b = jnp.maximum(
            jnp.maximum(jnp.max(sbb, axis=-1, keepdims=True),
                        jnp.max(jnp.where(first, NEG_INF, sm), axis=-1, keepdims=True)),
            sink_b)
        pa = jnp.exp2(sa - mx_a)
        pb = jnp.exp2(sbb - mx_b)
        pm = jnp.exp2(sm - jnp.where(first, mx_a, mx_b))
        den_a = (jnp.sum(pa, axis=-1, keepdims=True)
                 + jnp.sum(jnp.where(first, pm, 0.0), axis=-1, keepdims=True)
                 + jnp.exp2(sink_a - mx_a))
        den_b = (jnp.sum(pb, axis=-1, keepdims=True)
                 + jnp.sum(jnp.where(first, 0.0, pm), axis=-1, keepdims=True)
                 + jnp.exp2(sink_b - mx_b))
        u["inv"] = jnp.where(lo_half, 1.0 / den_a, 1.0 / den_b)
        u["p_band"] = jnp.concatenate([pa, pb], axis=1).astype(BF16)
        u["p_meta"] = pm.astype(BF16)

    for key in units:
        j, g = key
        u = st[key]
        o = ((_dot(u["p_band"], u["vb"]) + _dot(u["p_meta"], meta[g][1])) * u["inv"]).astype(BF16)
        for i, s in enumerate((2 * g, 2 * g + 1)):
            o_ref[j * blk:(j + 1) * blk, s * LANES:(s + 1) * LANES] = o[i * blk:(i + 1) * blk]


def _attn(sinks, qkv, qkv_meta, batch, seq, qb):
    nb = seq // WINDOW
    steps = nb // qb
    n = batch * seq
    kvw = 4 * KV_W
    rows = qb * WINDOW
    qi = np.arange(2 * WINDOW)[:, None] % WINDOW
    c = np.arange(2 * WINDOW)[None, :]
    band_ok = np.where(c < WINDOW, c > qi, c - WINDOW <= qi)
    band_mask = jnp.asarray(np.where(band_ok, 0.0, NEG_INF).astype(np.float32))
    return pl.pallas_call(
        functools.partial(_attn_kernel, qb=qb),
        grid=(batch, steps),
        in_specs=[
            pl.BlockSpec(memory_space=pltpu.SMEM),
            pl.BlockSpec((rows, Q_W), lambda b, i: (b * steps + i, 0)),
            pl.BlockSpec((rows, kvw), lambda b, i: (b * steps + i, 1)),
            pl.BlockSpec((WINDOW, kvw), lambda b, i: (jnp.maximum((b * steps + i) * qb - 1, 0), 1)),
            pl.BlockSpec((N_META, kvw), lambda b, i: (0, 1)),
            _const_spec((2 * WINDOW, 2 * WINDOW)),
        ],
        out_specs=pl.BlockSpec((rows, Q_W), lambda b, i: (b * steps + i, 0)),
        out_shape=jax.ShapeDtypeStruct((n, Q_W), BF16),
        compiler_params=_params(2),
        name="attn",
    )(sinks, qkv, qkv, qkv, qkv_meta, band_mask)


def _emit_interleaved(*segment_lists):
    keyed = []
    for li, segs in enumerate(segment_lists):
        costs = [_SEGMENT_COST.get(seg.__name__, 1.0) for seg in segs]
        total, done = sum(costs), 0.0
        for j, (seg, cost) in enumerate(zip(segs, costs)):
            keyed.append(((done + 0.5 * cost) / total, li, j, seg))
            done += cost
    for _, _, _, seg in sorted(keyed, key=lambda t: t[:3]):
        seg()


_SEGMENT_COST = dict(
    lora=5, decay_gate=8, norms=5, cumulative=9, pair_prep=2.5,
    scores=6, square=3, neumann=6, values=3, solve=6,
    apply_state=5, outputs=4, update_state=7, centre=3, variance=2, finish=4)


def _rwkv_kernel(*refs, nc, pipelined):
    if pipelined:
        p0_ref, later_p_refs, next_p_ref = refs[0], refs[1:pipelined], refs[pipelined]
        (s0_ref, w0_ref, a0_ref, w2a_ref, g2_ref, kk_ref, ka_ref, rk_ref,
         lnw_ref, lnb_ref, bd_ref, tri_ref, eye_ref,
         y_ref, sfin_ref, s_scr, c_f32, c_pc, c_blk, c_rows, c_chunk) = refs[pipelined + 1:]
    else:
        (p0_ref, s0_ref, w0_ref, a0_ref, w2a_ref, g2_ref, kk_ref, ka_ref, rk_ref,
         lnw_ref, lnb_ref, bd_ref, tri_ref, eye_ref,
         y_ref, sfin_ref, s_scr) = refs
    c = pl.program_id(1)
    tok = nc * CHUNK
    o1, o2, o3 = RWKV_DIM, 2 * RWKV_DIM, 3 * RWKV_DIM
    pairs = [(gi, ri) for ri in range(nc) for gi in range(N_GROUPS)]

    bd = bd_ref[...]
    bd_f = bd.astype(F32)
    tri_mask = tri_ref[...]
    eye = eye_ref[...]
    lane = lax.broadcasted_iota(jnp.int32, (tok, LANES), 1)
    lane1 = lax.broadcasted_iota(jnp.int32, (1, LANES), 1)
    head_lo = (lane1 < HEAD_DIM).astype(BF16)
    head_hi = (lane1 >= HEAD_DIM).astype(BF16)
    zero_slab = jnp.zeros((CHUNK, LANES), BF16)

    def blk(x):
        xb = x.astype(BF16)
        left, right = xb[:, :LANES], xb[:, LANES:]
        return jnp.concatenate(
            [jnp.concatenate([left * head_lo, zero_slab], axis=1),
             jnp.concatenate([left * head_hi, zero_slab], axis=1),
             jnp.concatenate([zero_slab, right * head_lo], axis=1),
             jnp.concatenate([zero_slab, right * head_hi], axis=1)], axis=0)

    pos_in_chunk = lax.broadcasted_iota(jnp.int32, (tok, 1), 0) % CHUNK

    def seg_sums(xs):
        out = _dot(jnp.concatenate([x.astype(BF16) for x in xs], axis=0), bd)
        return [out[i * tok:(i + 1) * tok] for i in range(len(xs))]

    def grp(x, gi):
        return x[:, gi * GROUP:(gi + 1) * GROUP]

    def val(q, name):
        item = q[name]
        return item() if callable(item) else item

    def front_segments(p_ref, out, to_carry):
        f = {}

        def lora():
            pf = p_ref[...].reshape(tok, RW_W)
            f["r"], f["k"], f["v"] = pf[:, :o1], pf[:, o1:o2], pf[:, o2:o3]
            dwa = pf[:, o3:o3 + LANES]
            f["dg"] = pf[:, o3 + LANES:]
            z = jnp.where(lane < DECAY_LORA, jnp.tanh(dwa), dwa).astype(BF16)
            f["wa"] = _dot(z, w2a_ref[...])

        def decay_gate():
            zw = -(w0_ref[...] + f["wa"][:, :o1])
            softplus = jnp.maximum(zw, 0.0) + jnp.log(1.0 + jnp.exp(-jnp.abs(zw)))
            f["logw"] = -jnp.exp(-softplus - 0.5)
            f["a"] = _sigmoid(a0_ref[...] + f["wa"][:, o1:])
            f["gate"] = _dot(_sigmoid(f["dg"]).astype(BF16), g2_ref[...])

        def norms():
            f["kkv"] = f["k"] * kk_ref[...]
            f["kp"] = f["k"] * (1.0 + (f["a"] - 1.0) * ka_ref[...])
            rkk = f["r"] * f["kp"] * rk_ref[...]
            sq = f["kkv"] * f["kkv"]
            f["sums"] = seg_sums([grp(sq, 0), grp(sq, 1), grp(rkk, 0), grp(rkk, 1)])

        def cumulative():
            ss = jnp.concatenate(f["sums"][0:2], axis=1)
            f["bonus"] = jnp.concatenate(f["sums"][2:4], axis=1)
            kkn = f["kkv"] / jnp.maximum(jnp.sqrt(ss), 1e-12)
            f["aa"] = -kkn
            f["bb"] = kkn * f["a"]
            lp = f["logw"]
            step = 1
            while step < CHUNK:
                lp = lp + jnp.where(pos_in_chunk >= step, pltpu.roll(lp, step, 0), 0.0)
                step *= 2
            f["lps"] = [grp(lp, gi) for gi in range(N_GROUPS)]
            chunk = dict(gate=f["gate"], bonus=f["bonus"], v=f["v"])
            if to_carry is not None:
                c_chunk[to_carry, 0] = chunk["gate"]
                c_chunk[to_carry, 1] = chunk["bonus"]
                c_chunk[to_carry, 2] = chunk["v"]
            out["chunk"] = chunk
            out["pairs"] = {}

        def pair_prep(i, key):
            def run():
                gi, ri = key
                rows = slice(ri * CHUNK, (ri + 1) * CHUNK)
                cut = lambda x: x[rows, gi * GROUP:(gi + 1) * GROUP]
                lp = f["lps"][gi][rows]
                lw = cut(f["logw"])
                lpc = lp[CHUNK - 1:CHUNK, :]
                e_neg = jnp.exp(-lp)
                e_end = jnp.exp(lpc - lp)
                rg, kg, vg = cut(f["r"]), cut(f["kp"]), cut(f["v"])
                ag, bg = cut(f["aa"]), cut(f["bb"])
                at = ag * jnp.exp(lp - lw)
                rt = rg * jnp.exp(lp)
                q = dict(
                    at=at, vg=vg, pc=jnp.exp(lpc), vblk=blk(vg),
                    bt=blk(bg * e_neg), kt=blk(kg * e_neg),
                    ar=jnp.concatenate([at, rt], axis=0).astype(BF16),
                    bk_end=jnp.concatenate([bg * e_end, kg * e_end], axis=0).astype(BF16))
                if to_carry is not None:
                    c_f32[to_carry, i, 0], c_f32[to_carry, i, 1] = q["at"], q["vg"]
                    c_pc[to_carry, i, 0:1, :] = q["pc"]
                    c_blk[to_carry, i, 0], c_blk[to_carry, i, 1] = q["bt"], q["kt"]
                    c_blk[to_carry, i, 2] = q["vblk"]
                    c_rows[to_carry, i, 0], c_rows[to_carry, i, 1] = q["ar"], q["bk_end"]
                out["pairs"][key] = q
            run.__name__ = "pair_prep"
            return run

        return [lora, decay_gate, norms, cumulative] + [pair_prep(i, k) for i, k in enumerate(pairs)]

    def carried(slot):
        def pair(i):
            return dict(
                at=lambda: c_f32[slot, i, 0], vg=lambda: c_f32[slot, i, 1],
                pc=lambda: c_pc[slot, i, 0:1, :],
                bt=lambda: c_blk[slot, i, 0], kt=lambda: c_blk[slot, i, 1],
                vblk=lambda: c_blk[slot, i, 2],
                ar=lambda: c_rows[slot, i, 0], bk_end=lambda: c_rows[slot, i, 1])
        chunk = dict(gate=lambda: c_chunk[slot, 0], bonus=lambda: c_chunk[slot, 1],
                     v=lambda: c_chunk[slot, 2])
        return {key: pair(i) for i, key in enumerate(pairs)}, chunk

    def intra_segments(st):
        def stages(batch):
            def scores():
                for key in batch:
                    q = st[key]
                    ar = val(q, "ar")
                    sb = _dot_nt(ar, val(q, "bt")) * tri_mask
                    sk = _dot_nt(ar, val(q, "kt")) * tri_mask
                    q["a_ab"] = sb[:CHUNK]
                    q["a_rb"] = sb[CHUNK:].astype(BF16)
                    q["a_k"] = sk.astype(BF16)

            def square():
                for key in batch:
                    q = st[key]
                    q["pw"] = _dot(q["a_ab"].astype(BF16), blk(q["a_ab"]))
                    q["minv"] = eye + q["a_ab"]

            def neumann(j):
                def run():
                    for key in batch:
                        q = st[key]
                        if j < 5:
                            out = _dot(q["pw"].astype(BF16),
                                       jnp.concatenate([blk(q["pw"]), blk(q["minv"])], axis=1))
                            q["pw"] = out[:, :GROUP]
                            q["minv"] = q["minv"] + out[:, GROUP:]
                        else:
                            q["minv"] = q["minv"] + _dot(q["pw"].astype(BF16), blk(q["minv"]))
                run.__name__ = "neumann"
                return run

            def values():
                for key in batch:
                    q = st[key]
                    kv = _dot(q["a_k"], val(q, "vblk"))
                    q["av"] = kv[:CHUNK]
                    q["y_rk"] = kv[CHUNK:]

            def solve():
                for key in batch:
                    q = st[key]
                    wu = _dot(q["minv"].astype(BF16),
                              jnp.concatenate([blk(val(q, "at")), blk(q["av"])], axis=1))
                    q["w"] = wu[:, :GROUP].astype(BF16)
                    q["u0"] = wu[:, GROUP:]

            return [scores, square] + [neumann(j) for j in range(1, 6)] + [values, solve]

        segs = []
        for lo in range(0, len(pairs), RWKV_CHAINS):
            segs.extend(stages(pairs[lo:lo + RWKV_CHAINS]))
        return segs

    def state_back_segments(st, chunk, write_y):
        def apply_state():
            for key in pairs:
                gi, ri = key
                q = st[key]
                q["s"] = s_scr[ri, gi]
                rt = val(q, "ar")[CHUNK:]
                ws = _dot_nt(jnp.concatenate([q["w"], rt], axis=0), q["s"].astype(BF16))
                q["u"] = ws[:CHUNK] + q["u0"]
                q["y_rs"] = ws[CHUNK:]

        def outputs():
            for key in pairs:
                q = st[key]
                q["y"] = q["y_rs"] + _dot(q["a_rb"], blk(q["u"])) + q["y_rk"]

        def update_state():
            for key in pairs:
                gi, ri = key
                q = st[key]
                upd = _dot_tn(jnp.concatenate([q["u"], val(q, "vg")], axis=0).astype(BF16),
                              val(q, "bk_end"))
                s_scr[ri, gi] = q["s"] * val(q, "pc") + upd * bd_f

        g = {}

        def centre():
            g["ys"] = [jnp.concatenate([st[gi, ri]["y"] for ri in range(nc)], axis=0)
                       if nc > 1 else st[gi, 0]["y"] for gi in range(N_GROUPS)]
            means = seg_sums(g["ys"])
            g["ds"] = [g["ys"][i] - means[i] * (1.0 / HEAD_DIM) for i in range(N_GROUPS)]

        def variance():
            g["vars"] = seg_sums([d * d for d in g["ds"]])

        def finish():
            yn = jnp.concatenate(
                [g["ds"][i] * lax.rsqrt(g["vars"][i] * (1.0 / HEAD_DIM) + RWKV_LN_EPS)
                 for i in range(N_GROUPS)], axis=1)
            yn = yn * lnw_ref[...] + lnb_ref[...]
            out = (yn + val(chunk, "bonus") * val(chunk, "v")) * val(chunk, "gate")
            write_y(out.astype(BF16).reshape(nc, CHUNK, RWKV_DIM))

        return [apply_state, outputs, update_state, centre, variance, finish]

    def init_state():
        for ri in range(nc):
            s_scr[ri] = s0_ref[...]

    if pipelined:
        slot = c % 2

        @pl.when(c == 0)
        def _():
            init_state()
            _emit_interleaved(front_segments(p0_ref, {}, 0))

        def write_part(k):
            def write(y):
                y_ref[:, k * CHUNK:(k + 1) * CHUNK, :] = y
            return write

        cur, cur_chunk = carried(slot)
        done = None
        for k, p_ref in enumerate(later_p_refs):
            nxt = {}
            lists = [intra_segments(cur), front_segments(p_ref, nxt, None)]
            if done is not None:
                lists.append(state_back_segments(*done, write_part(k - 1)))
            _emit_interleaved(*lists)
            done = (cur, cur_chunk)
            cur, cur_chunk = nxt["pairs"], nxt["chunk"]
        last = len(later_p_refs)
        _emit_interleaved(intra_segments(cur),
                          state_back_segments(*done, write_part(last - 1)),
                          front_segments(next_p_ref, {}, 1 - slot))
        _emit_interleaved(state_back_segments(cur, cur_chunk, write_part(last)))
    else:
        pl.when(c == 0)(init_state)
        only = {}

        def write_all(y):
            y_ref[...] = y

        _emit_interleaved(front_segments(p0_ref, only, None))
        _emit_interleaved(intra_segments(only["pairs"]))
        _emit_interleaved(state_back_segments(only["pairs"], only["chunk"], write_all))

    @pl.when(c == pl.num_programs(1) - 1)
    def _():
        sfin_ref[...] = s_scr[...]


def _rwkv(p, s0, vecs, w2a, g2p, consts, batch, n_chunks, nc, pipelined):
    w0, a0, k_k, k_a, r_k, ln_w, ln_b = vecs
    bd, tri_mask, eye = consts
    tok = nc * CHUNK
    n_pairs = nc * N_GROUPS
    vec = _const_spec((1, RWKV_DIM))
    chunk_spec = lambda index: pl.BlockSpec((nc, CHUNK, RW_W), index)
    if pipelined:
        per_step = pipelined
        steps = n_chunks // per_step
        p_specs = ([pl.BlockSpec((nc, CHUNK, RW_W), lambda b, c: (b, 0, 0),
                                 pipeline_mode=pl.Buffered(1))]
                   + [chunk_spec(functools.partial(lambda b, c, k: (b, per_step * c + k, 0), k=k))
                      for k in range(1, per_step)]
                   + [chunk_spec(lambda b, c: (b, jnp.minimum(per_step * (c + 1), n_chunks - 1), 0))])
        p_args = [p] * (per_step + 1)
        y_rows = per_step * CHUNK
        carry = [pltpu.VMEM((2, n_pairs, 2, CHUNK, GROUP), F32),
                 pltpu.VMEM((2, n_pairs, 8, GROUP), F32),
                 pltpu.VMEM((2, n_pairs, 3, GROUP, GROUP), BF16),
                 pltpu.VMEM((2, n_pairs, 2, 2 * CHUNK, GROUP), BF16),
                 pltpu.VMEM((2, 3, tok, RWKV_DIM), F32)]
    else:
        steps = n_chunks
        p_specs = [chunk_spec(lambda b, c: (b, c, 0))]
        p_args = [p]
        y_rows = CHUNK
        carry = []
    return pl.pallas_call(
        functools.partial(_rwkv_kernel, nc=nc, pipelined=pipelined),
        grid=(batch // nc, steps),
        in_specs=p_specs + [
            _const_spec((N_GROUPS, GROUP, GROUP)),
            vec, vec,
            _const_spec((LANES, 2 * RWKV_DIM)),
            _const_spec((GATE_PAD, RWKV_DIM)),
            vec, vec, vec, vec, vec,
            _const_spec((GROUP, GROUP)),
            _const_spec((2 * CHUNK, GROUP)),
            _const_spec((CHUNK, GROUP)),
        ],
        out_specs=[
            pl.BlockSpec((nc, y_rows, RWKV_DIM), lambda b, c: (b, c, 0)),
            pl.BlockSpec((nc, N_GROUPS, GROUP, GROUP), lambda b, c: (b, 0, 0, 0),
                         pipeline_mode=pl.Buffered(1)),
        ],
        out_shape=[
            jax.ShapeDtypeStruct((batch, n_chunks * CHUNK, RWKV_DIM), BF16),
            jax.ShapeDtypeStruct((batch, N_GROUPS, GROUP, GROUP), F32),
        ],
        scratch_shapes=[pltpu.VMEM((nc, N_GROUPS, GROUP, GROUP), F32)] + carry,
        compiler_params=_params(2),
        name="rwkv",
    )(*p_args, s0, w0, a0, w2a, g2p, k_k, k_a, r_k, ln_w, ln_b, bd, tri_mask, eye)


def _post_kernel(ya_ref, yr_ref, gate_ref, x_ref, wba_ref, wbr_ref, wo_ref, gf_ref,
                 wg_ref, wu_ref, wd_ref, gn_ref, o_ref, *, parts):
    pm = o_ref.shape[0] // parts
    rows = [slice(i * pm, (i + 1) * pm) for i in range(parts)]

    def rms(h, g_ref):
        ms = jnp.mean(h * h, axis=-1, keepdims=True)
        return h * lax.rsqrt(ms + RMS_EPS) * g_ref[...]

    merged = []
    for r in rows:
        gates = gate_ref[r, :].astype(F32)
        merged.append((gates[:, :D_MODEL] * _dot(ya_ref[r, :], wba_ref[...])
                       + gates[:, D_MODEL:] * _dot(yr_ref[r, :], wbr_ref[...])).astype(BF16))
    h1 = [x_ref[r, :] + _dot(m, wo_ref[...]) for r, m in zip(rows, merged)]
    acts = []
    for h in h1:
        f = rms(h, gf_ref).astype(BF16)
        gt = _dot(f, wg_ref[...])
        up = _dot(f, wu_ref[...])
        acts.append((gt * _sigmoid(gt) * up).astype(BF16))
    h2 = [h + _dot(a, wd_ref[...]) for h, a in zip(h1, acts)]
    for r, h in zip(rows, h2):
        o_ref[r, :] = rms(h, gn_ref)


def _post(ya, yr, gates, x2, wba, wbr, wo, gf, wg, wu, wd, gn, tm):
    n = x2.shape[0]
    row = lambda i: (i, 0)
    return pl.pallas_call(
        functools.partial(_post_kernel, parts=POST_PARTS),
        grid=(n // tm,),
        in_specs=[
            pl.BlockSpec((tm, Q_W), row),
            pl.BlockSpec((tm, RWKV_DIM), row),
            pl.BlockSpec((tm, 2 * D_MODEL), row),
            pl.BlockSpec((tm, D_MODEL), row),
            _const_spec((Q_W, D_MODEL)),
            _const_spec((RWKV_DIM, D_MODEL)),
            _const_spec((D_MODEL, D_MODEL)),
            _const_spec((1, D_MODEL)),
            _const_spec((D_MODEL, D_FF)),
            _const_spec((D_MODEL, D_FF)),
            _const_spec((D_FF, D_MODEL)),
            _const_spec((1, D_MODEL)),
        ],
        out_specs=pl.BlockSpec((tm, D_MODEL), row),
        out_shape=jax.ShapeDtypeStruct((n, D_MODEL), F32),
        compiler_params=_params(1),
        name="post",
    )(ya, yr, gates, x2, wba, wbr, wo, gf, wg, wu, wd, gn)


def _rope_tables(first_pos, n):
    half = ROPE_DIM // 2
    f32 = np.float32
    inv_freq = np.power(f32(ROPE_THETA), -np.arange(half, dtype=f32) * f32(2.0 / ROPE_DIM))
    pos = (first_pos + np.arange(n)).astype(f32)
    ang = (pos[:, None] * inv_freq[None, :]).astype(f32)
    cos, sin = np.cos(ang).astype(f32), np.sin(ang).astype(f32)
    one = np.ones((n, HEAD_DIM - ROPE_DIM), f32)
    zero8 = np.zeros((n, half), f32)
    zero48 = np.zeros((n, HEAD_DIM - ROPE_DIM), f32)
    c_head = np.concatenate([cos, cos, one], axis=1)
    s1_head = np.concatenate([zero8, sin, zero48], axis=1)
    s2_head = np.concatenate([-sin, zero8, zero48], axis=1)
    dup = lambda t: jnp.asarray(np.concatenate([t, t], axis=1))
    return dup(c_head), dup(s1_head), dup(s2_head)


def _rwkv_constants():
    t = np.arange(CHUNK)
    hh = np.arange(GROUP) // HEAD_DIM
    bd = (hh[:, None] == hh[None, :]).astype(np.float32)
    s = np.arange(GROUP) % CHUNK
    strict = (s[None, :] < t[:, None]).astype(np.float32)
    incl = (s[None, :] <= t[:, None]).astype(np.float32)
    eye = (s[None, :] == t[:, None]).astype(np.float32)
    return (jnp.asarray(bd, BF16), jnp.asarray(np.concatenate([strict, incl], axis=0)),
            jnp.asarray(eye))


def kernel(x, meta_tokens, norm_mix_g, w_in, b_in, attn_sinks, rwkv_mix, rwkv_w0, rwkv_w2, rwkv_a0, rwkv_a2, rwkv_g2, rwkv_k_k, rwkv_k_a, rwkv_r_k, rwkv_ln_w, rwkv_ln_b, w_br_attn, w_br_rwkv, w_o, norm_ffn_g, w_ffn_gate, w_ffn_up, w_ffn_down, norm_final_g):
    batch, seq, _ = x.shape
    layer = 0
    x2 = x.reshape(batch * seq, D_MODEL)

    w = w_in[layer]
    b = b_in[layer]
    scale = LOG2_E * HEAD_DIM ** -0.5

    col_scale = jnp.where(jnp.arange(w.shape[-1]) < Q_W, scale, 1.0).astype(F32)

    def rw_cols(t, n_lead):
        pad = jnp.zeros(t.shape[:n_lead] + (GATE_PAD - GATE_LORA,), t.dtype)
        return jnp.concatenate([t, pad], axis=-1)

    wb = (w * col_scale).astype(BF16)
    b = b * col_scale
    wq = wb[:, :ATTN_PROJ]
    bq = b[:ATTN_PROJ][None]
    wr = rw_cols(wb[:, ATTN_PROJ:ATTN_PROJ + RWKV_PROJ], 1)
    br = rw_cols(b[ATTN_PROJ:ATTN_PROJ + RWKV_PROJ], 0)[None]
    wg = wb[:, ATTN_PROJ + RWKV_PROJ:]
    bg = b[ATTN_PROJ + RWKV_PROJ:][None]
    g_mix = norm_mix_g[layer][None]

    mix = rw_cols(rwkv_mix[layer], 0)[None]
    zl = jnp.zeros((DECAY_LORA, RWKV_DIM), F32)
    w2a = jnp.concatenate(
        [jnp.concatenate([rwkv_w2[layer], zl], axis=1),
         jnp.concatenate([zl, rwkv_a2[layer]], axis=1)], axis=0).astype(BF16)
    g2p = jnp.concatenate(
        [rwkv_g2[layer], jnp.zeros((GATE_PAD - GATE_LORA, RWKV_DIM), F32)], axis=0).astype(BF16)
    vecs = (rwkv_w0[layer][None], rwkv_a0[layer][None], rwkv_k_k[layer][None],
            rwkv_k_a[layer][None], rwkv_r_k[layer].reshape(1, RWKV_DIM),
            rwkv_ln_w[layer][None], rwkv_ln_b[layer][None])

    cos_m, s1_m, s2_m = _rope_tables(0, N_META)
    qkv_m, p_m, _, p_m_tail = _inproj(meta_tokens, g_mix, wq, wr, wg, bq, br, bg,
                                      cos_m, s1_m, s2_m, jnp.zeros((8, RW_W), F32), mix, N_META)
    p_m_pad = jnp.concatenate([jnp.zeros((CHUNK - N_META, RW_W), F32), p_m], axis=0)
    zero_state = jnp.zeros((N_GROUPS, GROUP, GROUP), F32)
    _, s_meta = _rwkv(p_m_pad[None], zero_state, vecs, w2a, g2p, _rwkv_constants(), 1, 1, 1, 0)

    cos, s1, s2 = _rope_tables(N_META, seq)
    tm_in = min(512, seq)
    qkv, p, gates, _ = _inproj(x2, g_mix, wq, wr, wg, bq, br, bg, cos, s1, s2,
                               p_m_tail, mix, tm_in)
    y_attn = _attn(attn_sinks[layer] * LOG2_E, qkv, qkv_m, batch, seq, ATTN_BLOCKS_PER_STEP)
    nc = min(RWKV_ROWS_PER_STEP, batch)
    y_rwkv, _ = _rwkv(p.reshape(batch, seq, RW_W), s_meta[0], vecs, w2a, g2p,
                      _rwkv_constants(), batch, seq // CHUNK, nc, RWKV_CHUNKS_PER_STEP)
    out = _post(y_attn, y_rwkv.reshape(batch * seq, RWKV_DIM), gates, x2,
                w_br_attn[layer].astype(BF16), w_br_rwkv[layer].astype(BF16),
                w_o[layer].astype(BF16), norm_ffn_g[layer][None],
                w_ffn_gate[layer].astype(BF16), w_ffn_up[layer].astype(BF16),
                w_ffn_down[layer].astype(BF16), norm_final_g[None], min(POST_ROWS, seq))
    return out.reshape(batch, seq, D_MODEL)
```

```python
import functools

import jax
import jax.numpy as jnp
import numpy as np
from jax import lax
from jax.experimental import pallas as pl
from jax.experimental.pallas import tpu as pltpu

F32 = jnp.float32
BF16 = jnp.bfloat16

D_MODEL = 1024
N_META = 16
HEAD_DIM = 64
Q_HEADS = 8
KV_HEADS = 2
WINDOW = 128
ROPE_THETA = 500000.0
ROPE_DIM = HEAD_DIM // 4
RWKV_HEADS = 8
RWKV_DIM = RWKV_HEADS * HEAD_DIM
DECAY_LORA = 64
AAA_LORA = 64
GATE_LORA = 160
RWKV_LN_EPS = 64e-5
D_FF = 2816
Q_W = Q_HEADS * HEAD_DIM
KV_W = KV_HEADS * HEAD_DIM
ATTN_PROJ = Q_W + 2 * KV_W
RWKV_PROJ = 3 * RWKV_DIM + DECAY_LORA + AAA_LORA + GATE_LORA
RMS_EPS = 1e-6
NEG_INF = -1e30
LOG2_E = 1.4426950408889634

LANES = 128
MXU_COLS = 256
QKV_W = Q_W + 4 * KV_W
GATE_PAD = 256
RW_W = 3 * RWKV_DIM + LANES + GATE_PAD
CHUNK = 64
GROUP = 4 * HEAD_DIM
N_GROUPS = RWKV_DIM // GROUP
ATTN_BLOCKS_PER_STEP = 8
RWKV_ROWS_PER_STEP = 4
RWKV_CHUNKS_PER_STEP = 2
RWKV_CHAINS = 8
INPROJ_PARTS = 4
POST_ROWS = 512
POST_PARTS = 2
VMEM_LIMIT = 56 * 1024 * 1024


def _dot(a, b):
    return jnp.dot(a, b, preferred_element_type=F32)


def _dot_nt(a, b):
    return lax.dot_general(a, b, (((1,), (1,)), ((), ())), preferred_element_type=F32)


def _dot_tn(a, b):
    return lax.dot_general(a, b, (((0,), (0,)), ((), ())), preferred_element_type=F32)


def _sigmoid(x):
    return 1.0 / (1.0 + jnp.exp(-x))


def _params(n_grid_dims):
    return pltpu.CompilerParams(
        dimension_semantics=("arbitrary",) * n_grid_dims,
        vmem_limit_bytes=VMEM_LIMIT)


def _const_spec(shape):
    nd = len(shape)
    return pl.BlockSpec(shape, lambda *_: (0,) * nd, pipeline_mode=pl.Buffered(1))


def _inproj_kernel(x_ref, g_ref, wq_ref, wr_ref, wg_ref, bq_ref, br_ref, bg_ref,
                   cos_ref, s1_ref, s2_ref, pprev_ref, mix_ref,
                   qkv_ref, p_ref, gate_ref, plast_ref, pbuf, *, tm, seq_tiles, parts):
    @pl.when(pl.program_id(0) % seq_tiles == 0)
    def _():
        pbuf[0:8, :] = pprev_ref[...]

    pm = tm // parts
    rows = [slice(i * pm, (i + 1) * pm) for i in range(parts)]
    normed = {}

    def u(i):
        if i not in normed:
            x = x_ref[rows[i], :]
            ms = jnp.mean(x * x, axis=-1, keepdims=True)
            normed[i] = (x * lax.rsqrt(ms + RMS_EPS) * g_ref[...]).astype(BF16)
        return normed[i]

    def rope(t, r):
        return (t * cos_ref[r, :] + pltpu.roll(t, 8, 1) * s1_ref[r, :]
                + pltpu.roll(t, LANES - 8, 1) * s2_ref[r, :])

    def put(slab, r, t):
        qkv_ref[r, slab * LANES:(slab + 1) * LANES] = t.astype(BF16)

    q_slabs = Q_W // LANES
    for j in range(Q_W // MXU_COLS):
        sl = slice(j * MXU_COLS, (j + 1) * MXU_COLS)
        for i, r in enumerate(rows):
            t2 = _dot(u(i), wq_ref[:, sl]) + bq_ref[:, sl]
            for h in range(MXU_COLS // LANES):
                put(j * (MXU_COLS // LANES) + h, r, rope(t2[:, h * LANES:(h + 1) * LANES], r))
    for i, r in enumerate(rows):
        kv = _dot(u(i), wq_ref[:, Q_W:]) + bq_ref[:, Q_W:]
        k = rope(kv[:, :LANES], r)
        v = kv[:, LANES:]
        put(q_slabs, r, k)
        put(q_slabs + 1, r, pltpu.roll(k, HEAD_DIM, 1))
        put(q_slabs + 2, r, v)
        put(q_slabs + 3, r, pltpu.roll(v, HEAD_DIM, 1))

    for lo in range(0, RW_W, MXU_COLS):
        sl = slice(lo, min(lo + MXU_COLS, RW_W))
        pcs = [_dot(u(i), wr_ref[:, sl]) + br_ref[:, sl] for i in range(parts)]
        for r, pc in zip(rows, pcs):
            pbuf[8 + r.start:8 + r.stop, sl] = pc
        for r, pc in zip(rows, pcs):
            psh = pbuf[7 + r.start:7 + r.stop, sl]
            p_ref[r, sl] = pc + (psh - pc) * mix_ref[:, sl]
        tail = pcs[-1][pm - 8:pm, :]
        pbuf[0:8, sl] = tail
        plast_ref[:, sl] = tail

    for lo in range(0, 2 * D_MODEL, MXU_COLS):
        sl = slice(lo, lo + MXU_COLS)
        for i, r in enumerate(rows):
            gate_ref[r, sl] = _sigmoid(_dot(u(i), wg_ref[:, sl]) + bg_ref[:, sl]).astype(BF16)


def _inproj(x2, g, wq, wr, wg, bq, br, bg, cos, s1, s2, pprev, mix, tm):
    n = x2.shape[0]
    seq_tiles = cos.shape[0] // tm
    row = lambda i: (i, 0)
    pos = lambda i: (i % seq_tiles, 0)
    return pl.pallas_call(
        functools.partial(_inproj_kernel, tm=tm, seq_tiles=seq_tiles,
                          parts=INPROJ_PARTS if tm % (INPROJ_PARTS * 128) == 0 else 1),
        grid=(n // tm,),
        in_specs=[
            pl.BlockSpec((tm, D_MODEL), row),
            _const_spec((1, D_MODEL)),
            _const_spec((D_MODEL, ATTN_PROJ)),
            _const_spec((D_MODEL, RW_W)),
            _const_spec((D_MODEL, 2 * D_MODEL)),
            _const_spec((1, ATTN_PROJ)),
            _const_spec((1, RW_W)),
            _const_spec((1, 2 * D_MODEL)),
            pl.BlockSpec((tm, LANES), pos),
            pl.BlockSpec((tm, LANES), pos),
            pl.BlockSpec((tm, LANES), pos),
            _const_spec((8, RW_W)),
            _const_spec((1, RW_W)),
        ],
        out_specs=[
            pl.BlockSpec((tm, QKV_W), row),
            pl.BlockSpec((tm, RW_W), row),
            pl.BlockSpec((tm, 2 * D_MODEL), row),
            pl.BlockSpec((8, RW_W), lambda i: (0, 0)),
        ],
        out_shape=[
            jax.ShapeDtypeStruct((n, QKV_W), BF16),
            jax.ShapeDtypeStruct((n, RW_W), F32),
            jax.ShapeDtypeStruct((n, 2 * D_MODEL), BF16),
            jax.ShapeDtypeStruct((8, RW_W), F32),
        ],
        scratch_shapes=[pltpu.VMEM((8 + tm, RW_W), F32)],
        compiler_params=_params(1),
        name="inproj",
    )(x2, g, wq, wr, wg, bq, br, bg, cos, s1, s2, pprev, mix)


def _attn_kernel(sink_ref, q_ref, kvc_ref, kvp_ref, kvm_ref, band_ref, o_ref, *, qb):
    n = pl.program_id(1)
    blk = WINDOW
    lane = lax.broadcasted_iota(jnp.int32, (1, LANES), 1)
    m_lo = (lane < HEAD_DIM).astype(BF16)
    m_hi = (lane >= HEAD_DIM).astype(BF16)
    lo_half = lane < HEAD_DIM

    kv_all = jnp.concatenate([kvp_ref[...], kvc_ref[...]], axis=0)
    kv_meta = kvm_ref[...]

    band_mask = band_ref[...]
    col = lax.broadcasted_iota(jnp.int32, (1, 2 * blk), 1)
    no_prev = (1 - jnp.minimum(n, 1)).astype(F32)
    first_mask = band_mask + jnp.where(col < blk, NEG_INF, 0.0) * no_prev
    first = lax.broadcasted_iota(jnp.int32, (2 * blk, 2 * N_META), 1) < N_META
    top = lax.broadcasted_iota(jnp.int32, (2 * blk, 1), 0) < blk

    def halves(kv, g):
        k_plain, k_swap = kv[:, 0:LANES], kv[:, LANES:2 * LANES]
        v_plain, v_swap = kv[:, 2 * LANES:3 * LANES], kv[:, 3 * LANES:4 * LANES]
        if g == 0:
            k_lo, k_hi, v_lo, v_hi = k_plain, k_swap, v_plain, v_swap
        else:
            k_lo, k_hi, v_lo, v_hi = k_swap, k_plain, v_swap, v_plain
        return k_lo * m_lo, k_hi * m_hi, v_lo * m_lo, v_hi * m_hi

    band = [halves(kv_all, g) for g in range(KV_HEADS)]
    meta = [tuple(jnp.concatenate(pair, axis=0) for pair in
                  ((h[0], h[1]), (h[2], h[3])))
            for h in (halves(kv_meta, g) for g in range(KV_HEADS))]

    units = [(j, g) for j in range(qb) for g in range(KV_HEADS)]
    st = {}
    for key in units:
        j, g = key
        rows = slice(j * blk, (j + 2) * blk)
        k_lo, k_hi, v_lo, v_hi = band[g]
        kb = jnp.concatenate([k_lo[rows], k_hi[rows]], axis=0)
        vb = jnp.concatenate([v_lo[rows], v_hi[rows]], axis=0)
        q = jnp.concatenate(
            [q_ref[j * blk:(j + 1) * blk, s * LANES:(s + 1) * LANES] for s in (2 * g, 2 * g + 1)],
            axis=0)
        st[key] = dict(vb=vb,
                       sb=_dot_nt(q, kb),
                       sm=_dot_nt(q, meta[g][0]))

    for key in units:
        j, g = key
        u = st[key]
        mask = first_mask if j == 0 else band_mask
        sink_a = jnp.where(top, sink_ref[4 * g], sink_ref[4 * g + 2])
        sink_b = jnp.where(top, sink_ref[4 * g + 1], sink_ref[4 * g + 3])
        sb, sm = u["sb"], u["sm"]
        sa = sb[:, :2 * blk] + mask
        sbb = sb[:, 2 * blk:] + mask
        mx_a = jnp.maximum(
            jnp.maximum(jnp.max(sa, axis=-1, keepdims=True),
                        jnp.max(jnp.where(first, sm, NEG_INF), axis=-1, keepdims=True)),
            sink_a)
        mx_b = jnp.maximum(
            jnp.maximum(jnp.max(sbb, axis=-1, keepdims=True),
                        jnp.max(jnp.where(first, NEG_INF, sm), axis=-1, keepdims=True)),
            sink_b)
        pa = jnp.exp2(sa - mx_a)
        pb = jnp.exp2(sbb - mx_b)
        pm = jnp.exp2(sm - jnp.where(first, mx_a, mx_b))
        den_a = (jnp.sum(pa, axis=-1, keepdims=True)
                 + jnp.sum(jnp.where(first, pm, 0.0), axis=-1, keepdims=True)
                 + jnp.exp2(sink_a - mx_a))
        den_b = (jnp.sum(pb, axis=-1, keepdims=True)
                 + jnp.sum(jnp.where(first, 0.0, pm), axis=-1, keepdims=True)
                 + jnp.exp2(sink_b - mx_b))
        u["inv"] = jnp.where(lo_half, 1.0 / den_a, 1.0 / den_b)
        u["p_band"] = jnp.concatenate([pa, pb], axis=1).astype(BF16)
        u["p_meta"] = pm.astype(BF16)

    for key in units:
        j, g = key
        u = st[key]
        o = ((_dot(u["p_band"], u["vb"]) + _dot(u["p_meta"], meta[g][1])) * u["inv"]).astype(BF16)
        for i, s in enumerate((2 * g, 2 * g + 1)):
            o_ref[j * blk:(j + 1) * blk, s * LANES:(s + 1) * LANES] = o[i * blk:(i + 1) * blk]


def _attn(sinks, qkv, qkv_meta, batch, seq, qb):
    nb = seq // WINDOW
    steps = nb // qb
    n = batch * seq
    kvw = 4 * KV_W
    rows = qb * WINDOW
    qi = np.arange(2 * WINDOW)[:, None] % WINDOW
    c = np.arange(2 * WINDOW)[None, :]
    band_ok = np.where(c < WINDOW, c > qi, c - WINDOW <= qi)
    band_mask = jnp.asarray(np.where(band_ok, 0.0, NEG_INF).astype(np.float32))
    return pl.pallas_call(
        functools.partial(_attn_kernel, qb=qb),
        grid=(batch, steps),
        in_specs=[
            pl.BlockSpec(memory_space=pltpu.SMEM),
            pl.BlockSpec((rows, Q_W), lambda b, i: (b * steps + i, 0)),
            pl.BlockSpec((rows, kvw), lambda b, i: (b * steps + i, 1)),
            pl.BlockSpec((WINDOW, kvw), lambda b, i: (jnp.maximum((b * steps + i) * qb - 1, 0), 1)),
            pl.BlockSpec((N_META, kvw), lambda b, i: (0, 1)),
            _const_spec((2 * WINDOW, 2 * WINDOW)),
        ],
        out_specs=pl.BlockSpec((rows, Q_W), lambda b, i: (b * steps + i, 0)),
        out_shape=jax.ShapeDtypeStruct((n, Q_W), BF16),
        compiler_params=_params(2),
        name="attn",
    )(sinks, qkv, qkv, qkv, qkv_meta, band_mask)


def _emit_interleaved(*segment_lists):
    keyed = []
    for li, segs in enumerate(segment_lists):
        costs = [_SEGMENT_COST.get(seg.__name__, 1.0) for seg in segs]
        total, done = sum(costs), 0.0
        for j, (seg, cost) in enumerate(zip(segs, costs)):
            keyed.append(((done + 0.5 * cost) / total, li, j, seg))
            done += cost
    for _, _, _, seg in sorted(keyed, key=lambda t: t[:3]):
        seg()


_SEGMENT_COST = dict(
    lora=5, decay_gate=8, norms=5, cumulative=9, pair_prep=2.5,
    scores=6, square=3, neumann=6, values=3, solve=6,
    apply_state=5, outputs=4, update_state=7, centre=3, variance=2, finish=4)


def _rwkv_kernel(*refs, nc, pipelined):
    if pipelined:
        p0_ref, later_p_refs, next_p_ref = refs[0], refs[1:pipelined], refs[pipelined]
        (s0_ref, w0_ref, a0_ref, w2a_ref, g2_ref, kk_ref, ka_ref, rk_ref,
         lnw_ref, lnb_ref, bd_ref, tri_ref, eye_ref,
         y_ref, sfin_ref, s_scr, c_f32, c_pc, c_blk, c_rows, c_chunk) = refs[pipelined + 1:]
    else:
        (p0_ref, s0_ref, w0_ref, a0_ref, w2a_ref, g2_ref, kk_ref, ka_ref, rk_ref,
         lnw_ref, lnb_ref, bd_ref, tri_ref, eye_ref,
         y_ref, sfin_ref, s_scr) = refs
    c = pl.program_id(1)
    tok = nc * CHUNK
    o1, o2, o3 = RWKV_DIM, 2 * RWKV_DIM, 3 * RWKV_DIM
    pairs = [(gi, ri) for ri in range(nc) for gi in range(N_GROUPS)]

    bd = bd_ref[...]
    bd_f = bd.astype(F32)
    tri_mask = tri_ref[...]
    eye = eye_ref[...]
    lane = lax.broadcasted_iota(jnp.int32, (tok, LANES), 1)
    lane1 = lax.broadcasted_iota(jnp.int32, (1, LANES), 1)
    head_lo = (lane1 < HEAD_DIM).astype(BF16)
    head_hi = (lane1 >= HEAD_DIM).astype(BF16)
    zero_slab = jnp.zeros((CHUNK, LANES), BF16)

    def blk(x):
        xb = x.astype(BF16)
        left, right = xb[:, :LANES], xb[:, LANES:]
        return jnp.concatenate(
            [jnp.concatenate([left * head_lo, zero_slab], axis=1),
             jnp.concatenate([left * head_hi, zero_slab], axis=1),
             jnp.concatenate([zero_slab, right * head_lo], axis=1),
             jnp.concatenate([zero_slab, right * head_hi], axis=1)], axis=0)

    pos_in_chunk = lax.broadcasted_iota(jnp.int32, (tok, 1), 0) % CHUNK

    def seg_sums(xs):
        out = _dot(jnp.concatenate([x.astype(BF16) for x in xs], axis=0), bd)
        return [out[i * tok:(i + 1) * tok] for i in range(len(xs))]

    def grp(x, gi):
        return x[:, gi * GROUP:(gi + 1) * GROUP]

    def val(q, name):
        item = q[name]
        return item() if callable(item) else item

    def front_segments(p_ref, out, to_carry):
        f = {}

        def lora():
            pf = p_ref[...].reshape(tok, RW_W)
            f["r"], f["k"], f["v"] = pf[:, :o1], pf[:, o1:o2], pf[:, o2:o3]
            dwa = pf[:, o3:o3 + LANES]
            f["dg"] = pf[:, o3 + LANES:]
            z = jnp.where(lane < DECAY_LORA, jnp.tanh(dwa), dwa).astype(BF16)
            f["wa"] = _dot(z, w2a_ref[...])

        def decay_gate():
            zw = -(w0_ref[...] + f["wa"][:, :o1])
            softplus = jnp.maximum(zw, 0.0) + jnp.log(1.0 + jnp.exp(-jnp.abs(zw)))
            f["logw"] = -jnp.exp(-softplus - 0.5)
            f["a"] = _sigmoid(a0_ref[...] + f["wa"][:, o1:])
            f["gate"] = _dot(_sigmoid(f["dg"]).astype(BF16), g2_ref[...])

        def norms():
            f["kkv"] = f["k"] * kk_ref[...]
            f["kp"] = f["k"] * (1.0 + (f["a"] - 1.0) * ka_ref[...])
            rkk = f["r"] * f["kp"] * rk_ref[...]
            sq = f["kkv"] * f["kkv"]
            f["sums"] = seg_sums([grp(sq, 0), grp(sq, 1), grp(rkk, 0), grp(rkk, 1)])

        def cumulative():
            ss = jnp.concatenate(f["sums"][0:2], axis=1)
            f["bonus"] = jnp.concatenate(f["sums"][2:4], axis=1)
            kkn = f["kkv"] / jnp.maximum(jnp.sqrt(ss), 1e-12)
            f["aa"] = -kkn
            f["bb"] = kkn * f["a"]
            lp = f["logw"]
            step = 1
            while step < CHUNK:
                lp = lp + jnp.where(pos_in_chunk >= step, pltpu.roll(lp, step, 0), 0.0)
                step *= 2
            f["lps"] = [grp(lp, gi) for gi in range(N_GROUPS)]
            chunk = dict(gate=f["gate"], bonus=f["bonus"], v=f["v"])
            if to_carry is not None:
                c_chunk[to_carry, 0] = chunk["gate"]
                c_chunk[to_carry, 1] = chunk["bonus"]
                c_chunk[to_carry, 2] = chunk["v"]
            out["chunk"] = chunk
            out["pairs"] = {}

        def pair_prep(i, key):
            def run():
                gi, ri = key
                rows = slice(ri * CHUNK, (ri + 1) * CHUNK)
                cut = lambda x: x[rows, gi * GROUP:(gi + 1) * GROUP]
                lp = f["lps"][gi][rows]
                lw = cut(f["logw"])
                lpc = lp[CHUNK - 1:CHUNK, :]
                e_neg = jnp.exp(-lp)
                e_end = jnp.exp(lpc - lp)
                rg, kg, vg = cut(f["r"]), cut(f["kp"]), cut(f["v"])
                ag, bg = cut(f["aa"]), cut(f["bb"])
                at = ag * jnp.exp(lp - lw)
                rt = rg * jnp.exp(lp)
                q = dict(
                    at=at, vg=vg, pc=jnp.exp(lpc), vblk=blk(vg),
                    bt=blk(bg * e_neg), kt=blk(kg * e_neg),
                    ar=jnp.concatenate([at, rt], axis=0).astype(BF16),
                    bk_end=jnp.concatenate([bg * e_end, kg * e_end], axis=0).astype(BF16))
                if to_carry is not None:
                    c_f32[to_carry, i, 0], c_f32[to_carry, i, 1] = q["at"], q["vg"]
                    c_pc[to_carry, i, 0:1, :] = q["pc"]
                    c_blk[to_carry, i, 0], c_blk[to_carry, i, 1] = q["bt"], q["kt"]
                    c_blk[to_carry, i, 2] = q["vblk"]
                    c_rows[to_carry, i, 0], c_rows[to_carry, i, 1] = q["ar"], q["bk_end"]
                out["pairs"][key] = q
            run.__name__ = "pair_prep"
            return run

        return [lora, decay_gate, norms, cumulative] + [pair_prep(i, k) for i, k in enumerate(pairs)]

    def carried(slot):
        def pair(i):
            return dict(
                at=lambda: c_f32[slot, i, 0], vg=lambda: c_f32[slot, i, 1],
                pc=lambda: c_pc[slot, i, 0:1, :],
                bt=lambda: c_blk[slot, i, 0], kt=lambda: c_blk[slot, i, 1],
                vblk=lambda: c_blk[slot, i, 2],
                ar=lambda: c_rows[slot, i, 0], bk_end=lambda: c_rows[slot, i, 1])
        chunk = dict(gate=lambda: c_chunk[slot, 0], bonus=lambda: c_chunk[slot, 1],
                     v=lambda: c_chunk[slot, 2])
        return {key: pair(i) for i, key in enumerate(pairs)}, chunk

    def intra_segments(st):
        def stages(batch):
            def scores():
                for key in batch:
                    q = st[key]
                    ar = val(q, "ar")
                    sb = _dot_nt(ar, val(q, "bt")) * tri_mask
                    sk = _dot_nt(ar, val(q, "kt")) * tri_mask
                    q["a_ab"] = sb[:CHUNK]
                    q["a_rb"] = sb[CHUNK:].astype(BF16)
                    q["a_k"] = sk.astype(BF16)

            def square():
                for key in batch:
                    q = st[key]
                    q["pw"] = _dot(q["a_ab"].astype(BF16), blk(q["a_ab"]))
                    q["minv"] = eye + q["a_ab"]

            def neumann(j):
                def run():
                    for key in batch:
                        q = st[key]
                        if j < 5:
                            out = _dot(q["pw"].astype(BF16),
                                       jnp.concatenate([blk(q["pw"]), blk(q["minv"])], axis=1))
                            q["pw"] = out[:, :GROUP]
                            q["minv"] = q["minv"] + out[:, GROUP:]
                        else:
                            q["minv"] = q["minv"] + _dot(q["pw"].astype(BF16), blk(q["minv"]))
                run.__name__ = "neumann"
                return run

            def values():
                for key in batch:
                    q = st[key]
                    kv = _dot(q["a_k"], val(q, "vblk"))
                    q["av"] = kv[:CHUNK]
                    q["y_rk"] = kv[CHUNK:]

            def solve():
                for key in batch:
                    q = st[key]
                    wu = _dot(q["minv"].astype(BF16),
                              jnp.concatenate([blk(val(q, "at")), blk(q["av"])], axis=1))
                    q["w"] = wu[:, :GROUP].astype(BF16)
                    q["u0"] = wu[:, GROUP:]

            return [scores, square] + [neumann(j) for j in range(1, 6)] + [values, solve]

        segs = []
        for lo in range(0, len(pairs), RWKV_CHAINS):
            segs.extend(stages(pairs[lo:lo + RWKV_CHAINS]))
        return segs

    def state_back_segments(st, chunk, write_y):
        def apply_state():
            for key in pairs:
                gi, ri = key
                q = st[key]
                q["s"] = s_scr[ri, gi]
                rt = val(q, "ar")[CHUNK:]
                ws = _dot_nt(jnp.concatenate([q["w"], rt], axis=0), q["s"].astype(BF16))
                q["u"] = ws[:CHUNK] + q["u0"]
                q["y_rs"] = ws[CHUNK:]

        def outputs():
            for key in pairs:
                q = st[key]
                q["y"] = q["y_rs"] + _dot(q["a_rb"], blk(q["u"])) + q["y_rk"]

        def update_state():
            for key in pairs:
                gi, ri = key
                q = st[key]
                upd = _dot_tn(jnp.concatenate([q["u"], val(q, "vg")], axis=0).astype(BF16),
                              val(q, "bk_end"))
                s_scr[ri, gi] = q["s"] * val(q, "pc") + upd * bd_f

        g = {}

        def centre():
            g["ys"] = [jnp.concatenate([st[gi, ri]["y"] for ri in range(nc)], axis=0)
                       if nc > 1 else st[gi, 0]["y"] for gi in range(N_GROUPS)]
            means = seg_sums(g["ys"])
            g["ds"] = [g["ys"][i] - means[i] * (1.0 / HEAD_DIM) for i in range(N_GROUPS)]

        def variance():
            g["vars"] = seg_sums([d * d for d in g["ds"]])

        def finish():
            yn = jnp.concatenate(
                [g["ds"][i] * lax.rsqrt(g["vars"][i] * (1.0 / HEAD_DIM) + RWKV_LN_EPS)
                 for i in range(N_GROUPS)], axis=1)
            yn = yn * lnw_ref[...] + lnb_ref[...]
            out = (yn + val(chunk, "bonus") * val(chunk, "v")) * val(chunk, "gate")
            write_y(out.astype(BF16).reshape(nc, CHUNK, RWKV_DIM))

        return [apply_state, outputs, update_state, centre, variance, finish]

    def init_state():
        for ri in range(nc):
            s_scr[ri] = s0_ref[...]

    if pipelined:
        slot = c % 2

        @pl.when(c == 0)
        def _():
            init_state()
            _emit_interleaved(front_segments(p0_ref, {}, 0))

        def write_part(k):
            def write(y):
                y_ref[:, k * CHUNK:(k + 1) * CHUNK, :] = y
            return write

        cur, cur_chunk = carried(slot)
        done = None
        for k, p_ref in enumerate(later_p_refs):
            nxt = {}
            lists = [intra_segments(cur), front_segments(p_ref, nxt, None)]
            if done is not None:
                lists.append(state_back_segments(*done, write_part(k - 1)))
            _emit_interleaved(*lists)
            done = (cur, cur_chunk)
            cur, cur_chunk = nxt["pairs"], nxt["chunk"]
        last = len(later_p_refs)
        _emit_interleaved(intra_segments(cur),
                          state_back_segments(*done, write_part(last - 1)),
                          front_segments(next_p_ref, {}, 1 - slot))
        _emit_interleaved(state_back_segments(cur, cur_chunk, write_part(last)))
    else:
        pl.when(c == 0)(init_state)
        only = {}

        def write_all(y):
            y_ref[...] = y

        _emit_interleaved(front_segments(p0_ref, only, None))
        _emit_interleaved(intra_segments(only["pairs"]))
        _emit_interleaved(state_back_segments(only["pairs"], only["chunk"], write_all))

    @pl.when(c == pl.num_programs(1) - 1)
    def _():
        sfin_ref[...] = s_scr[...]


def _rwkv(p, s0, vecs, w2a, g2p, consts, batch, n_chunks, nc, pipelined):
    w0, a0, k_k, k_a, r_k, ln_w, ln_b = vecs
    bd, tri_mask, eye = consts
    tok = nc * CHUNK
    n_pairs = nc * N_GROUPS
    vec = _const_spec((1, RWKV_DIM))
    chunk_spec = lambda index: pl.BlockSpec((nc, CHUNK, RW_W), index)
    if pipelined:
        per_step = pipelined
        steps = n_chunks // per_step
        p_specs = ([pl.BlockSpec((nc, CHUNK, RW_W), lambda b, c: (b, 0, 0),
                                 pipeline_mode=pl.Buffered(1))]
                   + [chunk_spec(functools.partial(lambda b, c, k: (b, per_step * c + k, 0), k=k))
                      for k in range(1, per_step)]
                   + [chunk_spec(lambda b, c: (b, jnp.minimum(per_step * (c + 1), n_chunks - 1), 0))])
        p_args = [p] * (per_step + 1)
        y_rows = per_step * CHUNK
        carry = [pltpu.VMEM((2, n_pairs, 2, CHUNK, GROUP), F32),
                 pltpu.VMEM((2, n_pairs, 8, GROUP), F32),
                 pltpu.VMEM((2, n_pairs, 3, GROUP, GROUP), BF16),
                 pltpu.VMEM((2, n_pairs, 2, 2 * CHUNK, GROUP), BF16),
                 pltpu.VMEM((2, 3, tok, RWKV_DIM), F32)]
    else:
        steps = n_chunks
        p_specs = [chunk_spec(lambda b, c: (b, c, 0))]
        p_args = [p]
        y_rows = CHUNK
        carry = []
    return pl.pallas_call(
        functools.partial(_rwkv_kernel, nc=nc, pipelined=pipelined),
        grid=(batch // nc, steps),
        in_specs=p_specs + [
            _const_spec((N_GROUPS, GROUP, GROUP)),
            vec, vec,
            _const_spec((LANES, 2 * RWKV_DIM)),
            _const_spec((GATE_PAD, RWKV_DIM)),
            vec, vec, vec, vec, vec,
            _const_spec((GROUP, GROUP)),
            _const_spec((2 * CHUNK, GROUP)),
            _const_spec((CHUNK, GROUP)),
        ],
        out_specs=[
            pl.BlockSpec((nc, y_rows, RWKV_DIM), lambda b, c: (b, c, 0)),
            pl.BlockSpec((nc, N_GROUPS, GROUP, GROUP), lambda b, c: (b, 0, 0, 0),
                         pipeline_mode=pl.Buffered(1)),
        ],
        out_shape=[
            jax.ShapeDtypeStruct((batch, n_chunks * CHUNK, RWKV_DIM), BF16),
            jax.ShapeDtypeStruct((batch, N_GROUPS, GROUP, GROUP), F32),
        ],
        scratch_shapes=[pltpu.VMEM((nc, N_GROUPS, GROUP, GROUP), F32)] + carry,
        compiler_params=_params(2),
        name="rwkv",
    )(*p_args, s0, w0, a0, w2a, g2p, k_k, k_a, r_k, ln_w, ln_b, bd, tri_mask, eye)


def _post_kernel(ya_ref, yr_ref, gate_ref, x_ref, wba_ref, wbr_ref, wo_ref, gf_ref,
                 wg_ref, wu_ref, wd_ref, gn_ref, o_ref, *, parts):
    pm = o_ref.shape[0] // parts
    rows = [slice(i * pm, (i + 1) * pm) for i in range(parts)]

    def rms(h, g_ref):
        ms = jnp.mean(h * h, axis=-1, keepdims=True)
        return h * lax.rsqrt(ms + RMS_EPS) * g_ref[...]

    merged = []
    for r in rows:
        gates = gate_ref[r, :].astype(F32)
        merged.append((gates[:, :D_MODEL] * _dot(ya_ref[r, :], wba_ref[...])
                       + gates[:, D_MODEL:] * _dot(yr_ref[r, :], wbr_ref[...])).astype(BF16))
    h1 = [x_ref[r, :] + _dot(m, wo_ref[...]) for r, m in zip(rows, merged)]
    acts = []
    for h in h1:
        f = rms(h, gf_ref).astype(BF16)
        gt = _dot(f, wg_ref[...])
        up = _dot(f, wu_ref[...])
        acts.append((gt * _sigmoid(gt) * up).astype(BF16))
    h2 = [h + _dot(a, wd_ref[...]) for h, a in zip(h1, acts)]
    for r, h in zip(rows, h2):
        o_ref[r, :] = rms(h, gn_ref)


def _post(ya, yr, gates, x2, wba, wbr, wo, gf, wg, wu, wd, gn, tm):
    n = x2.shape[0]
    row = lambda i: (i, 0)
    return pl.pallas_call(
        functools.partial(_post_kernel, parts=POST_PARTS),
        grid=(n // tm,),
        in_specs=[
            pl.BlockSpec((tm, Q_W), row),
            pl.BlockSpec((tm, RWKV_DIM), row),
            pl.BlockSpec((tm, 2 * D_MODEL), row),
            pl.BlockSpec((tm, D_MODEL), row),
            _const_spec((Q_W, D_MODEL)),
            _const_spec((RWKV_DIM, D_MODEL)),
            _const_spec((D_MODEL, D_MODEL)),
            _const_spec((1, D_MODEL)),
            _const_spec((D_MODEL, D_FF)),
            _const_spec((D_MODEL, D_FF)),
            _const_spec((D_FF, D_MODEL)),
            _const_spec((1, D_MODEL)),
        ],
        out_specs=pl.BlockSpec((tm, D_MODEL), row),
        out_shape=jax.ShapeDtypeStruct((n, D_MODEL), F32),
        compiler_params=_params(1),
        name="post",
    )(ya, yr, gates, x2, wba, wbr, wo, gf, wg, wu, wd, gn)


def _rope_tables(first_pos, n):
    half = ROPE_DIM // 2
    f32 = np.float32
    inv_freq = np.power(f32(ROPE_THETA), -np.arange(half, dtype=f32) * f32(2.0 / ROPE_DIM))
    pos = (first_pos + np.arange(n)).astype(f32)
    ang = (pos[:, None] * inv_freq[None, :]).astype(f32)
    cos, sin = np.cos(ang).astype(f32), np.sin(ang).astype(f32)
    one = np.ones((n, HEAD_DIM - ROPE_DIM), f32)
    zero8 = np.zeros((n, half), f32)
    zero48 = np.zeros((n, HEAD_DIM - ROPE_DIM), f32)
    c_head = np.concatenate([cos, cos, one], axis=1)
    s1_head = np.concatenate([zero8, sin, zero48], axis=1)
    s2_head = np.concatenate([-sin, zero8, zero48], axis=1)
    dup = lambda t: jnp.asarray(np.concatenate([t, t], axis=1))
    return dup(c_head), dup(s1_head), dup(s2_head)


def _rwkv_constants():
    t = np.arange(CHUNK)
    hh = np.arange(GROUP) // HEAD_DIM
    bd = (hh[:, None] == hh[None, :]).astype(np.float32)
    s = np.arange(GROUP) % CHUNK
    strict = (s[None, :] < t[:, None]).astype(np.float32)
    incl = (s[None, :] <= t[:, None]).astype(np.float32)
    eye = (s[None, :] == t[:, None]).astype(np.float32)
    return (jnp.asarray(bd, BF16), jnp.asarray(np.concatenate([strict, incl], axis=0)),
            jnp.asarray(eye))


def kernel(x, meta_tokens, norm_mix_g, w_in, b_in, attn_sinks, rwkv_mix, rwkv_w0, rwkv_w2, rwkv_a0, rwkv_a2, rwkv_g2, rwkv_k_k, rwkv_k_a, rwkv_r_k, rwkv_ln_w, rwkv_ln_b, w_br_attn, w_br_rwkv, w_o, norm_ffn_g, w_ffn_gate, w_ffn_up, w_ffn_down, norm_final_g):
    batch, seq, _ = x.shape
    layer = 0
    x2 = x.reshape(batch * seq, D_MODEL)

    w = w_in[layer]
    b = b_in[layer]
    scale = LOG2_E * HEAD_DIM ** -0.5

    col_scale = jnp.where(jnp.arange(w.shape[-1]) < Q_W, scale, 1.0).astype(F32)

    def rw_cols(t, n_lead):
        pad = jnp.zeros(t.shape[:n_lead] + (GATE_PAD - GATE_LORA,), t.dtype)
        return jnp.concatenate([t, pad], axis=-1)

    wb = (w * col_scale).astype(BF16)
    b = b * col_scale
    wq = wb[:, :ATTN_PROJ]
    bq = b[:ATTN_PROJ][None]
    wr = rw_cols(wb[:, ATTN_PROJ:ATTN_PROJ + RWKV_PROJ], 1)
    br = rw_cols(b[ATTN_PROJ:ATTN_PROJ + RWKV_PROJ], 0)[None]
    wg = wb[:, ATTN_PROJ + RWKV_PROJ:]
    bg = b[ATTN_PROJ + RWKV_PROJ:][None]
    g_mix = norm_mix_g[layer][None]

    mix = rw_cols(rwkv_mix[layer], 0)[None]
    zl = jnp.zeros((DECAY_LORA, RWKV_DIM), F32)
    w2a = jnp.concatenate(
        [jnp.concatenate([rwkv_w2[layer], zl], axis=1),
         jnp.concatenate([zl, rwkv_a2[layer]], axis=1)], axis=0).astype(BF16)
    g2p = jnp.concatenate(
        [rwkv_g2[layer], jnp.zeros((GATE_PAD - GATE_LORA, RWKV_DIM), F32)], axis=0).astype(BF16)
    vecs = (rwkv_w0[layer][None], rwkv_a0[layer][None], rwkv_k_k[layer][None],
            rwkv_k_a[layer][None], rwkv_r_k[layer].reshape(1, RWKV_DIM),
            rwkv_ln_w[layer][None], rwkv_ln_b[layer][None])

    cos_m, s1_m, s2_m = _rope_tables(0, N_META)
    qkv_m, p_m, _, p_m_tail = _inproj(meta_tokens, g_mix, wq, wr, wg, bq, br, bg,
                                      cos_m, s1_m, s2_m, jnp.zeros((8, RW_W), F32), mix, N_META)
    p_m_pad = jnp.concatenate([jnp.zeros((CHUNK - N_META, RW_W), F32), p_m], axis=0)
    zero_state = jnp.zeros((N_GROUPS, GROUP, GROUP), F32)
    _, s_meta = _rwkv(p_m_pad[None], zero_state, vecs, w2a, g2p, _rwkv_constants(), 1, 1, 1, 0)

    cos, s1, s2 = _rope_tables(N_META, seq)
    tm_in = min(512, seq)
    qkv, p, gates, _ = _inproj(x2, g_mix, wq, wr, wg, bq, br, bg, cos, s1, s2,
                               p_m_tail, mix, tm_in)
    y_attn = _attn(attn_sinks[layer] * LOG2_E, qkv, qkv_m, batch, seq, ATTN_BLOCKS_PER_STEP)
    nc = min(RWKV_ROWS_PER_STEP, batch)
    y_rwkv, _ = _rwkv(p.reshape(batch, seq, RW_W), s_meta[0], vecs, w2a, g2p,
                      _rwkv_constants(), batch, seq // CHUNK, nc, RWKV_CHUNKS_PER_STEP)
    out = _post(y_attn, y_rwkv.reshape(batch * seq, RWKV_DIM), gates, x2,
                w_br_attn[layer].astype(BF16), w_br_rwkv[layer].astype(BF16),
                w_o[layer].astype(BF16), norm_ffn_g[layer][None],
                w_ffn_gate[layer].astype(BF16), w_ffn_up[layer].astype(BF16),
                w_ffn_down[layer].astype(BF16), norm_final_g[None], min(POST_ROWS, seq))
    return out.reshape(batch, seq, D_MODEL)
```
